```python
import jax, jax.numpy as jnp
from jax import lax
import numpy as np

D_MODEL = 1024
BATCH = 8
SEQ = 2048
DEPTH = 4
DEC_BATCH = 128
DEC_SEQ = 1
PAST_LEN = 16384
PAGE_SIZE = 128

MIX_DIM = 2 * D_MODEL
RWKV_DIM = MIX_DIM // 2
RWKV_HEAD = 64
RWKV_HEADS = RWKV_DIM // RWKV_HEAD
DECAY_LORA = 64
AAA_LORA = 64
GATE_LORA = 160
RWKV_PROJ = 3 * RWKV_DIM + DECAY_LORA + AAA_LORA + GATE_LORA
RWKV_GN_EPS = 64e-5
SC_DIM = MIX_DIM // 4
SC_WIDTH = 3
SC_PROJ = 3 * SC_DIM
SSM_DIM = MIX_DIM // 4
SSM_HEADDIM = 64
SSM_HEADS = SSM_DIM // SSM_HEADDIM
SSM_GROUPS = 2
SSM_STATE = 128
SSM_CONV = 4
SSM_CONV_DIM = SSM_DIM + 2 * SSM_GROUPS * SSM_STATE
SSM_PROJ = SSM_DIM + SSM_CONV_DIM + SSM_HEADS
SSD_CHUNK = 128
IN_COLS = RWKV_PROJ + SC_PROJ + SSM_PROJ
D_FF = 2816
NORM_EPS = 1e-6

kernel_name = 'hymba_rwkv7_shortconv_mamba2_macaron_step'


def rmsnorm(x, g):
    xf = x.astype(jnp.float32)
    y = xf * lax.rsqrt(jnp.mean(xf * xf, axis=-1, keepdims=True) + NORM_EPS)
    return (y * g.astype(jnp.float32)).astype(x.dtype)


def swiglu(x, w_in, w_out):
    gate, up = jnp.split(x @ w_in, 2, axis=-1)
    return (jax.nn.silu(gate) * up) @ w_out


def causal_depthwise_conv(u, buf, w):
    width = w.shape[0]
    L = u.shape[1]
    padded = jnp.concatenate([buf.astype(u.dtype), u], axis=1)
    out = sum(padded[:, j:j + L] * w[j] for j in range(width))
    return out, padded[:, L:]


def rwkv7_mix(proj, shift_prev, wkv0, mu, w0, w2, a0, a2, g2, k_k, k_a, r_k, ln_w, ln_b):
    f32 = jnp.float32
    Bsz, L, _ = proj.shape
    prev = jnp.concatenate([shift_prev[:, None].astype(proj.dtype), proj[:, :-1]], axis=1)
    xs = proj + (prev - proj) * mu
    new_shift = proj[:, -1]
    r, k, v, w_lr, a_lr, g_lr = jnp.split(
        xs, [RWKV_DIM, 2 * RWKV_DIM, 3 * RWKV_DIM, 3 * RWKV_DIM + DECAY_LORA,
             3 * RWKV_DIM + DECAY_LORA + AAA_LORA], axis=-1)
    w = -jax.nn.softplus(-(w0 + jnp.tanh(w_lr) @ w2).astype(f32)) - 0.5
    decay = jnp.exp(-jnp.exp(w))
    a = jax.nn.sigmoid((a0 + a_lr @ a2).astype(f32))
    g = jax.nn.sigmoid(g_lr) @ g2
    heads = lambda t: t.astype(f32).reshape(Bsz, L, RWKV_HEADS, RWKV_HEAD)
    r, k, v, decay, a = heads(r), heads(k), heads(v), heads(decay), heads(a)
    kk = k * k_k.astype(f32).reshape(RWKV_HEADS, RWKV_HEAD)
    kk = kk * lax.rsqrt(jnp.maximum(jnp.sum(kk * kk, axis=-1, keepdims=True), 1e-24))
    k = k * (1.0 + (a - 1.0) * k_a.astype(f32).reshape(RWKV_HEADS, RWKV_HEAD))
    b = kk * a

    def step(S, inp):
        r_t, w_t, k_t, v_t, kk_t, b_t = inp
        Sa = jnp.einsum('bhij,bhj->bhi', S, -kk_t)
        S = S * w_t[:, :, None, :] + Sa[..., None] * b_t[:, :, None, :] + v_t[..., None] * k_t[:, :, None, :]
        return S, jnp.einsum('bhij,bhj->bhi', S, r_t)

    seq_in = tuple(jnp.moveaxis(t, 1, 0) for t in (r, decay, k, v, kk, b))
    S_final, y = lax.scan(step, wkv0.astype(f32), seq_in)
    y = jnp.moveaxis(y, 0, 1)
    mean = jnp.mean(y, axis=-1, keepdims=True)
    var = jnp.mean(jnp.square(y - mean), axis=-1, keepdims=True)
    y = ((y - mean) * lax.rsqrt(var + RWKV_GN_EPS)).reshape(Bsz, L, RWKV_DIM)
    y = y * ln_w.astype(f32) + ln_b.astype(f32)
    bonus = jnp.sum(r * k * r_k.astype(f32), axis=-1, keepdims=True) * v
    y = (y + bonus.reshape(Bsz, L, RWKV_DIM)) * g.astype(f32)
    return y.astype(proj.dtype), new_shift, S_final.astype(wkv0.dtype)


def short_conv_mix(proj, buf, conv_w, norm_g):
    b_gate, c_gate, xt = jnp.split(proj, 3, axis=-1)
    conv, new_buf = causal_depthwise_conv(c_gate * xt, buf, conv_w)
    return rmsnorm(b_gate * conv, norm_g), new_buf


def ssd_scan(x, dt, A, Bm, Cm, h0):
    Bsz, L, H, P = x.shape
    Q = min(SSD_CHUNK, L)
    nc = -(-L // Q)
    pad = nc * Q - L
    if pad:
        padw = lambda t: jnp.pad(t, [(0, 0), (0, pad)] + [(0, 0)] * (t.ndim - 2))
        x, dt, Bm, Cm = padw(x), padw(dt), padw(Bm), padw(Cm)
    chunk = lambda t: t.reshape((Bsz, nc, Q) + t.shape[2:])
    x, dt, Bm, Cm = chunk(x), chunk(dt), chunk(Bm), chunk(Cm)
    cum = jnp.cumsum(dt * A, axis=2)
    seg = cum[:, :, :, None, :] - cum[:, :, None, :, :]
    causal = jnp.tril(jnp.ones((Q, Q), dtype=bool))[None, None, :, :, None]
    decay_mask = jnp.exp(jnp.where(causal, seg, -jnp.inf))
    xdt = x * dt[..., None]
    scores = jnp.einsum('bcihn,bcjhn->bcijh', Cm, Bm) * decay_mask
    y_diag = jnp.einsum('bcijh,bcjhp->bcihp', scores, xdt)
    decay_end = jnp.exp(cum[:, :, -1:, :] - cum)
    chunk_states = jnp.einsum('bcjhn,bcjh,bcjhp->bchpn', Bm, decay_end, xdt)
    chunk_decay = jnp.exp(cum[:, :, -1, :])

    def step(h, inp):
        s, d = inp
        return h * d[:, :, None, None] + s, h

    h_final, h_prev = lax.scan(step, h0, (jnp.moveaxis(chunk_states, 1, 0), jnp.moveaxis(chunk_decay, 1, 0)))
    h_prev = jnp.moveaxis(h_prev, 0, 1)
    y_off = jnp.einsum('bcihn,bchpn,bcih->bcihp', Cm, h_prev, jnp.exp(cum))
    y = (y_diag + y_off).reshape(Bsz, nc * Q, H, P)[:, :L]
    return y, h_final


def mamba2_mix(proj, conv_buf, h0, conv_w, conv_b, dt_bias, A_log, D, norm_g):
    f32 = jnp.float32
    Bsz, L, _ = proj.shape
    z, xbc, dt = jnp.split(proj, [SSM_DIM, SSM_DIM + SSM_CONV_DIM], axis=-1)
    xbc_c, new_buf = causal_depthwise_conv(xbc, conv_buf, conv_w)
    xbc_c = jax.nn.silu(xbc_c + conv_b)
    xh, Bm, Cm = jnp.split(xbc_c, [SSM_DIM, SSM_DIM + SSM_GROUPS * SSM_STATE], axis=-1)
    xh = xh.astype(f32).reshape(Bsz, L, SSM_HEADS, SSM_HEADDIM)
    rep = SSM_HEADS // SSM_GROUPS
    Bm = jnp.repeat(Bm.astype(f32).reshape(Bsz, L, SSM_GROUPS, SSM_STATE), rep, axis=2)
    Cm = jnp.repeat(Cm.astype(f32).reshape(Bsz, L, SSM_GROUPS, SSM_STATE), rep, axis=2)
    dt = jax.nn.softplus((dt + dt_bias).astype(f32))
    A = -jnp.exp(A_log.astype(f32))
    y, h_final = ssd_scan(xh, dt, A, Bm, Cm, h0.astype(f32))
    y = (y + D.astype(f32)[:, None] * xh).reshape(Bsz, L, SSM_DIM)
    y = y * jax.nn.silu(z.astype(f32))
    yg = y.reshape(Bsz, L, SSM_GROUPS, SSM_DIM // SSM_GROUPS)
    yg = yg * lax.rsqrt(jnp.mean(yg * yg, axis=-1, keepdims=True) + NORM_EPS)
    y = yg.reshape(Bsz, L, SSM_DIM) * norm_g.astype(f32)
    return y.astype(proj.dtype), new_buf, h_final.astype(h0.dtype)


def decoder_layer(x, shift, wkv, sc_buf, ssm_conv, ssm, lp):
    x = x + 0.5 * swiglu(rmsnorm(x, lp['norm_ffn1']), lp['ffn1_w_in'], lp['ffn1_w_out'])
    h = rmsnorm(x, lp['norm_mix'])
    proj = h @ lp['w_in']
    p_rwkv, p_sc, p_ssm = jnp.split(proj, [RWKV_PROJ, RWKV_PROJ + SC_PROJ], axis=-1)
    y_rwkv, shift, wkv = rwkv7_mix(p_rwkv, shift, wkv, lp['rwkv_mu'], lp['rwkv_w0'], lp['rwkv_w2'],
                                   lp['rwkv_a0'], lp['rwkv_a2'], lp['rwkv_g2'], lp['rwkv_k_k'],
                                   lp['rwkv_k_a'], lp['rwkv_r_k'], lp['rwkv_ln_w'], lp['rwkv_ln_b'])
    y_sc, sc_buf = short_conv_mix(p_sc, sc_buf, lp['sc_conv_w'], lp['sc_norm'])
    y_ssm, ssm_conv, ssm = mamba2_mix(p_ssm, ssm_conv, ssm, lp['ssm_conv_w'], lp['ssm_conv_b'],
                                      lp['ssm_dt_bias'], lp['ssm_A_log'], lp['ssm_D'], lp['ssm_norm'])
    x = x + jnp.concatenate([y_rwkv, y_sc, y_ssm], axis=-1) @ lp['w_out']
    x = x + 0.5 * swiglu(rmsnorm(x, lp['norm_ffn2']), lp['ffn2_w_in'], lp['ffn2_w_out'])
    return x, (shift, wkv, sc_buf, ssm_conv, ssm)


def run_trunk(x, states, weights, norm_final):
    new = ([], [], [], [], [])
    for i in range(DEPTH):
        lp = {name: w[i] for name, w in weights.items()}
        x, layer_states = decoder_layer(x, states[0][i], states[1][i], states[2][i], states[3][i], states[4][i], lp)
        for lst, s in zip(new, layer_states):
            lst.append(s)
    return rmsnorm(x, norm_final), tuple(jnp.stack(lst) for lst in new)


def setup_inputs(seed: int = 0) -> dict:
    key = jax.random.key(seed)
    keys = iter(jax.random.split(key, 64))
    nrm = lambda shape, scale=1.0: scale * jax.random.normal(next(keys), shape, jnp.float32)
    gain = lambda shape: 1.0 + nrm(shape, 0.02)
    uni = lambda shape, lo, hi: jax.random.uniform(next(keys), shape, jnp.float32, lo, hi)
    dt_init = jnp.exp(uni((DEPTH, SSM_HEADS), float(np.log(1e-3)), float(np.log(1e-1))))
    return {
        'x_prompt': nrm((BATCH, SEQ, D_MODEL)),
        'x_sample': nrm((DEC_BATCH, DEC_SEQ, D_MODEL)),
        'state_rwkv_shift': nrm((DEPTH, DEC_BATCH, RWKV_PROJ)),
        'state_rwkv_wkv': nrm((DEPTH, DEC_BATCH, RWKV_HEADS, RWKV_HEAD, RWKV_HEAD), 0.5),
        'state_sc_buf': nrm((DEPTH, DEC_BATCH, SC_WIDTH - 1, SC_DIM)),
        'state_ssm_conv': nrm((DEPTH, DEC_BATCH, SSM_CONV - 1, SSM_CONV_DIM)),
        'state_ssm': nrm((DEPTH, DEC_BATCH, SSM_HEADS, SSM_HEADDIM, SSM_STATE), 0.5),
        'norm_ffn1': gain((DEPTH, D_MODEL)),
        'ffn1_w_in': nrm((DEPTH, D_MODEL, 2 * D_FF), D_MODEL ** -0.5),
        'ffn1_w_out': nrm((DEPTH, D_FF, D_MODEL), D_FF ** -0.5),
        'norm_mix': gain((DEPTH, D_MODEL)),
        'w_in': nrm((DEPTH, D_MODEL, IN_COLS), D_MODEL ** -0.5),
        'rwkv_mu': uni((DEPTH, RWKV_PROJ), 0.0, 1.0),
        'rwkv_w0': uni((DEPTH, RWKV_DIM), -6.0, 0.0),
        'rwkv_w2': nrm((DEPTH, DECAY_LORA, RWKV_DIM), 0.5 * DECAY_LORA ** -0.5),
        'rwkv_a0': nrm((DEPTH, RWKV_DIM), 0.1),
        'rwkv_a2': nrm((DEPTH, AAA_LORA, RWKV_DIM), AAA_LORA ** -0.5),
        'rwkv_g2': nrm((DEPTH, GATE_LORA, RWKV_DIM), GATE_LORA ** -0.5),
        'rwkv_k_k': 0.85 + nrm((DEPTH, RWKV_DIM), 0.05),
        'rwkv_k_a': 1.0 + nrm((DEPTH, RWKV_DIM), 0.05),
        'rwkv_r_k': nrm((DEPTH, RWKV_HEADS, RWKV_HEAD), 0.1),
        'rwkv_ln_w': gain((DEPTH, RWKV_DIM)),
        'rwkv_ln_b': nrm((DEPTH, RWKV_DIM), 0.02),
        'sc_conv_w': nrm((DEPTH, SC_WIDTH, SC_DIM), SC_WIDTH ** -0.5),
        'sc_norm': gain((DEPTH, SC_DIM)),
        'ssm_conv_w': nrm((DEPTH, SSM_CONV, SSM_CONV_DIM), SSM_CONV ** -0.5),
        'ssm_conv_b': nrm((DEPTH, SSM_CONV_DIM), 0.02),
        'ssm_dt_bias': dt_init + jnp.log(-jnp.expm1(-dt_init)),
        'ssm_A_log': jnp.log(uni((DEPTH, SSM_HEADS), 1.0, 16.0)),
        'ssm_D': 1.0 + nrm((DEPTH, SSM_HEADS), 0.1),
        'ssm_norm': gain((DEPTH, SSM_DIM)),
        'w_out': nrm((DEPTH, MIX_DIM, D_MODEL), MIX_DIM ** -0.5),
        'norm_ffn2': gain((DEPTH, D_MODEL)),
        'ffn2_w_in': nrm((DEPTH, D_MODEL, 2 * D_FF), D_MODEL ** -0.5),
        'ffn2_w_out': nrm((DEPTH, D_FF, D_MODEL), D_FF ** -0.5),
        'norm_final': gain((D_MODEL,)),
    }


def reference(x_prompt, x_sample, state_rwkv_shift, state_rwkv_wkv, state_sc_buf, state_ssm_conv, state_ssm,
              norm_ffn1, ffn1_w_in, ffn1_w_out, norm_mix, w_in, rwkv_mu, rwkv_w0, rwkv_w2, rwkv_a0, rwkv_a2,
              rwkv_g2, rwkv_k_k, rwkv_k_a, rwkv_r_k, rwkv_ln_w, rwkv_ln_b, sc_conv_w, sc_norm, ssm_conv_w,
              ssm_conv_b, ssm_dt_bias, ssm_A_log, ssm_D, ssm_norm, w_out, norm_ffn2, ffn2_w_in, ffn2_w_out,
              norm_final):
    weights = {
        'norm_ffn1': norm_ffn1, 'ffn1_w_in': ffn1_w_in, 'ffn1_w_out': ffn1_w_out, 'norm_mix': norm_mix,
        'w_in': w_in, 'rwkv_mu': rwkv_mu, 'rwkv_w0': rwkv_w0, 'rwkv_w2': rwkv_w2, 'rwkv_a0': rwkv_a0,
        'rwkv_a2': rwkv_a2, 'rwkv_g2': rwkv_g2, 'rwkv_k_k': rwkv_k_k, 'rwkv_k_a': rwkv_k_a,
        'rwkv_r_k': rwkv_r_k, 'rwkv_ln_w': rwkv_ln_w, 'rwkv_ln_b': rwkv_ln_b, 'sc_conv_w': sc_conv_w,
        'sc_norm': sc_norm, 'ssm_conv_w': ssm_conv_w, 'ssm_conv_b': ssm_conv_b, 'ssm_dt_bias': ssm_dt_bias,
        'ssm_A_log': ssm_A_log, 'ssm_D': ssm_D, 'ssm_norm': ssm_norm, 'w_out': w_out,
        'norm_ffn2': norm_ffn2, 'ffn2_w_in': ffn2_w_in, 'ffn2_w_out': ffn2_w_out,
    }
    nb, dt_ = x_prompt.shape[0], x_prompt.dtype
    zero_states = (
        jnp.zeros((DEPTH, nb, RWKV_PROJ), dt_),
        jnp.zeros((DEPTH, nb, RWKV_HEADS, RWKV_HEAD, RWKV_HEAD), dt_),
        jnp.zeros((DEPTH, nb, SC_WIDTH - 1, SC_DIM), dt_),
        jnp.zeros((DEPTH, nb, SSM_CONV - 1, SSM_CONV_DIM), dt_),
        jnp.zeros((DEPTH, nb, SSM_HEADS, SSM_HEADDIM, SSM_STATE), dt_),
    )
    y_prompt, p_states = run_trunk(x_prompt, zero_states, weights, norm_final)
    sample_states = (state_rwkv_shift, state_rwkv_wkv, state_sc_buf, state_ssm_conv, state_ssm)
    y_sample, s_states = run_trunk(x_sample, sample_states, weights, norm_final)
    p_rwkv_shift, p_rwkv_wkv, p_sc_buf, p_ssm_conv, p_ssm = p_states
    s_rwkv_shift, s_rwkv_wkv, s_sc_buf, s_ssm_conv, s_ssm = s_states
    return (y_prompt, y_sample, p_rwkv_shift, p_rwkv_wkv, p_sc_buf, p_ssm_conv, p_ssm,
            s_rwkv_shift, s_rwkv_wkv, s_sc_buf, s_ssm_conv, s_ssm)
```

```python
import functools

import jax
import jax.numpy as jnp
from jax import lax
from jax.experimental import pallas as pl
from jax.experimental.pallas import tpu as pltpu

F32 = jnp.float32
MXU_DTYPE = jnp.bfloat16
HIGHEST = lax.Precision.HIGHEST

D_MODEL = 1024
D_FF = 2816
RWKV_DIM = 1024
RWKV_HEAD = 64
RWKV_HEADS = 16
DECAY_LORA = 64
AAA_LORA = 64
GATE_LORA = 160
RWKV_PROJ = 3 * RWKV_DIM + DECAY_LORA + AAA_LORA + GATE_LORA
RWKV_GN_EPS = 64e-5
SC_DIM = 512
SC_WIDTH = 3
SSM_DIM = 512
SSM_HEADDIM = 64
SSM_HEADS = 8
SSM_GROUPS = 2
SSM_STATE = 128
SSM_CONV = 4
SSM_CONV_DIM = SSM_DIM + 2 * SSM_GROUPS * SSM_STATE
NORM_EPS = 1e-6

COL_RKV = 0
COL_SC = 3 * RWKV_DIM
COL_Z = COL_SC + 3 * SC_DIM
COL_XBC = COL_Z + SSM_DIM
COL_SMALL = COL_XBC + SSM_CONV_DIM
SMALL_W = 512
SM_GL = 128
SM_DT = 384
PROJ_COLS = COL_SMALL + SMALL_W

LANE = 128
SUBLANE = 8
VMEM_LIMIT = 56 * 1024 * 1024

RWKV_CHUNK = 64
SSD_CHUNK = 128


def _mm(a, b):
    return jnp.dot(a.astype(MXU_DTYPE), b.astype(MXU_DTYPE), preferred_element_type=F32)


def _mm_nt(a, b):
    return lax.dot_general(a.astype(MXU_DTYPE), b.astype(MXU_DTYPE), (((1,), (1,)), ((), ())),
                           preferred_element_type=F32)


def _mm_tn(a, b):
    return lax.dot_general(a.astype(MXU_DTYPE), b.astype(MXU_DTYPE), (((0,), (0,)), ((), ())),
                           preferred_element_type=F32)


def _mmx(a, b):
    return jnp.dot(a, b, precision=HIGHEST, preferred_element_type=F32)


def _mmx_nt(a, b):
    return lax.dot_general(a, b, (((1,), (1,)), ((), ())), precision=HIGHEST, preferred_element_type=F32)


def _mmx_tn(a, b):
    return lax.dot_general(a, b, (((0,), (0,)), ((), ())), precision=HIGHEST, preferred_element_type=F32)


def _sigmoid(x):
    return 1.0 / (1.0 + jnp.exp(-x))


def _silu(x):
    return x * _sigmoid(x)


def _softplus(x):
    return jnp.maximum(x, 0.0) + jnp.log1p(jnp.exp(-jnp.abs(x)))


def _rmsnorm(x, g):
    return x * lax.rsqrt(jnp.mean(x * x, axis=-1, keepdims=True) + NORM_EPS) * g


def _params(*sem):
    return pltpu.CompilerParams(dimension_semantics=sem, vmem_limit_bytes=VMEM_LIMIT)


def _ffn_kernel(x_ref, g_ref, wg_ref, wu_ref, wo_ref, gf_ref, o_ref, h_ref, acc_ref, *, final_norm):
    j = pl.program_id(1)

    @pl.when(j == 0)
    def _():
        h_ref[...] = _rmsnorm(x_ref[...], g_ref[...]).astype(h_ref.dtype)
        acc_ref[...] = jnp.zeros_like(acc_ref)

    h = h_ref[...]
    gate = jnp.dot(h, wg_ref[...], preferred_element_type=F32)
    up = jnp.dot(h, wu_ref[...], preferred_element_type=F32)
    act = (_silu(gate) * up).astype(MXU_DTYPE)
    acc_ref[...] += jnp.dot(act, wo_ref[...], preferred_element_type=F32)

    @pl.when(j == pl.num_programs(1) - 1)
    def _():
        y = x_ref[...] + 0.5 * acc_ref[...]
        if final_norm:
            y = _rmsnorm(y, gf_ref[...])
        o_ref[...] = y


def _ffn(x, norm_g, w_in, w_out, layer, final_g=None, *, tm, tf):
    m = x.shape[0]
    nf = D_FF // tf
    final_norm = final_g is not None
    gf = final_g if final_norm else norm_g[layer]
    return pl.pallas_call(
        functools.partial(_ffn_kernel, final_norm=final_norm),
        grid=(m // tm, nf),
        in_specs=[
            pl.BlockSpec((tm, D_MODEL), lambda i, j: (i, 0)),
            pl.BlockSpec((None, 1, D_MODEL), lambda i, j: (layer, 0, 0)),
            pl.BlockSpec((None, D_MODEL, tf), lambda i, j: (layer, 0, j)),
            pl.BlockSpec((None, D_MODEL, tf), lambda i, j: (layer, 0, nf + j)),
            pl.BlockSpec((None, tf, D_MODEL), lambda i, j: (layer, j, 0)),
            pl.BlockSpec((1, D_MODEL), lambda i, j: (0, 0)),
        ],
        out_specs=pl.BlockSpec((tm, D_MODEL), lambda i, j: (i, 0)),
        out_shape=jax.ShapeDtypeStruct((m, D_MODEL), F32),
        scratch_shapes=[pltpu.VMEM((tm, D_MODEL), MXU_DTYPE), pltpu.VMEM((tm, D_MODEL), F32)],
        compiler_params=_params("parallel", "arbitrary"),
        name="ffn",
    )(x, norm_g.reshape(-1, 1, D_MODEL), w_in, w_in, w_out, gf.reshape(1, D_MODEL))


def _proj_kernel(x_ref, g_ref, w_ref, o_ref, h_ref):
    @pl.when(pl.program_id(1) == 0)
    def _():
        h_ref[...] = _rmsnorm(x_ref[...], g_ref[...]).astype(h_ref.dtype)

    o_ref[...] = jnp.dot(h_ref[...], w_ref[...], preferred_element_type=F32)


def _proj_in(x, norm_g, w_all, layer, *, tm, tn):
    m = x.shape[0]
    return pl.pallas_call(
        _proj_kernel,
        grid=(m // tm, PROJ_COLS // tn),
        in_specs=[
            pl.BlockSpec((tm, D_MODEL), lambda i, j: (i, 0)),
            pl.BlockSpec((None, 1, D_MODEL), lambda i, j: (layer, 0, 0)),
            pl.BlockSpec((None, D_MODEL, tn), lambda i, j: (layer, 0, j)),
        ],
        out_specs=pl.BlockSpec((tm, tn), lambda i, j: (i, j)),
        out_shape=jax.ShapeDtypeStruct((m, PROJ_COLS), F32),
        scratch_shapes=[pltpu.VMEM((tm, D_MODEL), MXU_DTYPE)],
        compiler_params=_params("parallel", "arbitrary"),
        name="proj_in",
    )(x, norm_g.reshape(-1, 1, D_MODEL), w_all)


def _proj_out_kernel(x_ref, y1_ref, y2_ref, y3_ref, w1_ref, w2_ref, w3_ref, o_ref):
    acc = jnp.dot(y1_ref[...].astype(MXU_DTYPE), w1_ref[...], preferred_element_type=F32)
    acc += jnp.dot(y2_ref[...].astype(MXU_DTYPE), w2_ref[...], preferred_element_type=F32)
    acc += jnp.dot(y3_ref[...].astype(MXU_DTYPE), w3_ref[...], preferred_element_type=F32)
    o_ref[...] = x_ref[...] + acc


def _proj_out(x, y_rwkv, y_sc, y_ssm, w_out, layer, *, tm):
    m = x.shape[0]
    return pl.pallas_call(
        _proj_out_kernel,
        grid=(m // tm,),
        in_specs=[
            pl.BlockSpec((tm, D_MODEL), lambda i: (i, 0)),
            pl.BlockSpec((tm, RWKV_DIM), lambda i: (i, 0)),
            pl.BlockSpec((tm, SC_DIM), lambda i: (i, 0)),
            pl.BlockSpec((tm, SSM_DIM), lambda i: (i, 0)),
            pl.BlockSpec((None, RWKV_DIM, D_MODEL), lambda i: (layer, 0, 0)),
            pl.BlockSpec((None, SC_DIM, D_MODEL), lambda i: (layer, RWKV_DIM // SC_DIM, 0)),
            pl.BlockSpec((None, SSM_DIM, D_MODEL), lambda i: (layer, (RWKV_DIM + SC_DIM) // SSM_DIM, 0)),
        ],
        out_specs=pl.BlockSpec((tm, D_MODEL), lambda i: (i, 0)),
        out_shape=jax.ShapeDtypeStruct((m, D_MODEL), F32),
        compiler_params=_params("parallel"),
        name="proj_out",
    )(x, y_rwkv, y_sc, y_ssm, w_out, w_out, w_out)


def _rwkv_prepare(p_rkv, prev_rkv, p_sm, prev_sm, wts):
    (mu_rkv, mu_sm, w0, w2, a0, a2, g2, k_k, k_a) = wts
    xs = p_rkv + (prev_rkv - p_rkv) * mu_rkv
    xm = p_sm + (prev_sm - p_sm) * mu_sm
    r = xs[:, :RWKV_DIM]
    k = xs[:, RWKV_DIM:2 * RWKV_DIM]
    v = xs[:, 2 * RWKV_DIM:]
    w_lr = xm[:, :DECAY_LORA]
    a_lr = xm[:, DECAY_LORA:DECAY_LORA + AAA_LORA]
    g_lr = xm[:, SM_GL:SM_GL + GATE_LORA]
    w = -_softplus(-(w0 + _mm(jnp.tanh(w_lr), w2))) - 0.5
    log_decay = -jnp.exp(w)
    a = _sigmoid(a0 + _mm(a_lr, a2))
    g = _mm(_sigmoid(g_lr), g2)
    kk = k * k_k
    k_mod = k * (1.0 + (a - 1.0) * k_a)
    return r, log_decay, k_mod, v, kk, a, g


def _rwkv_weight_specs(layer):
    def vec(width):
        return pl.BlockSpec((None, 1, width), lambda *idx: (layer, 0, 0))

    def mat(rows):
        return pl.BlockSpec((None, rows, RWKV_DIM), lambda *idx: (layer, 0, 0))

    return [vec(3 * RWKV_DIM), vec(SMALL_W), vec(RWKV_DIM), mat(DECAY_LORA), vec(RWKV_DIM), mat(AAA_LORA),
            mat(GATE_LORA), vec(RWKV_DIM), vec(RWKV_DIM), vec(RWKV_DIM), vec(RWKV_DIM), vec(RWKV_DIM)]


def _rwkv_weight_args(wp):
    return (wp["mu_rkv"], wp["mu_sm"], wp["rwkv_w0"], wp["rwkv_w2"], wp["rwkv_a0"], wp["rwkv_a2"],
            wp["rwkv_g2"], wp["rwkv_k_k"], wp["rwkv_k_a"], wp["rwkv_r_k"], wp["rwkv_ln_w"], wp["rwkv_ln_b"])


def _unit_lower_inverse(low, eye):
    n = low.shape[0]
    inv = eye + low
    power = low
    span = 2
    while span < n:
        power = _mmx(power, power)
        inv = inv + _mmx(inv, power)
        span *= 2
    return inv


def _rwkv_chunk_kernel(p_rkv_ref, p_sm_ref, sh_rkv_ref, sh_sm_ref, s0_ref,
                       mu_rkv_ref, mu_sm_ref, w0_ref, w2_ref, a0_ref, a2_ref, g2_ref, kk_ref, ka_ref,
                       rk_ref, lnw_ref, lnb_ref,
                       y_ref, s_out_ref,
                       pad_rkv, pad_sm, state, r_s, k_s, v_s, kk_s, a_s, g_s, cum_s, ld_s, yn_s, bs_s):
    c = RWKV_CHUNK
    t = pl.program_id(1)

    @pl.when(t == 0)
    def _():
        pad_rkv[SUBLANE - 1:SUBLANE, :] = sh_rkv_ref[0]
        pad_sm[SUBLANE - 1:SUBLANE, :] = sh_sm_ref[0]
        state[...] = s0_ref[0]

    pad_rkv[SUBLANE:SUBLANE + c, :] = p_rkv_ref[...]
    pad_sm[SUBLANE:SUBLANE + c, :] = p_sm_ref[...]
    wts = (mu_rkv_ref[...], mu_sm_ref[...], w0_ref[...], w2_ref[...], a0_ref[...], a2_ref[...],
           g2_ref[...], kk_ref[...], ka_ref[...])
    r, log_decay, k_mod, v, kk, a, g = _rwkv_prepare(
        p_rkv_ref[...], pad_rkv[SUBLANE - 1:SUBLANE - 1 + c, :],
        p_sm_ref[...], pad_sm[SUBLANE - 1:SUBLANE - 1 + c, :], wts)
    pad_rkv[SUBLANE - 1:SUBLANE, :] = p_rkv_ref[c - 1:c, :]
    pad_sm[SUBLANE - 1:SUBLANE, :] = p_sm_ref[c - 1:c, :]

    row = lax.broadcasted_iota(jnp.int32, (c, c), 0)
    col = lax.broadcasted_iota(jnp.int32, (c, c), 1)
    incl = row >= col
    strict = row > col
    eye = (row == col).astype(F32)
    cum = _mmx(incl.astype(F32), log_decay)
    r_s[...] = r
    k_s[...] = k_mod
    v_s[...] = v
    kk_s[...] = kk
    a_s[...] = a
    g_s[...] = g
    cum_s[...] = cum
    ld_s[...] = log_decay

    for h in range(RWKV_HEADS):
        sl = slice(h * RWKV_HEAD, (h + 1) * RWKV_HEAD)
        rh, kh, vh, ah = r_s[:, sl], k_s[:, sl], v_s[:, sl], a_s[:, sl]
        cumh, ldh = cum_s[:, sl], ld_s[:, sl]
        kkh = kk_s[:, sl]
        kkh = kkh * lax.rsqrt(jnp.maximum(jnp.sum(kkh * kkh, axis=-1, keepdims=True), 1e-24))
        bh = kkh * ah
        e_in = jnp.exp(cumh)
        e_out = jnp.exp(-cumh)
        e_end = jnp.exp(cumh[c - 1:c, :] - cumh)
        r_t = rh * e_in
        a_t = -kkh * jnp.exp(cumh - ldh)
        k_t = kh * e_out
        b_t = bh * e_out
        l_ab = jnp.where(strict, _mmx_nt(a_t, b_t), 0.0)
        l_ak = jnp.where(strict, _mmx_nt(a_t, k_t), 0.0)
        m_rb = jnp.where(incl, _mmx_nt(r_t, b_t), 0.0)
        m_rk = jnp.where(incl, _mmx_nt(r_t, k_t), 0.0)
        s0 = state[h]
        inv = _unit_lower_inverse(l_ab, eye)
        u = _mmx(inv, _mmx_nt(a_t, s0) + _mmx(l_ak, vh))
        y = _mmx_nt(r_t, s0) + _mmx(m_rb, u) + _mmx(m_rk, vh)
        state[h] = s0 * e_in[c - 1:c, :] + _mmx_tn(u, bh * e_end) + _mmx_tn(vh, kh * e_end)
        mean = jnp.mean(y, axis=-1, keepdims=True)
        yc = y - mean
        var = jnp.mean(yc * yc, axis=-1, keepdims=True)
        yn_s[:, sl] = yc * lax.rsqrt(var + RWKV_GN_EPS)
        bonus = jnp.sum(rh * kh * rk_ref[:, sl], axis=-1, keepdims=True)
        bs_s[:, sl] = jnp.broadcast_to(bonus, (c, RWKV_HEAD))

    y_ref[...] = (yn_s[...] * lnw_ref[...] + lnb_ref[...] + bs_s[...] * v_s[...]) * g_s[...]

    @pl.when(t == pl.num_programs(1) - 1)
    def _():
        s_out_ref[0] = state[...]


def _rwkv_prompt(proj, shift_rkv, shift_sm, wkv0, wp, layer, *, batch, seqlen):
    c = RWKV_CHUNK
    nt = seqlen // c
    big = lambda: pltpu.VMEM((c, RWKV_DIM), F32)
    return pl.pallas_call(
        _rwkv_chunk_kernel,
        grid=(batch, nt),
        in_specs=[
            pl.BlockSpec((c, 3 * RWKV_DIM), lambda b, t: (b * nt + t, COL_RKV // (3 * RWKV_DIM))),
            pl.BlockSpec((c, SMALL_W), lambda b, t: (b * nt + t, COL_SMALL // SMALL_W)),
            pl.BlockSpec((1, 1, 3 * RWKV_DIM), lambda b, t: (b, 0, 0)),
            pl.BlockSpec((1, 1, SMALL_W), lambda b, t: (b, 0, 0)),
            pl.BlockSpec((1, RWKV_HEADS, RWKV_HEAD, RWKV_HEAD), lambda b, t: (b, 0, 0, 0)),
        ] + _rwkv_weight_specs(layer),
        out_specs=[
            pl.BlockSpec((c, RWKV_DIM), lambda b, t: (b * nt + t, 0)),
            pl.BlockSpec((1, RWKV_HEADS, RWKV_HEAD, RWKV_HEAD), lambda b, t: (b, 0, 0, 0)),
        ],
        out_shape=[
            jax.ShapeDtypeStruct((batch * seqlen, RWKV_DIM), F32),
            jax.ShapeDtypeStruct((batch, RWKV_HEADS, RWKV_HEAD, RWKV_HEAD), F32),
        ],
        scratch_shapes=[
            pltpu.VMEM((SUBLANE + c, 3 * RWKV_DIM), F32),
            pltpu.VMEM((SUBLANE + c, SMALL_W), F32),
            pltpu.VMEM((RWKV_HEADS, RWKV_HEAD, RWKV_HEAD), F32),
        ] + [big() for _ in range(10)],
        compiler_params=_params("parallel", "arbitrary"),
        name="rwkv_chunk",
    )(proj, proj, shift_rkv, shift_sm, wkv0, *_rwkv_weight_args(wp))


def _sc_kernel(p_ref, buf_ref, w_ref, g_ref, y_ref, buf_out_ref, pad, *, tl):
    t = pl.program_id(1)
    hist = SC_WIDTH - 1

    @pl.when(t == 0)
    def _():
        pad[SUBLANE - hist:SUBLANE, :] = buf_ref[0]

    b_gate = p_ref[:, :SC_DIM]
    u = p_ref[:, SC_DIM:2 * SC_DIM] * p_ref[:, 2 * SC_DIM:]
    pad[SUBLANE:SUBLANE + tl, :] = u
    conv = u * w_ref[hist:hist + 1, :]
    for j in range(hist):
        conv += pad[SUBLANE - hist + j:SUBLANE - hist + j + tl, :] * w_ref[j:j + 1, :]
    y_ref[...] = _rmsnorm(b_gate * conv, g_ref[...])
    last = pad[tl + SUBLANE - hist:tl + SUBLANE, :]
    pad[SUBLANE - hist:SUBLANE, :] = last

    @pl.when(t == pl.num_programs(1) - 1)
    def _():
        buf_out_ref[0] = last


def _sc_prompt(proj, buf0, wp, layer, *, batch, seqlen, tl):
    nt = seqlen // tl
    hist = SC_WIDTH - 1
    return pl.pallas_call(
        functools.partial(_sc_kernel, tl=tl),
        grid=(batch, nt),
        in_specs=[
            pl.BlockSpec((tl, 3 * SC_DIM), lambda b, t: (b * nt + t, COL_SC // (3 * SC_DIM))),
            pl.BlockSpec((1, hist, SC_DIM), lambda b, t: (b, 0, 0)),
            pl.BlockSpec((None, SC_WIDTH, SC_DIM), lambda b, t: (layer, 0, 0)),
            pl.BlockSpec((None, 1, SC_DIM), lambda b, t: (layer, 0, 0)),
        ],
        out_specs=[
            pl.BlockSpec((tl, SC_DIM), lambda b, t: (b * nt + t, 0)),
            pl.BlockSpec((1, hist, SC_DIM), lambda b, t: (b, 0, 0)),
        ],
        out_shape=[
            jax.ShapeDtypeStruct((batch * seqlen, SC_DIM), F32),
            jax.ShapeDtypeStruct((batch, hist, SC_DIM), F32),
        ],
        scratch_shapes=[pltpu.VMEM((SUBLANE + tl, SC_DIM), F32)],
        compiler_params=_params("parallel", "arbitrary"),
        name="short_conv",
    )(proj, buf0, wp["sc_conv_w"], wp["sc_norm"])


def _ssd_kernel(z_ref, xbc_ref, dt_ref, buf_ref, h0_ref, cw_ref, cb_ref, dtb_ref, alog_ref, dskip_ref, g_ref,
                y_ref, buf_out_ref, h_out_ref, pad, state, y_s):
    q = SSD_CHUNK
    t = pl.program_id(1)
    hist = SSM_CONV - 1
    gw = SSM_DIM // SSM_GROUPS

    @pl.when(t == 0)
    def _():
        pad[SUBLANE - hist:SUBLANE, :] = buf_ref[0]
        state[...] = h0_ref[0]

    xbc = xbc_ref[...]
    pad[SUBLANE:SUBLANE + q, :] = xbc
    conv = xbc * cw_ref[hist:hist + 1, :]
    for j in range(hist):
        conv += pad[SUBLANE - hist + j:SUBLANE - hist + j + q, :] * cw_ref[j:j + 1, :]
    last = pad[q + SUBLANE - hist:q + SUBLANE, :]
    pad[SUBLANE - hist:SUBLANE, :] = last
    xc = _silu(conv + cb_ref[...])
    xh = xc[:, :SSM_DIM]

    dt = _softplus(dt_ref[...] + dtb_ref[...])
    neg_a = -jnp.exp(alog_ref[...])
    row = lax.broadcasted_iota(jnp.int32, (q, q), 0)
    col = lax.broadcasted_iota(jnp.int32, (q, q), 1)
    causal = row >= col
    cum = _mmx(causal.astype(F32), dt * neg_a)
    cum_t = cum.T
    cum_end = cum[q - 1:q, :]
    heads_per_group = SSM_HEADS // SSM_GROUPS

    for grp in range(SSM_GROUPS):
        bm = xc[:, SSM_DIM + grp * SSM_STATE:SSM_DIM + (grp + 1) * SSM_STATE]
        cm = xc[:, SSM_DIM + (SSM_GROUPS + grp) * SSM_STATE:SSM_DIM + (SSM_GROUPS + grp + 1) * SSM_STATE]
        gram = _mm_nt(cm, bm)
        for hh in range(heads_per_group):
            h = grp * heads_per_group + hh
            sl = slice(h * SSM_HEADDIM, (h + 1) * SSM_HEADDIM)
            cum_col = cum[:, h:h + 1]
            seg = cum_col - cum_t[h:h + 1, :]
            decay = jnp.where(causal, jnp.exp(jnp.where(causal, seg, 0.0)), 0.0)
            xdt = xh[:, sl] * dt[:, h:h + 1]
            y = _mm(gram * decay, xdt)
            h_prev = state[h]
            y += _mm_nt(cm, h_prev) * jnp.exp(cum_col)
            end = cum_end[:, h:h + 1]
            state[h] = h_prev * jnp.exp(end) + _mm_tn(xdt * jnp.exp(end - cum_col), bm)
            y_s[:, sl] = y + dskip_ref[:, h:h + 1] * xh[:, sl]

    y = y_s[...] * _silu(z_ref[...])
    for grp in range(SSM_GROUPS):
        yg = y[:, grp * gw:(grp + 1) * gw]
        y_ref[:, grp * gw:(grp + 1) * gw] = _rmsnorm(yg, g_ref[:, grp * gw:(grp + 1) * gw])

    @pl.when(t == pl.num_programs(1) - 1)
    def _():
        buf_out_ref[0] = last
        h_out_ref[0] = state[...]


def _ssd_prompt(proj, buf0, h0, wp, layer, *, batch, seqlen):
    q = SSD_CHUNK
    nt = seqlen // q
    hist = SSM_CONV - 1
    lane_vec = pl.BlockSpec((None, 1, LANE), lambda b, t: (layer, 0, 0))
    return pl.pallas_call(
        _ssd_kernel,
        grid=(batch, nt),
        in_specs=[
            pl.BlockSpec((q, SSM_DIM), lambda b, t: (b * nt + t, COL_Z // SSM_DIM)),
            pl.BlockSpec((q, SSM_CONV_DIM), lambda b, t: (b * nt + t, COL_XBC // SSM_CONV_DIM)),
            pl.BlockSpec((q, LANE), lambda b, t: (b * nt + t, (COL_SMALL + SM_DT) // LANE)),
            pl.BlockSpec((1, hist, SSM_CONV_DIM), lambda b, t: (b, 0, 0)),
            pl.BlockSpec((1, SSM_HEADS, SSM_HEADDIM, SSM_STATE), lambda b, t: (b, 0, 0, 0)),
            pl.BlockSpec((None, SSM_CONV, SSM_CONV_DIM), lambda b, t: (layer, 0, 0)),
            pl.BlockSpec((None, 1, SSM_CONV_DIM), lambda b, t: (layer, 0, 0)),
            lane_vec, lane_vec, lane_vec,
            pl.BlockSpec((None, 1, SSM_DIM), lambda b, t: (layer, 0, 0)),
        ],
        out_specs=[
            pl.BlockSpec((q, SSM_DIM), lambda b, t: (b * nt + t, 0)),
            pl.BlockSpec((1, hist, SSM_CONV_DIM), lambda b, t: (b, 0, 0)),
            pl.BlockSpec((1, SSM_HEADS, SSM_HEADDIM, SSM_STATE), lambda b, t: (b, 0, 0, 0)),
        ],
        out_shape=[
            jax.ShapeDtypeStruct((batch * seqlen, SSM_DIM), F32),
            jax.ShapeDtypeStruct((batch, hist, SSM_CONV_DIM), F32),
            jax.ShapeDtypeStruct((batch, SSM_HEADS, SSM_HEADDIM, SSM_STATE), F32),
        ],
        scratch_shapes=[
            pltpu.VMEM((SUBLANE + q, SSM_CONV_DIM), F32),
            pltpu.VMEM((SSM_HEADS, SSM_HEADDIM, SSM_STATE), F32),
            pltpu.VMEM((q, SSM_DIM), F32),
        ],
        compiler_params=_params("parallel", "arbitrary"),
        name="ssd_chunk",
    )(proj, proj, proj, buf0, h0, wp["ssm_conv_w"], wp["ssm_conv_b"], wp["dt_bias_pad"], wp["a_log_pad"],
      wp["d_pad"], wp["ssm_norm"])


def _row_to_col(x_row, eye):
    return jnp.sum(jnp.where(eye, x_row, 0.0), axis=1, keepdims=True)


def _col_to_row(x_col, eye):
    return jnp.sum(jnp.where(eye, x_col, 0.0), axis=0, keepdims=True)


def _step_kernel(p_rkv_ref, p_sm_ref, p_sc_ref, z_ref, xbc_ref,
                 sh_rkv_ref, sh_sm_ref, wkv_ref, scbuf_ref, ssmbuf_ref, ssm_ref,
                 mu_rkv_ref, mu_sm_ref, w0_ref, w2_ref, a0_ref, a2_ref, g2_ref, kk_ref, ka_ref,
                 rk_ref, lnw_ref, lnb_ref,
                 scw_ref, scg_ref, cw_ref, cb_ref, dtb_ref, alog_ref, dskip_ref, ssmg_ref,
                 y_rwkv_ref, y_sc_ref, y_ssm_ref, wkv_out_ref, scbuf_out_ref, ssmbuf_out_ref, ssm_out_ref,
                 r_s, w_s, k_s, v_s, kk_s, b_s, y_s, xh_s, bc_s, dt_s, ys_s):
    n = RWKV_HEAD
    eye = lax.broadcasted_iota(jnp.int32, (n, n), 0) == lax.broadcasted_iota(jnp.int32, (n, n), 1)

    wts = (mu_rkv_ref[...], mu_sm_ref[...], w0_ref[...], w2_ref[...], a0_ref[...], a2_ref[...],
           g2_ref[...], kk_ref[...], ka_ref[...])
    r, log_decay, k_mod, v, kk, a, g = _rwkv_prepare(p_rkv_ref[0], sh_rkv_ref[0], p_sm_ref[0], sh_sm_ref[0], wts)
    r_s[...] = r
    w_s[...] = jnp.exp(log_decay)
    k_s[...] = k_mod
    v_s[...] = v
    kk_s[...] = kk
    b_s[...] = a
    for h in range(RWKV_HEADS):
        sl = slice(h * n, (h + 1) * n)
        kkh = kk_s[:, sl]
        kkh = kkh * lax.rsqrt(jnp.maximum(jnp.sum(kkh * kkh, axis=-1, keepdims=True), 1e-24))
        bh = kkh * b_s[:, sl]
        s = wkv_ref[0, h]
        sa = jnp.sum(s * (-kkh), axis=1, keepdims=True)
        v_col = _row_to_col(v_s[:, sl], eye)
        s = s * w_s[:, sl] + sa * bh + v_col * k_s[:, sl]
        wkv_out_ref[0, h] = s
        y = _col_to_row(jnp.sum(s * r_s[:, sl], axis=1, keepdims=True), eye)
        mean = jnp.mean(y, axis=-1, keepdims=True)
        yc = y - mean
        var = jnp.mean(yc * yc, axis=-1, keepdims=True)
        bonus = jnp.sum(r_s[:, sl] * k_s[:, sl] * rk_ref[:, sl], axis=-1, keepdims=True)
        y_s[:, sl] = (yc * lax.rsqrt(var + RWKV_GN_EPS) * lnw_ref[:, sl] + lnb_ref[:, sl]
                      + bonus * v_s[:, sl])
    y_rwkv_ref[0] = y_s[...] * g

    p_sc = p_sc_ref[0]
    u = p_sc[:, SC_DIM:2 * SC_DIM] * p_sc[:, 2 * SC_DIM:]
    buf = scbuf_ref[0]
    conv = buf[0:1] * scw_ref[0:1, :] + buf[1:2] * scw_ref[1:2, :] + u * scw_ref[2:3, :]
    y_sc_ref[0] = _rmsnorm(p_sc[:, :SC_DIM] * conv, scg_ref[...])
    scbuf_out_ref[0, 0:1] = buf[1:2]
    scbuf_out_ref[0, 1:2] = u

    xbc = xbc_ref[0]
    cbuf = ssmbuf_ref[0]
    conv = xbc * cw_ref[3:4, :]
    for j in range(SSM_CONV - 1):
        conv += cbuf[j:j + 1] * cw_ref[j:j + 1, :]
    ssmbuf_out_ref[0, 0:2] = cbuf[1:3]
    ssmbuf_out_ref[0, 2:3] = xbc
    xc = _silu(conv + cb_ref[...])
    xh_s[...] = xc[:, :SSM_DIM]
    bc_s[...] = xc[:, SSM_DIM:]
    dt = _softplus(p_sm_ref[0][:, SM_DT:SM_DT + LANE] + dtb_ref[...])
    dt_s[...] = dt
    decay_all = jnp.exp(dt * (-jnp.exp(alog_ref[...])))
    heads_per_group = SSM_HEADS // SSM_GROUPS
    for h in range(SSM_HEADS):
        grp = h // heads_per_group
        sl = slice(h * SSM_HEADDIM, (h + 1) * SSM_HEADDIM)
        bm = bc_s[:, grp * SSM_STATE:(grp + 1) * SSM_STATE]
        cm = bc_s[:, (SSM_GROUPS + grp) * SSM_STATE:(SSM_GROUPS + grp + 1) * SSM_STATE]
        xh = xh_s[:, sl]
        xdt_col = _row_to_col(xh * dt_s[:, h:h + 1], eye)
        hs = ssm_ref[0, h] * decay_all[:, h:h + 1] + xdt_col * bm
        ssm_out_ref[0, h] = hs
        y = _col_to_row(jnp.sum(hs * cm, axis=1, keepdims=True), eye)
        ys_s[:, sl] = y + dskip_ref[:, h:h + 1] * xh
    y = ys_s[...] * _silu(z_ref[0])
    gw = SSM_DIM // SSM_GROUPS
    for grp in range(SSM_GROUPS):
        yg = y[:, grp * gw:(grp + 1) * gw]
        y_ssm_ref[0, :, grp * gw:(grp + 1) * gw] = _rmsnorm(yg, ssmg_ref[:, grp * gw:(grp + 1) * gw])


def _mixers_step(proj, states, wp, layer, *, batch):
    shift_rkv, shift_sm, wkv, sc_buf, ssm_buf, ssm = states
    p3 = proj.reshape(batch, 1, PROJ_COLS)
    row = lambda width, blk: pl.BlockSpec((1, 1, width), lambda b: (b, 0, blk))
    st3 = lambda d1, d2: pl.BlockSpec((1, d1, d2), lambda b: (b, 0, 0))
    st4 = lambda d1, d2, d3: pl.BlockSpec((1, d1, d2, d3), lambda b: (b, 0, 0, 0))
    vec = lambda width: pl.BlockSpec((None, 1, width), lambda b: (layer, 0, 0))
    mat = lambda rows, width: pl.BlockSpec((None, rows, width), lambda b: (layer, 0, 0))
    one = lambda width: pltpu.VMEM((1, width), F32)
    outs = pl.pallas_call(
        _step_kernel,
        grid=(batch,),
        in_specs=[
            row(3 * RWKV_DIM, COL_RKV // (3 * RWKV_DIM)),
            row(SMALL_W, COL_SMALL // SMALL_W),
            row(3 * SC_DIM, COL_SC // (3 * SC_DIM)),
            row(SSM_DIM, COL_Z // SSM_DIM),
            row(SSM_CONV_DIM, COL_XBC // SSM_CONV_DIM),
            st3(1, 3 * RWKV_DIM), st3(1, SMALL_W),
            st4(RWKV_HEADS, RWKV_HEAD, RWKV_HEAD),
            st3(SC_WIDTH - 1, SC_DIM), st3(SSM_CONV - 1, SSM_CONV_DIM),
            st4(SSM_HEADS, SSM_HEADDIM, SSM_STATE),
        ] + [vec(3 * RWKV_DIM), vec(SMALL_W), vec(RWKV_DIM), mat(DECAY_LORA, RWKV_DIM), vec(RWKV_DIM),
             mat(AAA_LORA, RWKV_DIM), mat(GATE_LORA, RWKV_DIM), vec(RWKV_DIM), vec(RWKV_DIM), vec(RWKV_DIM),
             vec(RWKV_DIM), vec(RWKV_DIM),
             mat(SC_WIDTH, SC_DIM), vec(SC_DIM), mat(SSM_CONV, SSM_CONV_DIM), vec(SSM_CONV_DIM),
             vec(LANE), vec(LANE), vec(LANE), vec(SSM_DIM)],
        out_specs=[
            st3(1, RWKV_DIM), st3(1, SC_DIM), st3(1, SSM_DIM),
            st4(RWKV_HEADS, RWKV_HEAD, RWKV_HEAD),
            st3(SC_WIDTH - 1, SC_DIM), st3(SSM_CONV - 1, SSM_CONV_DIM),
            st4(SSM_HEADS, SSM_HEADDIM, SSM_STATE),
        ],
        out_shape=[
            jax.ShapeDtypeStruct((batch, 1, RWKV_DIM), F32),
            jax.ShapeDtypeStruct((batch, 1, SC_DIM), F32),
            jax.ShapeDtypeStruct((batch, 1, SSM_DIM), F32),
            jax.ShapeDtypeStruct(wkv.shape, F32),
            jax.ShapeDtypeStruct(sc_buf.shape, F32),
            jax.ShapeDtypeStruct(ssm_buf.shape, F32),
            jax.ShapeDtypeStruct(ssm.shape, F32),
        ],
        scratch_shapes=[one(RWKV_DIM) for _ in range(7)] + [one(SSM_DIM), one(2 * SSM_GROUPS * SSM_STATE),
                                                            one(LANE), one(SSM_DIM)],
        compiler_params=_params("parallel"),
        name="mixers_step",
    )(p3, p3, p3, p3, p3, shift_rkv, shift_sm, wkv, sc_buf, ssm_buf, ssm,
      *_rwkv_weight_args(wp), wp["sc_conv_w"], wp["sc_norm"], wp["ssm_conv_w"], wp["ssm_conv_b"],
      wp["dt_bias_pad"], wp["a_log_pad"], wp["d_pad"], wp["ssm_norm"])
    y_rwkv, y_sc, y_ssm, wkv_n, sc_n, ssmbuf_n, ssm_n = outs
    return (y_rwkv.reshape(batch, RWKV_DIM), y_sc.reshape(batch, SC_DIM), y_ssm.reshape(batch, SSM_DIM),
            wkv_n, sc_n, ssmbuf_n, ssm_n)


def _reorder_cols(a):
    lead = a.shape[:-1]
    rkv = a[..., :3 * RWKV_DIM]
    lora = a[..., 3 * RWKV_DIM:RWKV_PROJ]
    small = jnp.concatenate([lora, jnp.zeros(lead + (SMALL_W - lora.shape[-1],), a.dtype)], axis=-1)
    return rkv, small


def _prepare_weights(w):
    depth = w["w_in"].shape[0]
    wp = {}
    w_in = w["w_in"]
    rkv, small = _reorder_cols(w_in[..., :RWKV_PROJ])
    o_sc = RWKV_PROJ
    o_ssm = RWKV_PROJ + 3 * SC_DIM
    dt_cols = w_in[..., o_ssm + SSM_DIM + SSM_CONV_DIM:]
    small = small.at[..., SM_DT:SM_DT + SSM_HEADS].set(dt_cols)
    wp["w_all"] = jnp.concatenate(
        [rkv, w_in[..., o_sc:o_ssm], w_in[..., o_ssm:o_ssm + SSM_DIM + SSM_CONV_DIM], small],
        axis=-1).astype(MXU_DTYPE)
    mu_rkv, mu_sm = _reorder_cols(w["rwkv_mu"])
    wp["mu_rkv"] = mu_rkv.reshape(depth, 1, -1)
    wp["mu_sm"] = mu_sm.reshape(depth, 1, -1)
    for name in ("rwkv_w0", "rwkv_a0", "rwkv_k_k", "rwkv_k_a", "rwkv_ln_w", "rwkv_ln_b", "sc_norm",
                 "ssm_conv_b", "ssm_norm"):
        wp[name] = w[name].reshape(depth, 1, -1)
    wp["rwkv_r_k"] = w["rwkv_r_k"].reshape(depth, 1, RWKV_DIM)
    for name in ("rwkv_w2", "rwkv_a2", "rwkv_g2"):
        wp[name] = w[name].astype(MXU_DTYPE)
    wp["sc_conv_w"] = w["sc_conv_w"]
    wp["ssm_conv_w"] = w["ssm_conv_w"]
    pad_lane = lambda a: jnp.pad(a, ((0, 0), (0, LANE - a.shape[-1]))).reshape(depth, 1, LANE)
    wp["dt_bias_pad"] = pad_lane(w["ssm_dt_bias"])
    wp["a_log_pad"] = pad_lane(w["ssm_A_log"])
    wp["d_pad"] = pad_lane(w["ssm_D"])
    for name in ("ffn1_w_in", "ffn1_w_out", "ffn2_w_in", "ffn2_w_out", "w_out"):
        wp[name] = w[name].astype(MXU_DTYPE)
    for name in ("norm_ffn1", "norm_mix", "norm_ffn2"):
        wp[name] = w[name]
    return wp


def _shift_state_to_cols(proj_last):
    return jnp.concatenate([proj_last[..., :3 * RWKV_DIM],
                            proj_last[..., COL_SMALL:COL_SMALL + RWKV_PROJ - 3 * RWKV_DIM]], axis=-1)


def _tile(m, pref):
    t = min(m, pref)
    while m % t:
        t //= 2
    return t


def _trunk(x3, states, wp, norm_final):
    batch, seqlen, _ = x3.shape
    m = batch * seqlen
    x = x3.reshape(m, D_MODEL)
    depth = wp["w_all"].shape[0]
    tm = _tile(m, 512)
    new = ([], [], [], [], [])
    for layer in range(depth):
        shift, wkv, sc_buf, ssm_buf, ssm = (s[layer] for s in states)
        shift_rkv, shift_sm = _reorder_cols(shift)
        shift_rkv = shift_rkv.reshape(batch, 1, -1)
        shift_sm = shift_sm.reshape(batch, 1, -1)
        x = _ffn(x, wp["norm_ffn1"], wp["ffn1_w_in"], wp["ffn1_w_out"], layer, tm=tm, tf=1408)
        proj = _proj_in(x, wp["norm_mix"], wp["w_all"], layer, tm=tm, tn=PROJ_COLS // 13)
        if seqlen == 1:
            y_rwkv, y_sc, y_ssm, wkv_n, sc_n, ssmbuf_n, ssm_n = _mixers_step(
                proj, (shift_rkv, shift_sm, wkv, sc_buf, ssm_buf, ssm), wp, layer, batch=batch)
        else:
            y_rwkv, wkv_n = _rwkv_prompt(proj, shift_rkv, shift_sm, wkv, wp, layer, batch=batch, seqlen=seqlen)
            y_sc, sc_n = _sc_prompt(proj, sc_buf, wp, layer, batch=batch, seqlen=seqlen, tl=_tile(seqlen, 256))
            y_ssm, ssmbuf_n, ssm_n = _ssd_prompt(proj, ssm_buf, ssm, wp, layer, batch=batch, seqlen=seqlen)
        shift_n = _shift_state_to_cols(proj.reshape(batch, seqlen, PROJ_COLS)[:, -1])
        x = _proj_out(x, y_rwkv, y_sc, y_ssm, wp["w_out"], layer, tm=tm)
        x = _ffn(x, wp["norm_ffn2"], wp["ffn2_w_in"], wp["ffn2_w_out"], layer,
                 final_g=norm_final if layer == depth - 1 else None, tm=tm, tf=1408)
        for lst, s in zip(new, (shift_n, wkv_n, sc_n, ssmbuf_n, ssm_n)):
            lst.append(s)
    return x.reshape(batch, seqlen, D_MODEL), tuple(jnp.stack(lst) for lst in new)


def kernel(x_prompt, x_sample, state_rwkv_shift, state_rwkv_wkv, state_sc_buf, state_ssm_conv, state_ssm,
           norm_ffn1, ffn1_w_in, ffn1_w_out, norm_mix, w_in, rwkv_mu, rwkv_w0, rwkv_w2, rwkv_a0, rwkv_a2,
           rwkv_g2, rwkv_k_k, rwkv_k_a, rwkv_r_k, rwkv_ln_w, rwkv_ln_b, sc_conv_w, sc_norm, ssm_conv_w,
           ssm_conv_b, ssm_dt_bias, ssm_A_log, ssm_D, ssm_norm, w_out, norm_ffn2, ffn2_w_in, ffn2_w_out,
           norm_final):
    weights = dict(
        norm_ffn1=norm_ffn1, ffn1_w_in=ffn1_w_in, ffn1_w_out=ffn1_w_out, norm_mix=norm_mix, w_in=w_in,
        rwkv_mu=rwkv_mu, rwkv_w0=rwkv_w0, rwkv_w2=rwkv_w2, rwkv_a0=rwkv_a0, rwkv_a2=rwkv_a2, rwkv_g2=rwkv_g2,
        rwkv_k_k=rwkv_k_k, rwkv_k_a=rwkv_k_a, rwkv_r_k=rwkv_r_k, rwkv_ln_w=rwkv_ln_w, rwkv_ln_b=rwkv_ln_b,
        sc_conv_w=sc_conv_w, sc_norm=sc_norm, ssm_conv_w=ssm_conv_w, ssm_conv_b=ssm_conv_b,
        ssm_dt_bias=ssm_dt_bias, ssm_A_log=ssm_A_log, ssm_D=ssm_D, ssm_norm=ssm_norm, w_out=w_out,
        norm_ffn2=norm_ffn2, ffn2_w_in=ffn2_w_in, ffn2_w_out=ffn2_w_out)
    wp = _prepare_weights(weights)
    depth = w_in.shape[0]
    nb, dt_ = x_prompt.shape[0], x_prompt.dtype
    zero_states = (
        jnp.zeros((depth, nb, RWKV_PROJ), dt_),
        jnp.zeros((depth, nb, RWKV_HEADS, RWKV_HEAD, RWKV_HEAD), dt_),
        jnp.zeros((depth, nb, SC_WIDTH - 1, SC_DIM), dt_),
        jnp.zeros((depth, nb, SSM_CONV - 1, SSM_CONV_DIM), dt_),
        jnp.zeros((depth, nb, SSM_HEADS, SSM_HEADDIM, SSM_STATE), dt_),
    )
    y_prompt, p_states = _trunk(x_prompt, zero_states, wp, norm_final)
    sample_states = (state_rwkv_shift, state_rwkv_wkv, state_sc_buf, state_ssm_conv, state_ssm)
    y_sample, s_states = _trunk(x_sample, sample_states, wp, norm_final)
    return (y_prompt, y_sample) + tuple(p_states) + tuple(s_states)
```

```python
import functools

import jax
import jax.numpy as jnp
from jax import lax
from jax.experimental import pallas as pl
from jax.experimental.pallas import tpu as pltpu

F32 = jnp.float32
MXU_DTYPE = jnp.bfloat16
HIGHEST = lax.Precision.HIGHEST

D_MODEL = 1024
D_FF = 2816
RWKV_DIM = 1024
RWKV_HEAD = 64
RWKV_HEADS = 16
DECAY_LORA = 64
AAA_LORA = 64
GATE_LORA = 160
RWKV_PROJ = 3 * RWKV_DIM + DECAY_LORA + AAA_LORA + GATE_LORA
RWKV_GN_EPS = 64e-5
SC_DIM = 512
SC_WIDTH = 3
SSM_DIM = 512
SSM_HEADDIM = 64
SSM_HEADS = 8
SSM_GROUPS = 2
SSM_STATE = 128
SSM_CONV = 4
SSM_CONV_DIM = SSM_DIM + 2 * SSM_GROUPS * SSM_STATE
NORM_EPS = 1e-6

COL_RKV = 0
COL_SC = 3 * RWKV_DIM
COL_Z = COL_SC + 3 * SC_DIM
COL_XBC = COL_Z + SSM_DIM
COL_SMALL = COL_XBC + SSM_CONV_DIM
SMALL_W = 512
SM_GL = 128
SM_DT = 384
PROJ_COLS = COL_SMALL + SMALL_W

LANE = 128
SUBLANE = 8
VMEM_LIMIT = 56 * 1024 * 1024

RWKV_CHUNK = 64
SSD_CHUNK = 128


def _mm(a, b):
    return jnp.dot(a.astype(MXU_DTYPE), b.astype(MXU_DTYPE), preferred_element_type=F32)


def _mm_nt(a, b):
    return lax.dot_general(a.astype(MXU_DTYPE), b.astype(MXU_DTYPE), (((1,), (1,)), ((), ())),
                           preferred_element_type=F32)


def _mm_tn(a, b):
    return lax.dot_general(a.astype(MXU_DTYPE), b.astype(MXU_DTYPE), (((0,), (0,)), ((), ())),
                           preferred_element_type=F32)


def _mmx(a, b):
    return jnp.dot(a, b, precision=HIGHEST, preferred_element_type=F32)


def _mmx_nt(a, b):
    return lax.dot_general(a, b, (((1,), (1,)), ((), ())), precision=HIGHEST, preferred_element_type=F32)


def _mmx_tn(a, b):
    return lax.dot_general(a, b, (((0,), (0,)), ((), ())), precision=HIGHEST, preferred_element_type=F32)


def _sigmoid(x):
    return 1.0 / (1.0 + jnp.exp(-x))


def _silu(x):
    return x * _sigmoid(x)


def _softplus(x):
    return jnp.maximum(x, 0.0) + jnp.log1p(jnp.exp(-jnp.abs(x)))


def _rmsnorm(x, g):
    return x * lax.rsqrt(jnp.mean(x * x, axis=-1, keepdims=True) + NORM_EPS) * g


def _params(*sem):
    return pltpu.CompilerParams(dimension_semantics=sem, vmem_limit_bytes=VMEM_LIMIT)


def _ffn_kernel(x_ref, g_ref, wg_ref, wu_ref, wo_ref, gf_ref, o_ref, h_ref, acc_ref, *, final_norm):
    j = pl.program_id(1)

    @pl.when(j == 0)
    def _():
        h_ref[...] = _rmsnorm(x_ref[...], g_ref[...]).astype(h_ref.dtype)
        acc_ref[...] = jnp.zeros_like(acc_ref)

    h = h_ref[...]
    gate = jnp.dot(h, wg_ref[...], preferred_element_type=F32)
    up = jnp.dot(h, wu_ref[...], preferred_element_type=F32)
    act = (_silu(gate) * up).astype(MXU_DTYPE)
    acc_ref[...] += jnp.dot(act, wo_ref[...], preferred_element_type=F32)

    @pl.when(j == pl.num_programs(1) - 1)
    def _():
        y = x_ref[...] + 0.5 * acc_ref[...]
        if final_norm:
            y = _rmsnorm(y, gf_ref[...])
        o_ref[...] = y


def _ffn(x, norm_g, w_in, w_out, layer, final_g=None, *, tm, tf):
    m = x.shape[0]
    nf = D_FF // tf
    final_norm = final_g is not None
    gf = final_g if final_norm else norm_g[layer]
    return pl.pallas_call(
        functools.partial(_ffn_kernel, final_norm=final_norm),
        grid=(m // tm, nf),
        in_specs=[
            pl.BlockSpec((tm, D_MODEL), lambda i, j: (i, 0)),
            pl.BlockSpec((None, 1, D_MODEL), lambda i, j: (layer, 0, 0)),
            pl.BlockSpec((None, D_MODEL, tf), lambda i, j: (layer, 0, j)),
            pl.BlockSpec((None, D_MODEL, tf), lambda i, j: (layer, 0, nf + j)),
            pl.BlockSpec((None, tf, D_MODEL), lambda i, j: (layer, j, 0)),
            pl.BlockSpec((1, D_MODEL), lambda i, j: (0, 0)),
        ],
        out_specs=pl.BlockSpec((tm, D_MODEL), lambda i, j: (i, 0)),
        out_shape=jax.ShapeDtypeStruct((m, D_MODEL), F32),
        scratch_shapes=[pltpu.VMEM((tm, D_MODEL), MXU_DTYPE), pltpu.VMEM((tm, D_MODEL), F32)],
        compiler_params=_params("parallel", "arbitrary"),
        name="ffn",
    )(x, norm_g.reshape(-1, 1, D_MODEL), w_in, w_in, w_out, gf.reshape(1, D_MODEL))


def _proj_kernel(x_ref, g_ref, w_ref, o_ref, h_ref):
    @pl.when(pl.program_id(1) == 0)
    def _():
        h_ref[...] = _rmsnorm(x_ref[...], g_ref[...]).astype(h_ref.dtype)

    o_ref[...] = jnp.dot(h_ref[...], w_ref[...], preferred_element_type=F32)


def _proj_in(x, norm_g, w_all, layer, *, tm, tn):
    m = x.shape[0]
    return pl.pallas_call(
        _proj_kernel,
        grid=(m // tm, PROJ_COLS // tn),
        in_specs=[
            pl.BlockSpec((tm, D_MODEL), lambda i, j: (i, 0)),
            pl.BlockSpec((None, 1, D_MODEL), lambda i, j: (layer, 0, 0)),
            pl.BlockSpec((None, D_MODEL, tn), lambda i, j: (layer, 0, j)),
        ],
        out_specs=pl.BlockSpec((tm, tn), lambda i, j: (i, j)),
        out_shape=jax.ShapeDtypeStruct((m, PROJ_COLS), F32),
        scratch_shapes=[pltpu.VMEM((tm, D_MODEL), MXU_DTYPE)],
        compiler_params=_params("parallel", "arbitrary"),
        name="proj_in",
    )(x, norm_g.reshape(-1, 1, D_MODEL), w_all)


def _proj_out_kernel(x_ref, y1_ref, y2_ref, y3_ref, w1_ref, w2_ref, w3_ref, o_ref):
    acc = jnp.dot(y1_ref[...].astype(MXU_DTYPE), w1_ref[...], preferred_element_type=F32)
    acc += jnp.dot(y2_ref[...].astype(MXU_DTYPE), w2_ref[...], preferred_element_type=F32)
    acc += jnp.dot(y3_ref[...].astype(MXU_DTYPE), w3_ref[...], preferred_element_type=F32)
    o_ref[...] = x_ref[...] + acc


def _proj_out(x, y_rwkv, y_sc, y_ssm, w_out, layer, *, tm):
    m = x.shape[0]
    return pl.pallas_call(
        _proj_out_kernel,
        grid=(m // tm,),
        in_specs=[
            pl.BlockSpec((tm, D_MODEL), lambda i: (i, 0)),
            pl.BlockSpec((tm, RWKV_DIM), lambda i: (i, 0)),
            pl.BlockSpec((tm, SC_DIM), lambda i: (i, 0)),
            pl.BlockSpec((tm, SSM_DIM), lambda i: (i, 0)),
            pl.BlockSpec((None, RWKV_DIM, D_MODEL), lambda i: (layer, 0, 0)),
            pl.BlockSpec((None, SC_DIM, D_MODEL), lambda i: (layer, RWKV_DIM // SC_DIM, 0)),
            pl.BlockSpec((None, SSM_DIM, D_MODEL), lambda i: (layer, (RWKV_DIM + SC_DIM) // SSM_DIM, 0)),
        ],
        out_specs=pl.BlockSpec((tm, D_MODEL), lambda i: (i, 0)),
        out_shape=jax.ShapeDtypeStruct((m, D_MODEL), F32),
        compiler_params=_params("parallel"),
        name="proj_out",
    )(x, y_rwkv, y_sc, y_ssm, w_out, w_out, w_out)


def _rwkv_prepare(p_rkv, prev_rkv, p_sm, prev_sm, wts):
    (mu_rkv, mu_sm, w0, w2, a0, a2, g2, k_k, k_a) = wts
    xs = p_rkv + (prev_rkv - p_rkv) * mu_rkv
    xm = p_sm + (prev_sm - p_sm) * mu_sm
    r = xs[:, :RWKV_DIM]
    k = xs[:, RWKV_DIM:2 * RWKV_DIM]
    v = xs[:, 2 * RWKV_DIM:]
    w_lr = xm[:, :DECAY_LORA]
    a_lr = xm[:, DECAY_LORA:DECAY_LORA + AAA_LORA]
    g_lr = xm[:, SM_GL:SM_GL + GATE_LORA]
    w = -_softplus(-(w0 + _mm(jnp.tanh(w_lr), w2))) - 0.5
    log_decay = -jnp.exp(w)
    a = _sigmoid(a0 + _mm(a_lr, a2))
    g = _mm(_sigmoid(g_lr), g2)
    kk = k * k_k
    k_mod = k * (1.0 + (a - 1.0) * k_a)
    return r, log_decay, k_mod, v, kk, a, g


def _rwkv_weight_specs(layer):
    def vec(width):
        return pl.BlockSpec((None, 1, width), lambda *idx: (layer, 0, 0))

    def mat(rows):
        return pl.BlockSpec((None, rows, RWKV_DIM), lambda *idx: (layer, 0, 0))

    return [vec(3 * RWKV_DIM), vec(SMALL_W), vec(RWKV_DIM), mat(DECAY_LORA), vec(RWKV_DIM), mat(AAA_LORA),
            mat(GATE_LORA), vec(RWKV_DIM), vec(RWKV_DIM), vec(RWKV_DIM), vec(RWKV_DIM), vec(RWKV_DIM)]


def _rwkv_weight_args(wp):
    return (wp["mu_rkv"], wp["mu_sm"], wp["rwkv_w0"], wp["rwkv_w2"], wp["rwkv_a0"], wp["rwkv_a2"],
            wp["rwkv_g2"], wp["rwkv_k_k"], wp["rwkv_k_a"], wp["rwkv_r_k"], wp["rwkv_ln_w"], wp["rwkv_ln_b"])


HEADS_PER_GROUP = 4
GROUP_W = HEADS_PER_GROUP * RWKV_HEAD
RWKV_GROUPS = RWKV_HEADS // HEADS_PER_GROUP


def _split3(x):
    hi = x.astype(MXU_DTYPE)
    r1 = x - hi.astype(F32)
    mid = r1.astype(MXU_DTYPE)
    lo = (r1 - mid.astype(F32)).astype(MXU_DTYPE)
    return hi, mid, lo


def _block_diag(x, head_masks):
    return jnp.concatenate([x * m for m in head_masks], axis=0)


def _rwkv_chunk_kernel(p_rkv_ref, p_sm_ref, sh_rkv_ref, sh_sm_ref, s0_ref,
                       mu_rkv_ref, mu_sm_ref, w0_ref, w2_ref, a0_ref, a2_ref, g2_ref, kk_ref, ka_ref,
                       rk_ref, lnw_ref, lnb_ref,
                       y_ref, s_out_ref,
                       pad_rkv, pad_sm, state, at_s, rt_s, bt_s, kt_s, bh_s, kh_s, v_s, yn_s):
    c = RWKV_CHUNK
    n = RWKV_HEAD
    gw = GROUP_W
    t = pl.program_id(1)

    @pl.when(t == 0)
    def _():
        pad_rkv[SUBLANE - 1:SUBLANE, :] = sh_rkv_ref[0]
        pad_sm[SUBLANE - 1:SUBLANE, :] = sh_sm_ref[0]
        for h in range(RWKV_HEADS):
            state[:, h * n:(h + 1) * n] = s0_ref[0, h]

    pad_rkv[SUBLANE:SUBLANE + c, :] = p_rkv_ref[...]
    pad_sm[SUBLANE:SUBLANE + c, :] = p_sm_ref[...]
    wts = (mu_rkv_ref[...], mu_sm_ref[...], w0_ref[...], w2_ref[...], a0_ref[...], a2_ref[...],
           g2_ref[...], kk_ref[...], ka_ref[...])
    r, log_decay, k_mod, v, kk, a, g = _rwkv_prepare(
        p_rkv_ref[...], pad_rkv[SUBLANE - 1:SUBLANE - 1 + c, :],
        p_sm_ref[...], pad_sm[SUBLANE - 1:SUBLANE - 1 + c, :], wts)
    pad_rkv[SUBLANE - 1:SUBLANE, :] = p_rkv_ref[c - 1:c, :]
    pad_sm[SUBLANE - 1:SUBLANE, :] = p_sm_ref[c - 1:c, :]

    row = lax.broadcasted_iota(jnp.int32, (c, gw), 0)
    lane = lax.broadcasted_iota(jnp.int32, (c, gw), 1)
    pos = lane % n
    strict = row > pos
    incl = row >= pos
    eye = (row == pos).astype(F32)
    head_masks = [(lane // n == hh).astype(MXU_DTYPE) for hh in range(HEADS_PER_GROUP)]
    sq_r = lax.broadcasted_iota(jnp.int32, (gw, gw), 0) // n
    sq_c = lax.broadcasted_iota(jnp.int32, (gw, gw), 1) // n
    ones_blk = (sq_r == sq_c).astype(MXU_DTYPE)
    tri_r = lax.broadcasted_iota(jnp.int32, (c, c), 0)
    tri_c = lax.broadcasted_iota(jnp.int32, (c, c), 1)
    tri = (tri_r >= tri_c).astype(MXU_DTYPE)

    def head_sum(x):
        return jnp.concatenate(
            [jnp.dot(x[:, gi * gw:(gi + 1) * gw].astype(MXU_DTYPE), ones_blk, preferred_element_type=F32)
             for gi in range(RWKV_GROUPS)], axis=1)

    hi, mid, lo = _split3(log_decay)
    cum = (jnp.dot(tri, hi, preferred_element_type=F32) + jnp.dot(tri, mid, preferred_element_type=F32)
           + jnp.dot(tri, lo, preferred_element_type=F32))
    cum_end = cum[c - 1:c, :]
    e_out = jnp.exp(-cum)
    e_end = jnp.exp(cum_end - cum)
    kk = kk * lax.rsqrt(jnp.maximum(head_sum(kk * kk), 1e-24))
    b = kk * a
    at_s[...] = (-kk * jnp.exp(cum - log_decay)).astype(MXU_DTYPE)
    rt_s[...] = (r * jnp.exp(cum)).astype(MXU_DTYPE)
    bt_s[...] = (b * e_out).astype(MXU_DTYPE)
    kt_s[...] = (k_mod * e_out).astype(MXU_DTYPE)
    bh_s[...] = (b * e_end).astype(MXU_DTYPE)
    kh_s[...] = (k_mod * e_end).astype(MXU_DTYPE)
    v_s[...] = v
    bonus = head_sum(r * k_mod * rk_ref[...])
    w_end = jnp.exp(cum_end)

    def mm(x, y):
        return jnp.dot(x.astype(MXU_DTYPE), y, preferred_element_type=F32)

    def bd(x):
        return _block_diag(x.astype(MXU_DTYPE), head_masks)

    groups = range(RWKV_GROUPS)
    sls = [slice(gi * gw, (gi + 1) * gw) for gi in groups]
    nt = (((1,), (1,)), ((), ()))
    ar = [jnp.concatenate([at_s[:, sl], rt_s[:, sl]], axis=0) for sl in sls]
    vg = [v_s[:, sl].astype(MXU_DTYPE) for sl in sls]
    gram_b = [lax.dot_general(ar[gi], bd(bt_s[:, sls[gi]]), nt, preferred_element_type=F32) for gi in groups]
    gram_k = [lax.dot_general(ar[gi], bd(kt_s[:, sls[gi]]), nt, preferred_element_type=F32) for gi in groups]
    l_ab = [jnp.where(strict, x[:c], 0.0) for x in gram_b]
    m_rb = [jnp.where(incl, x[c:], 0.0) for x in gram_b]
    l_akrk = [jnp.where(jnp.concatenate([strict, incl], axis=0), x, 0.0) for x in gram_k]
    inv = [eye + x for x in l_ab]
    power = l_ab
    power_bd = [bd(x) for x in power]
    span = 2
    while span < c:
        power = [mm(power[gi], power_bd[gi]) for gi in groups]
        power_bd = [bd(x) for x in power]
        inv = [inv[gi] + mm(inv[gi], power_bd[gi]) for gi in groups]
        span *= 2
    s0 = [state[:, sl] for sl in sls]
    z = [mm(l_akrk[gi], bd(vg[gi])) + lax.dot_general(ar[gi], bd(s0[gi]), nt, preferred_element_type=F32)
         for gi in groups]
    u = [mm(inv[gi], bd(z[gi][:c])) for gi in groups]
    y = [z[gi][c:] + mm(m_rb[gi], bd(u[gi])) for gi in groups]
    for gi in groups:
        uv = jnp.concatenate([u[gi].astype(MXU_DTYPE), vg[gi]], axis=0)
        bk = jnp.concatenate([bh_s[:, sls[gi]], kh_s[:, sls[gi]]], axis=0)
        upd = lax.dot_general(uv, bk, (((0,), (0,)), ((), ())), preferred_element_type=F32)
        s_new = s0[gi] * w_end[:, sls[gi]]
        for hh in range(HEADS_PER_GROUP):
            s_new += upd[hh * n:(hh + 1) * n, :] * head_masks[hh].astype(F32)
        state[:, sls[gi]] = s_new
    mean = [mm(x, ones_blk) * (1.0 / n) for x in y]
    yc = [y[gi] - mean[gi] for gi in groups]
    var = [mm(x * x, ones_blk) * (1.0 / n) for x in yc]
    for gi in groups:
        yn_s[:, sls[gi]] = yc[gi] * lax.rsqrt(var[gi] + RWKV_GN_EPS)

    y_ref[...] = (yn_s[...] * lnw_ref[...] + lnb_ref[...] + bonus * v_s[...]) * g

    @pl.when(t == pl.num_programs(1) - 1)
    def _():
        for h in range(RWKV_HEADS):
            s_out_ref[0, h] = state[:, h * n:(h + 1) * n]


def _rwkv_prompt(proj, shift_rkv, shift_sm, wkv0, wp, layer, *, batch, seqlen):
    c = RWKV_CHUNK
    nt = seqlen // c
    return pl.pallas_call(
        _rwkv_chunk_kernel,
        grid=(batch, nt),
        in_specs=[
            pl.BlockSpec((c, 3 * RWKV_DIM), lambda b, t: (b * nt + t, COL_RKV // (3 * RWKV_DIM))),
            pl.BlockSpec((c, SMALL_W), lambda b, t: (b * nt + t, COL_SMALL // SMALL_W)),
            pl.BlockSpec((1, 1, 3 * RWKV_DIM), lambda b, t: (b, 0, 0)),
            pl.BlockSpec((1, 1, SMALL_W), lambda b, t: (b, 0, 0)),
            pl.BlockSpec((1, RWKV_HEADS, RWKV_HEAD, RWKV_HEAD), lambda b, t: (b, 0, 0, 0)),
        ] + _rwkv_weight_specs(layer),
        out_specs=[
            pl.BlockSpec((c, RWKV_DIM), lambda b, t: (b * nt + t, 0)),
            pl.BlockSpec((1, RWKV_HEADS, RWKV_HEAD, RWKV_HEAD), lambda b, t: (b, 0, 0, 0)),
        ],
        out_shape=[
            jax.ShapeDtypeStruct((batch * seqlen, RWKV_DIM), F32),
            jax.ShapeDtypeStruct((batch, RWKV_HEADS, RWKV_HEAD, RWKV_HEAD), F32),
        ],
        scratch_shapes=[
            pltpu.VMEM((SUBLANE + c, 3 * RWKV_DIM), F32),
            pltpu.VMEM((SUBLANE + c, SMALL_W), F32),
            pltpu.VMEM((RWKV_HEAD, RWKV_DIM), F32),
        ] + [pltpu.VMEM((c, RWKV_DIM), MXU_DTYPE) for _ in range(6)]
          + [pltpu.VMEM((c, RWKV_DIM), F32) for _ in range(2)],
        compiler_params=_params("parallel", "arbitrary"),
        name="rwkv_chunk",
    )(proj, proj, shift_rkv, shift_sm, wkv0, *_rwkv_weight_args(wp))


def _sc_kernel(p_ref, buf_ref, w_ref, g_ref, y_ref, buf_out_ref, pad, *, tl):
    t = pl.program_id(1)
    hist = SC_WIDTH - 1

    @pl.when(t == 0)
    def _():
        pad[SUBLANE - hist:SUBLANE, :] = buf_ref[0]

    b_gate = p_ref[:, :SC_DIM]
    u = p_ref[:, SC_DIM:2 * SC_DIM] * p_ref[:, 2 * SC_DIM:]
    pad[SUBLANE:SUBLANE + tl, :] = u
    conv = u * w_ref[hist:hist + 1, :]
    for j in range(hist):
        conv += pad[SUBLANE - hist + j:SUBLANE - hist + j + tl, :] * w_ref[j:j + 1, :]
    y_ref[...] = _rmsnorm(b_gate * conv, g_ref[...])
    last = pad[tl + SUBLANE - hist:tl + SUBLANE, :]
    pad[SUBLANE - hist:SUBLANE, :] = last

    @pl.when(t == pl.num_programs(1) - 1)
    def _():
        buf_out_ref[0] = last


def _sc_prompt(proj, buf0, wp, layer, *, batch, seqlen, tl):
    nt = seqlen // tl
    hist = SC_WIDTH - 1
    return pl.pallas_call(
        functools.partial(_sc_kernel, tl=tl),
        grid=(batch, nt),
        in_specs=[
            pl.BlockSpec((tl, 3 * SC_DIM), lambda b, t: (b * nt + t, COL_SC // (3 * SC_DIM))),
            pl.BlockSpec((1, hist, SC_DIM), lambda b, t: (b, 0, 0)),
            pl.BlockSpec((None, SC_WIDTH, SC_DIM), lambda b, t: (layer, 0, 0)),
            pl.BlockSpec((None, 1, SC_DIM), lambda b, t: (layer, 0, 0)),
        ],
        out_specs=[
            pl.BlockSpec((tl, SC_DIM), lambda b, t: (b * nt + t, 0)),
            pl.BlockSpec((1, hist, SC_DIM), lambda b, t: (b, 0, 0)),
        ],
        out_shape=[
            jax.ShapeDtypeStruct((batch * seqlen, SC_DIM), F32),
            jax.ShapeDtypeStruct((batch, hist, SC_DIM), F32),
        ],
        scratch_shapes=[pltpu.VMEM((SUBLANE + tl, SC_DIM), F32)],
        compiler_params=_params("parallel", "arbitrary"),
        name="short_conv",
    )(proj, buf0, wp["sc_conv_w"], wp["sc_norm"])


def _ssd_kernel(z_ref, xbc_ref, dt_ref, buf_ref, h0_ref, cw_ref, cb_ref, dtb_ref, alog_ref, dskip_ref, g_ref,
                y_ref, buf_out_ref, h_out_ref, pad, state, y_s):
    q = SSD_CHUNK
    t = pl.program_id(1)
    hist = SSM_CONV - 1
    gw = SSM_DIM // SSM_GROUPS

    @pl.when(t == 0)
    def _():
        pad[SUBLANE - hist:SUBLANE, :] = buf_ref[0]
        state[...] = h0_ref[0]

    xbc = xbc_ref[...]
    pad[SUBLANE:SUBLANE + q, :] = xbc
    conv = xbc * cw_ref[hist:hist + 1, :]
    for j in range(hist):
        conv += pad[SUBLANE - hist + j:SUBLANE - hist + j + q, :] * cw_ref[j:j + 1, :]
    last = pad[q + SUBLANE - hist:q + SUBLANE, :]
    pad[SUBLANE - hist:SUBLANE, :] = last
    xc = _silu(conv + cb_ref[...])
    xh = xc[:, :SSM_DIM]

    dt = _softplus(dt_ref[...] + dtb_ref[...])
    neg_a = -jnp.exp(alog_ref[...])
    row = lax.broadcasted_iota(jnp.int32, (q, q), 0)
    col = lax.broadcasted_iota(jnp.int32, (q, q), 1)
    causal = row >= col
    cum = _mmx(causal.astype(F32), dt * neg_a)
    cum_t = cum.T
    cum_end = cum[q - 1:q, :]
    heads_per_group = SSM_HEADS // SSM_GROUPS

    for grp in range(SSM_GROUPS):
        bm = xc[:, SSM_DIM + grp * SSM_STATE:SSM_DIM + (grp + 1) * SSM_STATE]
        cm = xc[:, SSM_DIM + (SSM_GROUPS + grp) * SSM_STATE:SSM_DIM + (SSM_GROUPS + grp + 1) * SSM_STATE]
        gram = _mm_nt(cm, bm)
        for hh in range(heads_per_group):
            h = grp * heads_per_group + hh
            sl = slice(h * SSM_HEADDIM, (h + 1) * SSM_HEADDIM)
            cum_col = cum[:, h:h + 1]
            seg = cum_col - cum_t[h:h + 1, :]
            decay = jnp.where(causal, jnp.exp(jnp.where(causal, seg, 0.0)), 0.0)
            xdt = xh[:, sl] * dt[:, h:h + 1]
            y = _mm(gram * decay, xdt)
            h_prev = state[h]
            y += _mm_nt(cm, h_prev) * jnp.exp(cum_col)
            end = cum_end[:, h:h + 1]
            state[h] = h_prev * jnp.exp(end) + _mm_tn(xdt * jnp.exp(end - cum_col), bm)
            y_s[:, sl] = y + dskip_ref[:, h:h + 1] * xh[:, sl]

    y = y_s[...] * _silu(z_ref[...])
    for grp in range(SSM_GROUPS):
        yg = y[:, grp * gw:(grp + 1) * gw]
        y_ref[:, grp * gw:(grp + 1) * gw] = _rmsnorm(yg, g_ref[:, grp * gw:(grp + 1) * gw])

    @pl.when(t == pl.num_programs(1) - 1)
    def _():
        buf_out_ref[0] = last
        h_out_ref[0] = state[...]


def _ssd_prompt(proj, buf0, h0, wp, layer, *, batch, seqlen):
    q = SSD_CHUNK
    nt = seqlen // q
    hist = SSM_CONV - 1
    lane_vec = pl.BlockSpec((None, 1, LANE), lambda b, t: (layer, 0, 0))
    return pl.pallas_call(
        _ssd_kernel,
        grid=(batch, nt),
        in_specs=[
            pl.BlockSpec((q, SSM_DIM), lambda b, t: (b * nt + t, COL_Z // SSM_DIM)),
            pl.BlockSpec((q, SSM_CONV_DIM), lambda b, t: (b * nt + t, COL_XBC // SSM_CONV_DIM)),
            pl.BlockSpec((q, LANE), lambda b, t: (b * nt + t, (COL_SMALL + SM_DT) // LANE)),
            pl.BlockSpec((1, hist, SSM_CONV_DIM), lambda b, t: (b, 0, 0)),
            pl.BlockSpec((1, SSM_HEADS, SSM_HEADDIM, SSM_STATE), lambda b, t: (b, 0, 0, 0)),
            pl.BlockSpec((None, SSM_CONV, SSM_CONV_DIM), lambda b, t: (layer, 0, 0)),
            pl.BlockSpec((None, 1, SSM_CONV_DIM), lambda b, t: (layer, 0, 0)),
            lane_vec, lane_vec, lane_vec,
            pl.BlockSpec((None, 1, SSM_DIM), lambda b, t: (layer, 0, 0)),
        ],
        out_specs=[
            pl.BlockSpec((q, SSM_DIM), lambda b, t: (b * nt + t, 0)),
            pl.BlockSpec((1, hist, SSM_CONV_DIM), lambda b, t: (b, 0, 0)),
            pl.BlockSpec((1, SSM_HEADS, SSM_HEADDIM, SSM_STATE), lambda b, t: (b, 0, 0, 0)),
        ],
        out_shape=[
            jax.ShapeDtypeStruct((batch * seqlen, SSM_DIM), F32),
            jax.ShapeDtypeStruct((batch, hist, SSM_CONV_DIM), F32),
            jax.ShapeDtypeStruct((batch, SSM_HEADS, SSM_HEADDIM, SSM_STATE), F32),
        ],
        scratch_shapes=[
            pltpu.VMEM((SUBLANE + q, SSM_CONV_DIM), F32),
            pltpu.VMEM((SSM_HEADS, SSM_HEADDIM, SSM_STATE), F32),
            pltpu.VMEM((q, SSM_DIM), F32),
        ],
        compiler_params=_params("parallel", "arbitrary"),
        name="ssd_chunk",
    )(proj, proj, proj, buf0, h0, wp["ssm_conv_w"], wp["ssm_conv_b"], wp["dt_bias_pad"], wp["a_log_pad"],
      wp["d_pad"], wp["ssm_norm"])


def _row_to_col(x_row, eye):
    return jnp.sum(jnp.where(eye, x_row, 0.0), axis=1, keepdims=True)


def _col_to_row(x_col, eye):
    return jnp.sum(jnp.where(eye, x_col, 0.0), axis=0, keepdims=True)


def _step_kernel(p_rkv_ref, p_sm_ref, p_sc_ref, z_ref, xbc_ref,
                 sh_rkv_ref, sh_sm_ref, wkv_ref, scbuf_ref, ssmbuf_ref, ssm_ref,
                 mu_rkv_ref, mu_sm_ref, w0_ref, w2_ref, a0_ref, a2_ref, g2_ref, kk_ref, ka_ref,
                 rk_ref, lnw_ref, lnb_ref,
                 scw_ref, scg_ref, cw_ref, cb_ref, dtb_ref, alog_ref, dskip_ref, ssmg_ref,
                 y_rwkv_ref, y_sc_ref, y_ssm_ref, wkv_out_ref, scbuf_out_ref, ssmbuf_out_ref, ssm_out_ref,
                 r_s, w_s, k_s, v_s, kk_s, b_s, y_s, xh_s, bc_s, dt_s, ys_s):
    n = RWKV_HEAD
    eye = lax.broadcasted_iota(jnp.int32, (n, n), 0) == lax.broadcasted_iota(jnp.int32, (n, n), 1)

    wts = (mu_rkv_ref[...], mu_sm_ref[...], w0_ref[...], w2_ref[...], a0_ref[...], a2_ref[...],
           g2_ref[...], kk_ref[...], ka_ref[...])
    r, log_decay, k_mod, v, kk, a, g = _rwkv_prepare(p_rkv_ref[0], sh_rkv_ref[0], p_sm_ref[0], sh_sm_ref[0], wts)
    r_s[...] = r
    w_s[...] = jnp.exp(log_decay)
    k_s[...] = k_mod
    v_s[...] = v
    kk_s[...] = kk
    b_s[...] = a
    heads = range(RWKV_HEADS)
    sls = [slice(h * n, (h + 1) * n) for h in heads]
    kkh = [kk_s[:, sl] for sl in sls]
    norm2 = [jnp.sum(x * x, axis=-1, keepdims=True) for x in kkh]
    kkh = [kkh[h] * lax.rsqrt(jnp.maximum(norm2[h], 1e-24)) for h in heads]
    bh = [kkh[h] * b_s[:, sls[h]] for h in heads]
    sa = [jnp.sum(wkv_ref[0, h] * (-kkh[h]), axis=1, keepdims=True) for h in heads]
    v_col = [_row_to_col(v_s[:, sl], eye) for sl in sls]
    for h in heads:
        wkv_out_ref[0, h] = wkv_ref[0, h] * w_s[:, sls[h]] + sa[h] * bh[h] + v_col[h] * k_s[:, sls[h]]
    y_col = [jnp.sum(wkv_out_ref[0, h] * r_s[:, sls[h]], axis=1, keepdims=True) for h in heads]
    y = [_col_to_row(x, eye) for x in y_col]
    mean = [jnp.mean(x, axis=-1, keepdims=True) for x in y]
    yc = [y[h] - mean[h] for h in heads]
    var = [jnp.mean(x * x, axis=-1, keepdims=True) for x in yc]
    bonus = [jnp.sum(r_s[:, sl] * k_s[:, sl] * rk_ref[:, sl], axis=-1, keepdims=True) for sl in sls]
    for h in heads:
        sl = sls[h]
        y_s[:, sl] = (yc[h] * lax.rsqrt(var[h] + RWKV_GN_EPS) * lnw_ref[:, sl] + lnb_ref[:, sl]
                      + bonus[h] * v_s[:, sl])
    y_rwkv_ref[0] = y_s[...] * g

    p_sc = p_sc_ref[0]
    u = p_sc[:, SC_DIM:2 * SC_DIM] * p_sc[:, 2 * SC_DIM:]
    buf = scbuf_ref[0]
    conv = buf[0:1] * scw_ref[0:1, :] + buf[1:2] * scw_ref[1:2, :] + u * scw_ref[2:3, :]
    y_sc_ref[0] = _rmsnorm(p_sc[:, :SC_DIM] * conv, scg_ref[...])
    scbuf_out_ref[0, 0:1] = buf[1:2]
    scbuf_out_ref[0, 1:2] = u

    xbc = xbc_ref[0]
    cbuf = ssmbuf_ref[0]
    conv = xbc * cw_ref[3:4, :]
    for j in range(SSM_CONV - 1):
        conv += cbuf[j:j + 1] * cw_ref[j:j + 1, :]
    ssmbuf_out_ref[0, 0:2] = cbuf[1:3]
    ssmbuf_out_ref[0, 2:3] = xbc
    xc = _silu(conv + cb_ref[...])
    xh_s[...] = xc[:, :SSM_DIM]
    bc_s[...] = xc[:, SSM_DIM:]
    dt = _softplus(p_sm_ref[0][:, SM_DT:SM_DT + LANE] + dtb_ref[...])
    dt_s[...] = dt
    decay_all = jnp.exp(dt * (-jnp.exp(alog_ref[...])))
    heads_per_group = SSM_HEADS // SSM_GROUPS
    heads = range(SSM_HEADS)
    sls = [slice(h * SSM_HEADDIM, (h + 1) * SSM_HEADDIM) for h in heads]
    bms = [bc_s[:, g * SSM_STATE:(g + 1) * SSM_STATE] for g in range(SSM_GROUPS)]
    cms = [bc_s[:, (SSM_GROUPS + g) * SSM_STATE:(SSM_GROUPS + g + 1) * SSM_STATE] for g in range(SSM_GROUPS)]
    xdt_col = [_row_to_col(xh_s[:, sls[h]] * dt_s[:, h:h + 1], eye) for h in heads]
    for h in heads:
        ssm_out_ref[0, h] = ssm_ref[0, h] * decay_all[:, h:h + 1] + xdt_col[h] * bms[h // heads_per_group]
    y_col = [jnp.sum(ssm_out_ref[0, h] * cms[h // heads_per_group], axis=1, keepdims=True) for h in heads]
    y = [_col_to_row(x, eye) for x in y_col]
    for h in heads:
        ys_s[:, sls[h]] = y[h] + dskip_ref[:, h:h + 1] * xh_s[:, sls[h]]
    y = ys_s[...] * _silu(z_ref[0])
    gw = SSM_DIM // SSM_GROUPS
    for grp in range(SSM_GROUPS):
        yg = y[:, grp * gw:(grp + 1) * gw]
        y_ssm_ref[0, :, grp * gw:(grp + 1) * gw] = _rmsnorm(yg, ssmg_ref[:, grp * gw:(grp + 1) * gw])


def _mixers_step(proj, states, wp, layer, *, batch):
    shift_rkv, shift_sm, wkv, sc_buf, ssm_buf, ssm = states
    p3 = proj.reshape(batch, 1, PROJ_COLS)
    row = lambda width, blk: pl.BlockSpec((1, 1, width), lambda b: (b, 0, blk))
    st3 = lambda d1, d2: pl.BlockSpec((1, d1, d2), lambda b: (b, 0, 0))
    st4 = lambda d1, d2, d3: pl.BlockSpec((1, d1, d2, d3), lambda b: (b, 0, 0, 0))
    vec = lambda width: pl.BlockSpec((None, 1, width), lambda b: (layer, 0, 0))
    mat = lambda rows, width: pl.BlockSpec((None, rows, width), lambda b: (layer, 0, 0))
    one = lambda width: pltpu.VMEM((1, width), F32)
    outs = pl.pallas_call(
        _step_kernel,
        grid=(batch,),
        in_specs=[
            row(3 * RWKV_DIM, COL_RKV // (3 * RWKV_DIM)),
            row(SMALL_W, COL_SMALL // SMALL_W),
            row(3 * SC_DIM, COL_SC // (3 * SC_DIM)),
            row(SSM_DIM, COL_Z // SSM_DIM),
            row(SSM_CONV_DIM, COL_XBC // SSM_CONV_DIM),
            st3(1, 3 * RWKV_DIM), st3(1, SMALL_W),
            st4(RWKV_HEADS, RWKV_HEAD, RWKV_HEAD),
            st3(SC_WIDTH - 1, SC_DIM), st3(SSM_CONV - 1, SSM_CONV_DIM),
            st4(SSM_HEADS, SSM_HEADDIM, SSM_STATE),
        ] + [vec(3 * RWKV_DIM), vec(SMALL_W), vec(RWKV_DIM), mat(DECAY_LORA, RWKV_DIM), vec(RWKV_DIM),
             mat(AAA_LORA, RWKV_DIM), mat(GATE_LORA, RWKV_DIM), vec(RWKV_DIM), vec(RWKV_DIM), vec(RWKV_DIM),
             vec(RWKV_DIM), vec(RWKV_DIM),
             mat(SC_WIDTH, SC_DIM), vec(SC_DIM), mat(SSM_CONV, SSM_CONV_DIM), vec(SSM_CONV_DIM),
             vec(LANE), vec(LANE), vec(LANE), vec(SSM_DIM)],
        out_specs=[
            st3(1, RWKV_DIM), st3(1, SC_DIM), st3(1, SSM_DIM),
            st4(RWKV_HEADS, RWKV_HEAD, RWKV_HEAD),
            st3(SC_WIDTH - 1, SC_DIM), st3(SSM_CONV - 1, SSM_CONV_DIM),
            st4(SSM_HEADS, SSM_HEADDIM, SSM_STATE),
        ],
        out_shape=[
            jax.ShapeDtypeStruct((batch, 1, RWKV_DIM), F32),
            jax.ShapeDtypeStruct((batch, 1, SC_DIM), F32),
            jax.ShapeDtypeStruct((batch, 1, SSM_DIM), F32),
            jax.ShapeDtypeStruct(wkv.shape, F32),
            jax.ShapeDtypeStruct(sc_buf.shape, F32),
            jax.ShapeDtypeStruct(ssm_buf.shape, F32),
            jax.ShapeDtypeStruct(ssm.shape, F32),
        ],
        scratch_shapes=[one(RWKV_DIM) for _ in range(7)] + [one(SSM_DIM), one(2 * SSM_GROUPS * SSM_STATE),
                                                            one(LANE), one(SSM_DIM)],
        compiler_params=_params("parallel"),
        name="mixers_step",
    )(p3, p3, p3, p3, p3, shift_rkv, shift_sm, wkv, sc_buf, ssm_buf, ssm,
      *_rwkv_weight_args(wp), wp["sc_conv_w"], wp["sc_norm"], wp["ssm_conv_w"], wp["ssm_conv_b"],
      wp["dt_bias_pad"], wp["a_log_pad"], wp["d_pad"], wp["ssm_norm"])
    y_rwkv, y_sc, y_ssm, wkv_n, sc_n, ssmbuf_n, ssm_n = outs
    return (y_rwkv.reshape(batch, RWKV_DIM), y_sc.reshape(batch, SC_DIM), y_ssm.reshape(batch, SSM_DIM),
            wkv_n, sc_n, ssmbuf_n, ssm_n)


def _reorder_cols(a):
    lead = a.shape[:-1]
    rkv = a[..., :3 * RWKV_DIM]
    lora = a[..., 3 * RWKV_DIM:RWKV_PROJ]
    small = jnp.concatenate([lora, jnp.zeros(lead + (SMALL_W - lora.shape[-1],), a.dtype)], axis=-1)
    return rkv, small


def _prepare_weights(w):
    depth = w["w_in"].shape[0]
    wp = {}
    w_in = w["w_in"]
    rkv, small = _reorder_cols(w_in[..., :RWKV_PROJ])
    o_sc = RWKV_PROJ
    o_ssm = RWKV_PROJ + 3 * SC_DIM
    dt_cols = w_in[..., o_ssm + SSM_DIM + SSM_CONV_DIM:]
    small = small.at[..., SM_DT:SM_DT + SSM_HEADS].set(dt_cols)
    wp["w_all"] = jnp.concatenate(
        [rkv, w_in[..., o_sc:o_ssm], w_in[..., o_ssm:o_ssm + SSM_DIM + SSM_CONV_DIM], small],
        axis=-1).astype(MXU_DTYPE)
    mu_rkv, mu_sm = _reorder_cols(w["rwkv_mu"])
    wp["mu_rkv"] = mu_rkv.reshape(depth, 1, -1)
    wp["mu_sm"] = mu_sm.reshape(depth, 1, -1)
    for name in ("rwkv_w0", "rwkv_a0", "rwkv_k_k", "rwkv_k_a", "rwkv_ln_w", "rwkv_ln_b", "sc_norm",
                 "ssm_conv_b", "ssm_norm"):
        wp[name] = w[name].reshape(depth, 1, -1)
    wp["rwkv_r_k"] = w["rwkv_r_k"].reshape(depth, 1, RWKV_DIM)
    for name in ("rwkv_w2", "rwkv_a2", "rwkv_g2"):
        wp[name] = w[name].astype(MXU_DTYPE)
    wp["sc_conv_w"] = w["sc_conv_w"]
    wp["ssm_conv_w"] = w["ssm_conv_w"]
    pad_lane = lambda a: jnp.pad(a, ((0, 0), (0, LANE - a.shape[-1]))).reshape(depth, 1, LANE)
    wp["dt_bias_pad"] = pad_lane(w["ssm_dt_bias"])
    wp["a_log_pad"] = pad_lane(w["ssm_A_log"])
    wp["d_pad"] = pad_lane(w["ssm_D"])
    for name in ("ffn1_w_in", "ffn1_w_out", "ffn2_w_in", "ffn2_w_out", "w_out"):
        wp[name] = w[name].astype(MXU_DTYPE)
    for name in ("norm_ffn1", "norm_mix", "norm_ffn2"):
        wp[name] = w[name]
    return wp


def _shift_state_to_cols(proj_last):
    return jnp.concatenate([proj_last[..., :3 * RWKV_DIM],
                            proj_last[..., COL_SMALL:COL_SMALL + RWKV_PROJ - 3 * RWKV_DIM]], axis=-1)


def _tile(m, pref):
    t = min(m, pref)
    while m % t:
        t //= 2
    return t


def _trunk(x3, states, wp, norm_final):
    batch, seqlen, _ = x3.shape
    m = batch * seqlen
    x = x3.reshape(m, D_MODEL)
    depth = wp["w_all"].shape[0]
    tm = _tile(m, 512)
    new = ([], [], [], [], [])
    for layer in range(depth):
        shift, wkv, sc_buf, ssm_buf, ssm = (s[layer] for s in states)
        shift_rkv, shift_sm = _reorder_cols(shift)
        shift_rkv = shift_rkv.reshape(batch, 1, -1)
        shift_sm = shift_sm.reshape(batch, 1, -1)
        x = _ffn(x, wp["norm_ffn1"], wp["ffn1_w_in"], wp["ffn1_w_out"], layer, tm=tm, tf=1408)
        proj = _proj_in(x, wp["norm_mix"], wp["w_all"], layer, tm=tm, tn=PROJ_COLS // 13)
        if seqlen == 1:
            y_rwkv, y_sc, y_ssm, wkv_n, sc_n, ssmbuf_n, ssm_n = _mixers_step(
                proj, (shift_rkv, shift_sm, wkv, sc_buf, ssm_buf, ssm), wp, layer, batch=batch)
        else:
            y_rwkv, wkv_n = _rwkv_prompt(proj, shift_rkv, shift_sm, wkv, wp, layer, batch=batch, seqlen=seqlen)
            y_sc, sc_n = _sc_prompt(proj, sc_buf, wp, layer, batch=batch, seqlen=seqlen, tl=_tile(seqlen, 256))
            y_ssm, ssmbuf_n, ssm_n = _ssd_prompt(proj, ssm_buf, ssm, wp, layer, batch=batch, seqlen=seqlen)
        shift_n = _shift_state_to_cols(proj.reshape(batch, seqlen, PROJ_COLS)[:, -1])
        x = _proj_out(x, y_rwkv, y_sc, y_ssm, wp["w_out"], layer, tm=tm)
        x = _ffn(x, wp["norm_ffn2"], wp["ffn2_w_in"], wp["ffn2_w_out"], layer,
                 final_g=norm_final if layer == depth - 1 else None, tm=tm, tf=1408)
        for lst, s in zip(new, (shift_n, wkv_n, sc_n, ssmbuf_n, ssm_n)):
            lst.append(s)
    return x.reshape(batch, seqlen, D_MODEL), tuple(jnp.stack(lst) for lst in new)


def kernel(x_prompt, x_sample, state_rwkv_shift, state_rwkv_wkv, state_sc_buf, state_ssm_conv, state_ssm,
           norm_ffn1, ffn1_w_in, ffn1_w_out, norm_mix, w_in, rwkv_mu, rwkv_w0, rwkv_w2, rwkv_a0, rwkv_a2,
           rwkv_g2, rwkv_k_k, rwkv_k_a, rwkv_r_k, rwkv_ln_w, rwkv_ln_b, sc_conv_w, sc_norm, ssm_conv_w,
           ssm_conv_b, ssm_dt_bias, ssm_A_log, ssm_D, ssm_norm, w_out, norm_ffn2, ffn2_w_in, ffn2_w_out,
           norm_final):
    weights = dict(
        norm_ffn1=norm_ffn1, ffn1_w_in=ffn1_w_in, ffn1_w_out=ffn1_w_out, norm_mix=norm_mix, w_in=w_in,
        rwkv_mu=rwkv_mu, rwkv_w0=rwkv_w0, rwkv_w2=rwkv_w2, rwkv_a0=rwkv_a0, rwkv_a2=rwkv_a2, rwkv_g2=rwkv_g2,
        rwkv_k_k=rwkv_k_k, rwkv_k_a=rwkv_k_a, rwkv_r_k=rwkv_r_k, rwkv_ln_w=rwkv_ln_w, rwkv_ln_b=rwkv_ln_b,
        sc_conv_w=sc_conv_w, sc_norm=sc_norm, ssm_conv_w=ssm_conv_w, ssm_conv_b=ssm_conv_b,
        ssm_dt_bias=ssm_dt_bias, ssm_A_log=ssm_A_log, ssm_D=ssm_D, ssm_norm=ssm_norm, w_out=w_out,
        norm_ffn2=norm_ffn2, ffn2_w_in=ffn2_w_in, ffn2_w_out=ffn2_w_out)
    wp = _prepare_weights(weights)
    depth = w_in.shape[0]
    nb, dt_ = x_prompt.shape[0], x_prompt.dtype
    zero_states = (
        jnp.zeros((depth, nb, RWKV_PROJ), dt_),
        jnp.zeros((depth, nb, RWKV_HEADS, RWKV_HEAD, RWKV_HEAD), dt_),
        jnp.zeros((depth, nb, SC_WIDTH - 1, SC_DIM), dt_),
        jnp.zeros((depth, nb, SSM_CONV - 1, SSM_CONV_DIM), dt_),
        jnp.zeros((depth, nb, SSM_HEADS, SSM_HEADDIM, SSM_STATE), dt_),
    )
    y_prompt, p_states = _trunk(x_prompt, zero_states, wp, norm_final)
    sample_states = (state_rwkv_shift, state_rwkv_wkv, state_sc_buf, state_ssm_conv, state_ssm)
    y_sample, s_states = _trunk(x_sample, sample_states, wp, norm_final)
    return (y_prompt, y_sample) + tuple(p_states) + tuple(s_states)
```

```python
import functools

import jax
import jax.numpy as jnp
from jax import lax
from jax.experimental import pallas as pl
from jax.experimental.pallas import tpu as pltpu

F32 = jnp.float32
MXU_DTYPE = jnp.bfloat16
HIGHEST = lax.Precision.HIGHEST

D_MODEL = 1024
D_FF = 2816
RWKV_DIM = 1024
RWKV_HEAD = 64
RWKV_HEADS = 16
DECAY_LORA = 64
AAA_LORA = 64
GATE_LORA = 160
RWKV_PROJ = 3 * RWKV_DIM + DECAY_LORA + AAA_LORA + GATE_LORA
RWKV_GN_EPS = 64e-5
DECAY_SCALE = 0.6065306597126334
SC_DIM = 512
SC_WIDTH = 3
SSM_DIM = 512
SSM_HEADDIM = 64
SSM_HEADS = 8
SSM_GROUPS = 2
SSM_STATE = 128
SSM_CONV = 4
SSM_CONV_DIM = SSM_DIM + 2 * SSM_GROUPS * SSM_STATE
NORM_EPS = 1e-6

COL_RKV = 0
COL_SC = 3 * RWKV_DIM
COL_Z = COL_SC + 3 * SC_DIM
COL_XBC = COL_Z + SSM_DIM
COL_SMALL = COL_XBC + SSM_CONV_DIM
SMALL_W = 512
SM_GL = 128
SM_DT = 384
PROJ_COLS = COL_SMALL + SMALL_W

LANE = 128
SUBLANE = 8
VMEM_LIMIT = 56 * 1024 * 1024

RWKV_CHUNK = 64
RWKV_CHUNKS_PER_STEP = 4
SSD_CHUNK = 128


def _mm(a, b):
    return jnp.dot(a.astype(MXU_DTYPE), b.astype(MXU_DTYPE), preferred_element_type=F32)


def _mm_nt(a, b):
    return lax.dot_general(a.astype(MXU_DTYPE), b.astype(MXU_DTYPE), (((1,), (1,)), ((), ())),
                           preferred_element_type=F32)


def _mm_tn(a, b):
    return lax.dot_general(a.astype(MXU_DTYPE), b.astype(MXU_DTYPE), (((0,), (0,)), ((), ())),
                           preferred_element_type=F32)


def _mmx(a, b):
    return jnp.dot(a, b, precision=HIGHEST, preferred_element_type=F32)


def _mmx_nt(a, b):
    return lax.dot_general(a, b, (((1,), (1,)), ((), ())), precision=HIGHEST, preferred_element_type=F32)


def _mmx_tn(a, b):
    return lax.dot_general(a, b, (((0,), (0,)), ((), ())), precision=HIGHEST, preferred_element_type=F32)


def _sigmoid(x):
    return 1.0 / (1.0 + jnp.exp(-x))


def _silu(x):
    return x * _sigmoid(x)


def _softplus(x):
    return jnp.maximum(x, 0.0) + jnp.log1p(jnp.exp(-jnp.abs(x)))


def _rmsnorm(x, g):
    return x * lax.rsqrt(jnp.mean(x * x, axis=-1, keepdims=True) + NORM_EPS) * g


def _params(*sem):
    return pltpu.CompilerParams(dimension_semantics=sem, vmem_limit_bytes=VMEM_LIMIT)


def _ffn_kernel(x_ref, g_ref, wg_ref, wu_ref, wo_ref, gf_ref, o_ref, h_ref, acc_ref, *, final_norm):
    j = pl.program_id(1)

    @pl.when(j == 0)
    def _():
        h_ref[...] = _rmsnorm(x_ref[...], g_ref[...]).astype(h_ref.dtype)
        acc_ref[...] = jnp.zeros_like(acc_ref)

    h = h_ref[...]
    gate = jnp.dot(h, wg_ref[...], preferred_element_type=F32)
    up = jnp.dot(h, wu_ref[...], preferred_element_type=F32)
    act = (_silu(gate) * up).astype(MXU_DTYPE)
    acc_ref[...] += jnp.dot(act, wo_ref[...], preferred_element_type=F32)

    @pl.when(j == pl.num_programs(1) - 1)
    def _():
        y = x_ref[...] + 0.5 * acc_ref[...]
        if final_norm:
            y = _rmsnorm(y, gf_ref[...])
        o_ref[...] = y


def _ffn(x, norm_g, w_in, w_out, layer, final_g=None, *, tm, tf):
    m = x.shape[0]
    nf = D_FF // tf
    final_norm = final_g is not None
    gf = final_g if final_norm else norm_g[layer]
    return pl.pallas_call(
        functools.partial(_ffn_kernel, final_norm=final_norm),
        grid=(m // tm, nf),
        in_specs=[
            pl.BlockSpec((tm, D_MODEL), lambda i, j: (i, 0)),
            pl.BlockSpec((None, 1, D_MODEL), lambda i, j: (layer, 0, 0)),
            pl.BlockSpec((None, D_MODEL, tf), lambda i, j: (layer, 0, j)),
            pl.BlockSpec((None, D_MODEL, tf), lambda i, j: (layer, 0, nf + j)),
            pl.BlockSpec((None, tf, D_MODEL), lambda i, j: (layer, j, 0)),
            pl.BlockSpec((1, D_MODEL), lambda i, j: (0, 0)),
        ],
        out_specs=pl.BlockSpec((tm, D_MODEL), lambda i, j: (i, 0)),
        out_shape=jax.ShapeDtypeStruct((m, D_MODEL), F32),
        scratch_shapes=[pltpu.VMEM((tm, D_MODEL), MXU_DTYPE), pltpu.VMEM((tm, D_MODEL), F32)],
        compiler_params=_params("parallel", "arbitrary"),
        name="ffn",
    )(x, norm_g.reshape(-1, 1, D_MODEL), w_in, w_in, w_out, gf.reshape(1, D_MODEL))


def _proj_kernel(x_ref, g_ref, w_ref, o_ref, h_ref):
    @pl.when(pl.program_id(1) == 0)
    def _():
        h_ref[...] = _rmsnorm(x_ref[...], g_ref[...]).astype(h_ref.dtype)

    o_ref[...] = jnp.dot(h_ref[...], w_ref[...], preferred_element_type=F32)


def _proj_in(x, norm_g, w_all, layer, *, tm, tn):
    m = x.shape[0]
    return pl.pallas_call(
        _proj_kernel,
        grid=(m // tm, PROJ_COLS // tn),
        in_specs=[
            pl.BlockSpec((tm, D_MODEL), lambda i, j: (i, 0)),
            pl.BlockSpec((None, 1, D_MODEL), lambda i, j: (layer, 0, 0)),
            pl.BlockSpec((None, D_MODEL, tn), lambda i, j: (layer, 0, j)),
        ],
        out_specs=pl.BlockSpec((tm, tn), lambda i, j: (i, j)),
        out_shape=jax.ShapeDtypeStruct((m, PROJ_COLS), F32),
        scratch_shapes=[pltpu.VMEM((tm, D_MODEL), MXU_DTYPE)],
        compiler_params=_params("parallel", "arbitrary"),
        name="proj_in",
    )(x, norm_g.reshape(-1, 1, D_MODEL), w_all)


def _proj_out_kernel(x_ref, y1_ref, y2_ref, y3_ref, w1_ref, w2_ref, w3_ref, o_ref):
    acc = jnp.dot(y1_ref[...].astype(MXU_DTYPE), w1_ref[...], preferred_element_type=F32)
    acc += jnp.dot(y2_ref[...].astype(MXU_DTYPE), w2_ref[...], preferred_element_type=F32)
    acc += jnp.dot(y3_ref[...].astype(MXU_DTYPE), w3_ref[...], preferred_element_type=F32)
    o_ref[...] = x_ref[...] + acc


def _proj_out(x, y_rwkv, y_sc, y_ssm, w_out, layer, *, tm):
    m = x.shape[0]
    return pl.pallas_call(
        _proj_out_kernel,
        grid=(m // tm,),
        in_specs=[
            pl.BlockSpec((tm, D_MODEL), lambda i: (i, 0)),
            pl.BlockSpec((tm, RWKV_DIM), lambda i: (i, 0)),
            pl.BlockSpec((tm, SC_DIM), lambda i: (i, 0)),
            pl.BlockSpec((tm, SSM_DIM), lambda i: (i, 0)),
            pl.BlockSpec((None, RWKV_DIM, D_MODEL), lambda i: (layer, 0, 0)),
            pl.BlockSpec((None, SC_DIM, D_MODEL), lambda i: (layer, RWKV_DIM // SC_DIM, 0)),
            pl.BlockSpec((None, SSM_DIM, D_MODEL), lambda i: (layer, (RWKV_DIM + SC_DIM) // SSM_DIM, 0)),
        ],
        out_specs=pl.BlockSpec((tm, D_MODEL), lambda i: (i, 0)),
        out_shape=jax.ShapeDtypeStruct((m, D_MODEL), F32),
        compiler_params=_params("parallel"),
        name="proj_out",
    )(x, y_rwkv, y_sc, y_ssm, w_out, w_out, w_out)


def _previous_rows(x, carry_row):
    first = lax.broadcasted_iota(jnp.int32, x.shape, 0) == 0
    return jnp.where(first, carry_row, pltpu.roll(x, 1, 0))


def _rwkv_prepare(p_rkv, prev_rkv, p_sm, prev_sm, wts):
    (mu_rkv, mu_sm, w0, w2, a0, a2, g2, k_k, k_a) = wts
    xs = p_rkv + (prev_rkv - p_rkv) * mu_rkv
    xm = p_sm + (prev_sm - p_sm) * mu_sm
    r = xs[:, :RWKV_DIM]
    k = xs[:, RWKV_DIM:2 * RWKV_DIM]
    v = xs[:, 2 * RWKV_DIM:]
    w_lr = xm[:, :DECAY_LORA]
    a_lr = xm[:, DECAY_LORA:DECAY_LORA + AAA_LORA]
    g_lr = xm[:, SM_GL:SM_GL + GATE_LORA]
    log_decay = -DECAY_SCALE * _sigmoid(w0 + _mm(jnp.tanh(w_lr), w2))
    a = _sigmoid(a0 + _mm(a_lr, a2))
    g = _mm(_sigmoid(g_lr), g2)
    kk = k * k_k
    k_mod = k * (1.0 + (a - 1.0) * k_a)
    return r, log_decay, k_mod, v, kk, a, g


def _rwkv_weight_specs(layer):
    def vec(width):
        return pl.BlockSpec((None, 1, width), lambda *idx: (layer, 0, 0))

    def mat(rows):
        return pl.BlockSpec((None, rows, RWKV_DIM), lambda *idx: (layer, 0, 0))

    return [vec(3 * RWKV_DIM), vec(SMALL_W), vec(RWKV_DIM), mat(DECAY_LORA), vec(RWKV_DIM), mat(AAA_LORA),
            mat(GATE_LORA), vec(RWKV_DIM), vec(RWKV_DIM), vec(RWKV_DIM), vec(RWKV_DIM), vec(RWKV_DIM)]


def _rwkv_weight_args(wp):
    return (wp["mu_rkv"], wp["mu_sm"], wp["rwkv_w0"], wp["rwkv_w2"], wp["rwkv_a0"], wp["rwkv_a2"],
            wp["rwkv_g2"], wp["rwkv_k_k"], wp["rwkv_k_a"], wp["rwkv_r_k"], wp["rwkv_ln_w"], wp["rwkv_ln_b"])


HEADS_PER_GROUP = 4
GROUP_W = HEADS_PER_GROUP * RWKV_HEAD
RWKV_GROUPS = RWKV_HEADS // HEADS_PER_GROUP


def _split3(x):
    hi = x.astype(MXU_DTYPE)
    r1 = x - hi.astype(F32)
    mid = r1.astype(MXU_DTYPE)
    lo = (r1 - mid.astype(F32)).astype(MXU_DTYPE)
    return hi, mid, lo


def _block_diag(x, head_masks):
    return jnp.concatenate([x * m for m in head_masks], axis=0)


def _rwkv_chunk_kernel(p_rkv_ref, p_sm_ref, sh_rkv_ref, sh_sm_ref, s0_ref,
                       mu_rkv_ref, mu_sm_ref, w0_ref, w2_ref, a0_ref, a2_ref, g2_ref, kk_ref, ka_ref,
                       rk_ref, lnw_ref, lnb_ref,
                       y_ref, s_out_ref,
                       pad_rkv, pad_sm, state, at_s, rt_s, bt_s, kt_s, bh_s, kh_s, v_s, yn_s,
                       hm_s, ones_s, tri_s):
    c = RWKV_CHUNK
    nc = RWKV_CHUNKS_PER_STEP
    tl = nc * c
    n = RWKV_HEAD
    gw = GROUP_W
    t = pl.program_id(1)

    @pl.when(t == 0)
    def _():
        pad_rkv[...] = sh_rkv_ref[0]
        pad_sm[...] = sh_sm_ref[0]
        for h in range(RWKV_HEADS):
            state[:, h * n:(h + 1) * n] = s0_ref[0, h]
        lane_head = lax.broadcasted_iota(jnp.int32, (c, gw), 1) // n
        for hh in range(HEADS_PER_GROUP):
            hm_s[hh] = (lane_head == hh).astype(MXU_DTYPE)
        sq_r = lax.broadcasted_iota(jnp.int32, (gw, gw), 0) // n
        sq_c = lax.broadcasted_iota(jnp.int32, (gw, gw), 1) // n
        ones_s[...] = (sq_r == sq_c).astype(MXU_DTYPE)
        tri_r = lax.broadcasted_iota(jnp.int32, (tl, tl), 0)
        tri_c = lax.broadcasted_iota(jnp.int32, (tl, tl), 1)
        tri_s[...] = ((tri_r >= tri_c) & (tri_r // c == tri_c // c)).astype(MXU_DTYPE)

    wts = (mu_rkv_ref[...], mu_sm_ref[...], w0_ref[...], w2_ref[...], a0_ref[...], a2_ref[...],
           g2_ref[...], kk_ref[...], ka_ref[...])
    r, log_decay, k_mod, v, kk, a, g = _rwkv_prepare(
        p_rkv_ref[...], _previous_rows(p_rkv_ref[...], pad_rkv[...]),
        p_sm_ref[...], _previous_rows(p_sm_ref[...], pad_sm[...]), wts)
    pad_rkv[...] = p_rkv_ref[tl - 1:tl, :]
    pad_sm[...] = p_sm_ref[tl - 1:tl, :]

    row = lax.broadcasted_iota(jnp.int32, (c, gw), 0)
    lane = lax.broadcasted_iota(jnp.int32, (c, gw), 1)
    pos = lane % n
    strict = row > pos
    incl = row >= pos
    eye = (row == pos).astype(F32)
    head_masks = [hm_s[hh] for hh in range(HEADS_PER_GROUP)]
    ones_blk = ones_s[...]
    tri = tri_s[...]

    def head_sum(x):
        return jnp.concatenate(
            [jnp.dot(x[:, gi * gw:(gi + 1) * gw].astype(MXU_DTYPE), ones_blk, preferred_element_type=F32)
             for gi in range(RWKV_GROUPS)], axis=1)

    hi, mid, lo = _split3(log_decay)
    cum = (jnp.dot(tri, hi, preferred_element_type=F32) + jnp.dot(tri, mid, preferred_element_type=F32)
           + jnp.dot(tri, lo, preferred_element_type=F32))
    cum_end = jnp.concatenate([jnp.broadcast_to(cum[(ci + 1) * c - 1:(ci + 1) * c, :], (c, RWKV_DIM))
                               for ci in range(nc)], axis=0)
    e_out = jnp.exp(-cum)
    e_end = jnp.exp(cum_end - cum)
    kk = kk * lax.rsqrt(jnp.maximum(head_sum(kk * kk), 1e-24))
    b = kk * a
    at_s[...] = (-kk * jnp.exp(cum - log_decay)).astype(MXU_DTYPE)
    rt_s[...] = (r * jnp.exp(cum)).astype(MXU_DTYPE)
    bt_s[...] = (b * e_out).astype(MXU_DTYPE)
    kt_s[...] = (k_mod * e_out).astype(MXU_DTYPE)
    bh_s[...] = (b * e_end).astype(MXU_DTYPE)
    kh_s[...] = (k_mod * e_end).astype(MXU_DTYPE)
    v_s[...] = v
    bonus = head_sum(r * k_mod * rk_ref[...])
    w_end = jnp.exp(cum_end)

    def mm(x, y):
        return jnp.dot(x.astype(MXU_DTYPE), y, preferred_element_type=F32)

    def bd(x):
        return _block_diag(x.astype(MXU_DTYPE), head_masks)

    groups = range(RWKV_GROUPS)
    chunks = range(nc)
    pairs = [(ci, gi) for ci in chunks for gi in groups]
    rws = [slice(ci * c, (ci + 1) * c) for ci in chunks]
    sls = [slice(gi * gw, (gi + 1) * gw) for gi in groups]
    nt = (((1,), (1,)), ((), ()))
    ar = {(ci, gi): jnp.concatenate([at_s[rws[ci], sls[gi]], rt_s[rws[ci], sls[gi]]], axis=0)
          for ci, gi in pairs}
    vg = {(ci, gi): v_s[rws[ci], sls[gi]].astype(MXU_DTYPE) for ci, gi in pairs}
    gram_b = {(ci, gi): lax.dot_general(ar[ci, gi], bd(bt_s[rws[ci], sls[gi]]), nt, preferred_element_type=F32)
              for ci, gi in pairs}
    gram_k = {(ci, gi): lax.dot_general(ar[ci, gi], bd(kt_s[rws[ci], sls[gi]]), nt, preferred_element_type=F32)
              for ci, gi in pairs}
    l_ab = {p: jnp.where(strict, gram_b[p][:c], 0.0) for p in pairs}
    m_rb = {p: jnp.where(incl, gram_b[p][c:], 0.0) for p in pairs}
    l_akrk = {p: jnp.where(jnp.concatenate([strict, incl], axis=0), gram_k[p], 0.0) for p in pairs}
    inv = {p: eye + l_ab[p] for p in pairs}
    power = l_ab
    power_bd = {p: bd(power[p]) for p in pairs}
    span = 2
    while span < c:
        power = {p: mm(power[p], power_bd[p]) for p in pairs}
        power_bd = {p: bd(power[p]) for p in pairs}
        inv = {p: inv[p] + mm(inv[p], power_bd[p]) for p in pairs}
        span *= 2
    y = {}
    for ci in chunks:
        s0 = [state[:, sl] for sl in sls]
        z = [mm(l_akrk[ci, gi], bd(vg[ci, gi]))
             + lax.dot_general(ar[ci, gi], bd(s0[gi]), nt, preferred_element_type=F32) for gi in groups]
        u = [mm(inv[ci, gi], bd(z[gi][:c])) for gi in groups]
        for gi in groups:
            y[ci, gi] = z[gi][c:] + mm(m_rb[ci, gi], bd(u[gi]))
        for gi in groups:
            uv = jnp.concatenate([u[gi].astype(MXU_DTYPE), vg[ci, gi]], axis=0)
            bk = jnp.concatenate([bh_s[rws[ci], sls[gi]], kh_s[rws[ci], sls[gi]]], axis=0)
            upd = lax.dot_general(uv, bk, (((0,), (0,)), ((), ())), preferred_element_type=F32)
            s_new = s0[gi] * w_end[ci * c:ci * c + 1, sls[gi]]
            for hh in range(HEADS_PER_GROUP):
                s_new += upd[hh * n:(hh + 1) * n, :] * head_masks[hh].astype(F32)
            state[:, sls[gi]] = s_new
    mean = {p: mm(y[p], ones_blk) * (1.0 / n) for p in pairs}
    yc = {p: y[p] - mean[p] for p in pairs}
    var = {p: mm(yc[p] * yc[p], ones_blk) * (1.0 / n) for p in pairs}
    for ci, gi in pairs:
        yn_s[rws[ci], sls[gi]] = yc[ci, gi] * lax.rsqrt(var[ci, gi] + RWKV_GN_EPS)

    y_ref[...] = (yn_s[...] * lnw_ref[...] + lnb_ref[...] + bonus * v_s[...]) * g

    @pl.when(t == pl.num_programs(1) - 1)
    def _():
        for h in range(RWKV_HEADS):
            s_out_ref[0, h] = state[:, h * n:(h + 1) * n]


def _rwkv_prompt(proj, shift_rkv, shift_sm, wkv0, wp, layer, *, batch, seqlen):
    c = RWKV_CHUNK * RWKV_CHUNKS_PER_STEP
    nt = seqlen // c
    return pl.pallas_call(
        _rwkv_chunk_kernel,
        grid=(batch, nt),
        in_specs=[
            pl.BlockSpec((c, 3 * RWKV_DIM), lambda b, t: (b * nt + t, COL_RKV // (3 * RWKV_DIM))),
            pl.BlockSpec((c, SMALL_W), lambda b, t: (b * nt + t, COL_SMALL // SMALL_W)),
            pl.BlockSpec((1, 1, 3 * RWKV_DIM), lambda b, t: (b, 0, 0)),
            pl.BlockSpec((1, 1, SMALL_W), lambda b, t: (b, 0, 0)),
            pl.BlockSpec((1, RWKV_HEADS, RWKV_HEAD, RWKV_HEAD), lambda b, t: (b, 0, 0, 0)),
        ] + _rwkv_weight_specs(layer),
        out_specs=[
            pl.BlockSpec((c, RWKV_DIM), lambda b, t: (b * nt + t, 0)),
            pl.BlockSpec((1, RWKV_HEADS, RWKV_HEAD, RWKV_HEAD), lambda b, t: (b, 0, 0, 0)),
        ],
        out_shape=[
            jax.ShapeDtypeStruct((batch * seqlen, RWKV_DIM), F32),
            jax.ShapeDtypeStruct((batch, RWKV_HEADS, RWKV_HEAD, RWKV_HEAD), F32),
        ],
        scratch_shapes=[
            pltpu.VMEM((1, 3 * RWKV_DIM), F32),
            pltpu.VMEM((1, SMALL_W), F32),
            pltpu.VMEM((RWKV_HEAD, RWKV_DIM), F32),
        ] + [pltpu.VMEM((c, RWKV_DIM), MXU_DTYPE) for _ in range(6)]
          + [pltpu.VMEM((c, RWKV_DIM), F32) for _ in range(2)]
          + [pltpu.VMEM((HEADS_PER_GROUP, RWKV_CHUNK, GROUP_W), MXU_DTYPE),
             pltpu.VMEM((GROUP_W, GROUP_W), MXU_DTYPE), pltpu.VMEM((c, c), MXU_DTYPE)],
        compiler_params=_params("parallel", "arbitrary"),
        name="rwkv_chunk",
    )(proj, proj, shift_rkv, shift_sm, wkv0, *_rwkv_weight_args(wp))


def _sc_kernel(p_ref, buf_ref, w_ref, g_ref, y_ref, buf_out_ref, pad, *, tl):
    t = pl.program_id(1)
    hist = SC_WIDTH - 1

    @pl.when(t == 0)
    def _():
        pad[SUBLANE - hist:SUBLANE, :] = buf_ref[0]

    b_gate = p_ref[:, :SC_DIM]
    u = p_ref[:, SC_DIM:2 * SC_DIM] * p_ref[:, 2 * SC_DIM:]
    pad[SUBLANE:SUBLANE + tl, :] = u
    conv = u * w_ref[hist:hist + 1, :]
    for j in range(hist):
        conv += pad[SUBLANE - hist + j:SUBLANE - hist + j + tl, :] * w_ref[j:j + 1, :]
    y_ref[...] = _rmsnorm(b_gate * conv, g_ref[...])
    last = pad[tl + SUBLANE - hist:tl + SUBLANE, :]
    pad[SUBLANE - hist:SUBLANE, :] = last

    @pl.when(t == pl.num_programs(1) - 1)
    def _():
        buf_out_ref[0] = last


def _sc_prompt(proj, buf0, wp, layer, *, batch, seqlen, tl):
    nt = seqlen // tl
    hist = SC_WIDTH - 1
    return pl.pallas_call(
        functools.partial(_sc_kernel, tl=tl),
        grid=(batch, nt),
        in_specs=[
            pl.BlockSpec((tl, 3 * SC_DIM), lambda b, t: (b * nt + t, COL_SC // (3 * SC_DIM))),
            pl.BlockSpec((1, hist, SC_DIM), lambda b, t: (b, 0, 0)),
            pl.BlockSpec((None, SC_WIDTH, SC_DIM), lambda b, t: (layer, 0, 0)),
            pl.BlockSpec((None, 1, SC_DIM), lambda b, t: (layer, 0, 0)),
        ],
        out_specs=[
            pl.BlockSpec((tl, SC_DIM), lambda b, t: (b * nt + t, 0)),
            pl.BlockSpec((1, hist, SC_DIM), lambda b, t: (b, 0, 0)),
        ],
        out_shape=[
            jax.ShapeDtypeStruct((batch * seqlen, SC_DIM), F32),
            jax.ShapeDtypeStruct((batch, hist, SC_DIM), F32),
        ],
        scratch_shapes=[pltpu.VMEM((SUBLANE + tl, SC_DIM), F32)],
        compiler_params=_params("parallel", "arbitrary"),
        name="short_conv",
    )(proj, buf0, wp["sc_conv_w"], wp["sc_norm"])


def _ssd_kernel(z_ref, xbc_ref, dt_ref, buf_ref, h0_ref, cw_ref, cb_ref, dtb_ref, alog_ref, dskip_ref, g_ref,
                y_ref, buf_out_ref, h_out_ref, pad, state, y_s):
    q = SSD_CHUNK
    t = pl.program_id(1)
    hist = SSM_CONV - 1
    gw = SSM_DIM // SSM_GROUPS

    @pl.when(t == 0)
    def _():
        pad[SUBLANE - hist:SUBLANE, :] = buf_ref[0]
        state[...] = h0_ref[0]

    xbc = xbc_ref[...]
    pad[SUBLANE:SUBLANE + q, :] = xbc
    conv = xbc * cw_ref[hist:hist + 1, :]
    for j in range(hist):
        conv += pad[SUBLANE - hist + j:SUBLANE - hist + j + q, :] * cw_ref[j:j + 1, :]
    last = pad[q + SUBLANE - hist:q + SUBLANE, :]
    pad[SUBLANE - hist:SUBLANE, :] = last
    xc = _silu(conv + cb_ref[...])
    xh = xc[:, :SSM_DIM]

    dt = _softplus(dt_ref[...] + dtb_ref[...])
    neg_a = -jnp.exp(alog_ref[...])
    row = lax.broadcasted_iota(jnp.int32, (q, q), 0)
    col = lax.broadcasted_iota(jnp.int32, (q, q), 1)
    causal = row >= col
    cum = _mmx(causal.astype(F32), dt * neg_a)
    cum_t = cum.T
    cum_end = cum[q - 1:q, :]
    heads_per_group = SSM_HEADS // SSM_GROUPS

    heads = range(SSM_HEADS)
    grp_of = [h // heads_per_group for h in heads]
    sls = [slice(h * SSM_HEADDIM, (h + 1) * SSM_HEADDIM) for h in heads]
    bm = [xc[:, SSM_DIM + g * SSM_STATE:SSM_DIM + (g + 1) * SSM_STATE].astype(MXU_DTYPE) for g in range(SSM_GROUPS)]
    cm = [xc[:, SSM_DIM + (SSM_GROUPS + g) * SSM_STATE:SSM_DIM + (SSM_GROUPS + g + 1) * SSM_STATE]
          .astype(MXU_DTYPE) for g in range(SSM_GROUPS)]
    gram = [_mm_nt(cm[g], bm[g]) for g in range(SSM_GROUPS)]
    cum_col = [cum[:, h:h + 1] for h in heads]
    seg = [cum_col[h] - cum_t[h:h + 1, :] for h in heads]
    decay = [jnp.where(causal, jnp.exp(jnp.where(causal, x, 0.0)), 0.0) for x in seg]
    xdt = [xh[:, sls[h]] * dt[:, h:h + 1] for h in heads]
    y = [_mm(gram[grp_of[h]] * decay[h], xdt[h]) for h in heads]
    y_off = [_mm_nt(cm[grp_of[h]], state[h]) * jnp.exp(cum_col[h]) for h in heads]
    upd = [_mm_tn(xdt[h] * jnp.exp(cum_end[:, h:h + 1] - cum_col[h]), bm[grp_of[h]]) for h in heads]
    for h in heads:
        state[h] = state[h] * jnp.exp(cum_end[:, h:h + 1]) + upd[h]
        y_s[:, sls[h]] = y[h] + y_off[h] + dskip_ref[:, h:h + 1] * xh[:, sls[h]]

    y = y_s[...] * _silu(z_ref[...])
    for grp in range(SSM_GROUPS):
        yg = y[:, grp * gw:(grp + 1) * gw]
        y_ref[:, grp * gw:(grp + 1) * gw] = _rmsnorm(yg, g_ref[:, grp * gw:(grp + 1) * gw])

    @pl.when(t == pl.num_programs(1) - 1)
    def _():
        buf_out_ref[0] = last
        h_out_ref[0] = state[...]


def _ssd_prompt(proj, buf0, h0, wp, layer, *, batch, seqlen):
    q = SSD_CHUNK
    nt = seqlen // q
    hist = SSM_CONV - 1
    lane_vec = pl.BlockSpec((None, 1, LANE), lambda b, t: (layer, 0, 0))
    return pl.pallas_call(
        _ssd_kernel,
        grid=(batch, nt),
        in_specs=[
            pl.BlockSpec((q, SSM_DIM), lambda b, t: (b * nt + t, COL_Z // SSM_DIM)),
            pl.BlockSpec((q, SSM_CONV_DIM), lambda b, t: (b * nt + t, COL_XBC // SSM_CONV_DIM)),
            pl.BlockSpec((q, LANE), lambda b, t: (b * nt + t, (COL_SMALL + SM_DT) // LANE)),
            pl.BlockSpec((1, hist, SSM_CONV_DIM), lambda b, t: (b, 0, 0)),
            pl.BlockSpec((1, SSM_HEADS, SSM_HEADDIM, SSM_STATE), lambda b, t: (b, 0, 0, 0)),
            pl.BlockSpec((None, SSM_CONV, SSM_CONV_DIM), lambda b, t: (layer, 0, 0)),
            pl.BlockSpec((None, 1, SSM_CONV_DIM), lambda b, t: (layer, 0, 0)),
            lane_vec, lane_vec, lane_vec,
            pl.BlockSpec((None, 1, SSM_DIM), lambda b, t: (layer, 0, 0)),
        ],
        out_specs=[
            pl.BlockSpec((q, SSM_DIM), lambda b, t: (b * nt + t, 0)),
            pl.BlockSpec((1, hist, SSM_CONV_DIM), lambda b, t: (b, 0, 0)),
            pl.BlockSpec((1, SSM_HEADS, SSM_HEADDIM, SSM_STATE), lambda b, t: (b, 0, 0, 0)),
        ],
        out_shape=[
            jax.ShapeDtypeStruct((batch * seqlen, SSM_DIM), F32),
            jax.ShapeDtypeStruct((batch, hist, SSM_CONV_DIM), F32),
            jax.ShapeDtypeStruct((batch, SSM_HEADS, SSM_HEADDIM, SSM_STATE), F32),
        ],
        scratch_shapes=[
            pltpu.VMEM((SUBLANE + q, SSM_CONV_DIM), F32),
            pltpu.VMEM((SSM_HEADS, SSM_HEADDIM, SSM_STATE), F32),
            pltpu.VMEM((q, SSM_DIM), F32),
        ],
        compiler_params=_params("parallel", "arbitrary"),
        name="ssd_chunk",
    )(proj, proj, proj, buf0, h0, wp["ssm_conv_w"], wp["ssm_conv_b"], wp["dt_bias_pad"], wp["a_log_pad"],
      wp["d_pad"], wp["ssm_norm"])


def _lane_sums(x, ones):
    hi = x.astype(MXU_DTYPE)
    mid = (x - hi.astype(F32)).astype(MXU_DTYPE)
    return jnp.dot(hi, ones, preferred_element_type=F32) + jnp.dot(mid, ones, preferred_element_type=F32)


def _row_sums(sel, x):
    hi = x.astype(MXU_DTYPE)
    mid = (x - hi.astype(F32)).astype(MXU_DTYPE)
    return jnp.dot(sel, hi, preferred_element_type=F32) + jnp.dot(sel, mid, preferred_element_type=F32)


def _head_sums_rows(x, ones_pair):
    return jnp.concatenate([_lane_sums(x[:, q * LANE:(q + 1) * LANE], ones_pair)
                            for q in range(RWKV_DIM // LANE)], axis=1)


def _step_kernel(p_rkv_ref, p_sm_ref, p_sc_ref, z_ref, xbc_ref,
                 sh_rkv_ref, sh_sm_ref, wkv_ref, scbuf_ref, ssmbuf_ref, ssm_ref,
                 mu_rkv_ref, mu_sm_ref, w0_ref, w2_ref, a0_ref, a2_ref, g2_ref, kk_ref, ka_ref,
                 rk_ref, lnw_ref, lnb_ref,
                 scw_ref, scg_ref, cw_ref, cb_ref, dtb_ref, alog_ref, dskip_ref, ssmg_ref,
                 y_rwkv_ref, y_sc_ref, y_ssm_ref, wkv_out_ref, scbuf_out_ref, ssmbuf_out_ref, ssm_out_ref,
                 vec8, even8, odd8):
    n = RWKV_HEAD
    half = n // 2
    i128 = lambda shape, d: lax.broadcasted_iota(jnp.int32, shape, d)
    ones_pair = (i128((LANE, LANE), 0) // n == i128((LANE, LANE), 1) // n).astype(MXU_DTYPE)
    ones_full = jnp.ones((LANE, LANE), MXU_DTYPE)
    diag2 = i128((half, LANE), 1) % n == 2 * i128((half, LANE), 0) + i128((half, LANE), 1) // n
    diag_lo = i128((n, LANE), 1) == i128((n, LANE), 0)
    diag_hi = i128((n, LANE), 1) == i128((n, LANE), 0) + n
    upper_row = (i128((SUBLANE, RWKV_DIM), 1) // n) % 2 == 1
    upper_lane = i128((1, LANE), 1) // n == 1

    wts = (mu_rkv_ref[...], mu_sm_ref[...], w0_ref[...], w2_ref[...], a0_ref[...], a2_ref[...],
           g2_ref[...], kk_ref[...], ka_ref[...])
    r, log_decay, k_mod, v, kk, a, g = _rwkv_prepare(p_rkv_ref[0], sh_rkv_ref[0], p_sm_ref[0], sh_sm_ref[0], wts)
    vec8[...] = jnp.zeros_like(vec8)
    vec8[0:1, :] = kk * kk
    vec8[1:2, :] = r * k_mod * rk_ref[...]
    sums = _head_sums_rows(vec8[...], ones_pair)
    kk = kk * lax.rsqrt(jnp.maximum(sums[0:1], 1e-24))
    bonus = sums[1:2]
    vec8[0:1, :] = -kk
    vec8[1:2, :] = jnp.exp(log_decay)
    vec8[2:3, :] = kk * a
    vec8[3:4, :] = k_mod
    vec8[4:5, :] = r
    vec8[5:6, :] = v
    rows = vec8[...]
    even8[...] = jnp.where(upper_row, pltpu.roll(rows, n, 1), rows)
    odd8[...] = jnp.where(upper_row, rows, pltpu.roll(rows, RWKV_DIM - n, 1))

    def head_vec(h, i):
        src = odd8 if h % 2 else even8
        return src[i:i + 1, (h // 2) * LANE:(h // 2 + 1) * LANE]

    heads = range(RWKV_HEADS)
    rs = lambda x, h: x[h * half:(h + 1) * half]
    sa = _lane_sums(jnp.concatenate([wkv_ref[0, h] * head_vec(h, 0) for h in heads], axis=0), ones_pair)
    v_col = _lane_sums(jnp.concatenate([jnp.where(diag2, head_vec(h, 5), 0.0) for h in heads], axis=0), ones_pair)
    for h in heads:
        wkv_out_ref[0, h] = (wkv_ref[0, h] * head_vec(h, 1) + rs(sa, h) * head_vec(h, 2)
                             + rs(v_col, h) * head_vec(h, 3))
    y_b = _lane_sums(jnp.concatenate([wkv_out_ref[0, h] * head_vec(h, 4) for h in heads], axis=0), ones_pair)
    y_d = jnp.concatenate([jnp.where(diag2, rs(y_b, h), 0.0) for h in heads], axis=0)
    sel = (i128((RWKV_HEADS, RWKV_HEADS * half), 1) // half
           == i128((RWKV_HEADS, RWKV_HEADS * half), 0)).astype(MXU_DTYPE)
    y_h = _row_sums(sel, y_d)
    y_h = y_h + pltpu.roll(y_h, n, 1)
    y = jnp.concatenate([jnp.where(upper_lane, y_h[2 * q + 1:2 * q + 2], y_h[2 * q:2 * q + 1])
                         for q in range(RWKV_HEADS // 2)], axis=1)
    vec8[0:1, :] = y
    mean = _head_sums_rows(vec8[...], ones_pair)[0:1] * (1.0 / n)
    yc = y - mean
    vec8[0:1, :] = yc * yc
    var = _head_sums_rows(vec8[...], ones_pair)[0:1] * (1.0 / n)
    y_rwkv_ref[0] = (yc * lax.rsqrt(var + RWKV_GN_EPS) * lnw_ref[...] + lnb_ref[...] + bonus * v) * g

    p_sc = p_sc_ref[0]
    u = p_sc[:, SC_DIM:2 * SC_DIM] * p_sc[:, 2 * SC_DIM:]
    buf = scbuf_ref[0]
    conv = buf[0:1] * scw_ref[0:1, :] + buf[1:2] * scw_ref[1:2, :] + u * scw_ref[2:3, :]
    y_sc_ref[0] = _rmsnorm(p_sc[:, :SC_DIM] * conv, scg_ref[...])
    scbuf_out_ref[0, 0:1] = buf[1:2]
    scbuf_out_ref[0, 1:2] = u

    xbc = xbc_ref[0]
    cbuf = ssmbuf_ref[0]
    conv = xbc * cw_ref[3:4, :]
    for j in range(SSM_CONV - 1):
        conv += cbuf[j:j + 1] * cw_ref[j:j + 1, :]
    ssmbuf_out_ref[0, 0:2] = cbuf[1:3]
    ssmbuf_out_ref[0, 2:3] = xbc
    xc = _silu(conv + cb_ref[...])
    xh = xc[:, :SSM_DIM]
    dt = _softplus(p_sm_ref[0][:, SM_DT:SM_DT + LANE] + dtb_ref[...])
    decay_all = jnp.exp(dt * (-jnp.exp(alog_ref[...])))
    head_of_lane = i128((1, SSM_DIM), 1) // SSM_HEADDIM
    dt_exp = jnp.zeros((1, SSM_DIM), F32)
    d_exp = jnp.zeros((1, SSM_DIM), F32)
    for h in range(SSM_HEADS):
        dt_exp = jnp.where(head_of_lane == h, dt[:, h:h + 1], dt_exp)
        d_exp = jnp.where(head_of_lane == h, dskip_ref[:, h:h + 1], d_exp)
    xdt = xh * dt_exp
    heads = range(SSM_HEADS)
    heads_per_group = SSM_HEADS // SSM_GROUPS
    bms = [xc[:, SSM_DIM + g * SSM_STATE:SSM_DIM + (g + 1) * SSM_STATE] for g in range(SSM_GROUPS)]
    cms = [xc[:, SSM_DIM + (SSM_GROUPS + g) * SSM_STATE:SSM_DIM + (SSM_GROUPS + g + 1) * SSM_STATE]
           for g in range(SSM_GROUPS)]
    diag = lambda h: diag_hi if h % 2 else diag_lo
    slab = lambda x, h: x[:, (h // 2) * LANE:(h // 2 + 1) * LANE]
    rs = lambda x, h: x[h * SSM_HEADDIM:(h + 1) * SSM_HEADDIM]
    xdt_col = _lane_sums(jnp.concatenate([jnp.where(diag(h), slab(xdt, h), 0.0) for h in heads], axis=0), ones_full)
    for h in heads:
        ssm_out_ref[0, h] = ssm_ref[0, h] * decay_all[:, h:h + 1] + rs(xdt_col, h) * bms[h // heads_per_group]
    y_b = _lane_sums(jnp.concatenate([ssm_out_ref[0, h] * cms[h // heads_per_group] for h in heads], axis=0),
                     ones_full)
    y_d = jnp.concatenate([jnp.where(diag(h), rs(y_b, h), 0.0) for h in heads], axis=0)
    sel = (i128((SSM_HEADS, SSM_HEADS * SSM_HEADDIM), 1) // SSM_HEADDIM
           == i128((SSM_HEADS, SSM_HEADS * SSM_HEADDIM), 0)).astype(MXU_DTYPE)
    y_h = _row_sums(sel, y_d)
    y = jnp.concatenate([y_h[2 * q:2 * q + 1] + y_h[2 * q + 1:2 * q + 2] for q in range(SSM_HEADS // 2)], axis=1)
    y = (y + d_exp * xh) * _silu(z_ref[0])
    gw = SSM_DIM // SSM_GROUPS
    for grp in range(SSM_GROUPS):
        yg = y[:, grp * gw:(grp + 1) * gw]
        y_ssm_ref[0, :, grp * gw:(grp + 1) * gw] = _rmsnorm(yg, ssmg_ref[:, grp * gw:(grp + 1) * gw])


def _mixers_step(proj, states, wp, layer, *, batch):
    shift_rkv, shift_sm, wkv, sc_buf, ssm_buf, ssm = states
    p3 = proj.reshape(batch, 1, PROJ_COLS)
    wkv2 = wkv.reshape(batch, RWKV_HEADS, RWKV_HEAD // 2, LANE)
    row = lambda width, blk: pl.BlockSpec((1, 1, width), lambda b: (b, 0, blk))
    st3 = lambda d1, d2: pl.BlockSpec((1, d1, d2), lambda b: (b, 0, 0))
    st4 = lambda d1, d2, d3: pl.BlockSpec((1, d1, d2, d3), lambda b: (b, 0, 0, 0))
    vec = lambda width: pl.BlockSpec((None, 1, width), lambda b: (layer, 0, 0))
    mat = lambda rows, width: pl.BlockSpec((None, rows, width), lambda b: (layer, 0, 0))
    one = lambda width: pltpu.VMEM((1, width), F32)
    outs = pl.pallas_call(
        _step_kernel,
        grid=(batch,),
        in_specs=[
            row(3 * RWKV_DIM, COL_RKV // (3 * RWKV_DIM)),
            row(SMALL_W, COL_SMALL // SMALL_W),
            row(3 * SC_DIM, COL_SC // (3 * SC_DIM)),
            row(SSM_DIM, COL_Z // SSM_DIM),
            row(SSM_CONV_DIM, COL_XBC // SSM_CONV_DIM),
            st3(1, 3 * RWKV_DIM), st3(1, SMALL_W),
            st4(RWKV_HEADS, RWKV_HEAD // 2, LANE),
            st3(SC_WIDTH - 1, SC_DIM), st3(SSM_CONV - 1, SSM_CONV_DIM),
            st4(SSM_HEADS, SSM_HEADDIM, SSM_STATE),
        ] + [vec(3 * RWKV_DIM), vec(SMALL_W), vec(RWKV_DIM), mat(DECAY_LORA, RWKV_DIM), vec(RWKV_DIM),
             mat(AAA_LORA, RWKV_DIM), mat(GATE_LORA, RWKV_DIM), vec(RWKV_DIM), vec(RWKV_DIM), vec(RWKV_DIM),
             vec(RWKV_DIM), vec(RWKV_DIM),
             mat(SC_WIDTH, SC_DIM), vec(SC_DIM), mat(SSM_CONV, SSM_CONV_DIM), vec(SSM_CONV_DIM),
             vec(LANE), vec(LANE), vec(LANE), vec(SSM_DIM)],
        out_specs=[
            st3(1, RWKV_DIM), st3(1, SC_DIM), st3(1, SSM_DIM),
            st4(RWKV_HEADS, RWKV_HEAD // 2, LANE),
            st3(SC_WIDTH - 1, SC_DIM), st3(SSM_CONV - 1, SSM_CONV_DIM),
            st4(SSM_HEADS, SSM_HEADDIM, SSM_STATE),
        ],
        out_shape=[
            jax.ShapeDtypeStruct((batch, 1, RWKV_DIM), F32),
            jax.ShapeDtypeStruct((batch, 1, SC_DIM), F32),
            jax.ShapeDtypeStruct((batch, 1, SSM_DIM), F32),
            jax.ShapeDtypeStruct(wkv2.shape, F32),
            jax.ShapeDtypeStruct(sc_buf.shape, F32),
            jax.ShapeDtypeStruct(ssm_buf.shape, F32),
            jax.ShapeDtypeStruct(ssm.shape, F32),
        ],
        scratch_shapes=[pltpu.VMEM((SUBLANE, RWKV_DIM), F32) for _ in range(3)],
        compiler_params=_params("parallel"),
        name="mixers_step",
    )(p3, p3, p3, p3, p3, shift_rkv, shift_sm, wkv2, sc_buf, ssm_buf, ssm,
      *_rwkv_weight_args(wp), wp["sc_conv_w"], wp["sc_norm"], wp["ssm_conv_w"], wp["ssm_conv_b"],
      wp["dt_bias_pad"], wp["a_log_pad"], wp["d_pad"], wp["ssm_norm"])
    y_rwkv, y_sc, y_ssm, wkv_n, sc_n, ssmbuf_n, ssm_n = outs
    return (y_rwkv.reshape(batch, RWKV_DIM), y_sc.reshape(batch, SC_DIM), y_ssm.reshape(batch, SSM_DIM),
            wkv_n.reshape(wkv.shape), sc_n, ssmbuf_n, ssm_n)


def _reorder_cols(a):
    lead = a.shape[:-1]
    rkv = a[..., :3 * RWKV_DIM]
    lora = a[..., 3 * RWKV_DIM:RWKV_PROJ]
    small = jnp.concatenate([lora, jnp.zeros(lead + (SMALL_W - lora.shape[-1],), a.dtype)], axis=-1)
    return rkv, small


def _prepare_weights(w):
    depth = w["w_in"].shape[0]
    wp = {}
    w_in = w["w_in"]
    rkv, small = _reorder_cols(w_in[..., :RWKV_PROJ])
    o_sc = RWKV_PROJ
    o_ssm = RWKV_PROJ + 3 * SC_DIM
    dt_cols = w_in[..., o_ssm + SSM_DIM + SSM_CONV_DIM:]
    small = small.at[..., SM_DT:SM_DT + SSM_HEADS].set(dt_cols)
    wp["w_all"] = jnp.concatenate(
        [rkv, w_in[..., o_sc:o_ssm], w_in[..., o_ssm:o_ssm + SSM_DIM + SSM_CONV_DIM], small],
        axis=-1).astype(MXU_DTYPE)
    mu_rkv, mu_sm = _reorder_cols(w["rwkv_mu"])
    wp["mu_rkv"] = mu_rkv.reshape(depth, 1, -1)
    wp["mu_sm"] = mu_sm.reshape(depth, 1, -1)
    for name in ("rwkv_w0", "rwkv_a0", "rwkv_k_k", "rwkv_k_a", "rwkv_ln_w", "rwkv_ln_b", "sc_norm",
                 "ssm_conv_b", "ssm_norm"):
        wp[name] = w[name].reshape(depth, 1, -1)
    wp["rwkv_r_k"] = w["rwkv_r_k"].reshape(depth, 1, RWKV_DIM)
    for name in ("rwkv_w2", "rwkv_a2", "rwkv_g2"):
        wp[name] = w[name].astype(MXU_DTYPE)
    wp["sc_conv_w"] = w["sc_conv_w"]
    wp["ssm_conv_w"] = w["ssm_conv_w"]
    pad_lane = lambda a: jnp.pad(a, ((0, 0), (0, LANE - a.shape[-1]))).reshape(depth, 1, LANE)
    wp["dt_bias_pad"] = pad_lane(w["ssm_dt_bias"])
    wp["a_log_pad"] = pad_lane(w["ssm_A_log"])
    wp["d_pad"] = pad_lane(w["ssm_D"])
    for name in ("ffn1_w_in", "ffn1_w_out", "ffn2_w_in", "ffn2_w_out", "w_out"):
        wp[name] = w[name].astype(MXU_DTYPE)
    for name in ("norm_ffn1", "norm_mix", "norm_ffn2"):
        wp[name] = w[name]
    return wp


def _shift_state_to_cols(proj_last):
    return jnp.concatenate([proj_last[..., :3 * RWKV_DIM],
                            proj_last[..., COL_SMALL:COL_SMALL + RWKV_PROJ - 3 * RWKV_DIM]], axis=-1)


def _tile(m, pref):
    t = min(m, pref)
    while m % t:
        t //= 2
    return t


def _trunk(x3, states, wp, norm_final):
    batch, seqlen, _ = x3.shape
    m = batch * seqlen
    x = x3.reshape(m, D_MODEL)
    depth = wp["w_all"].shape[0]
    tm = _tile(m, 512)
    new = ([], [], [], [], [])
    for layer in range(depth):
        shift, wkv, sc_buf, ssm_buf, ssm = (s[layer] for s in states)
        shift_rkv, shift_sm = _reorder_cols(shift)
        shift_rkv = shift_rkv.reshape(batch, 1, -1)
        shift_sm = shift_sm.reshape(batch, 1, -1)
        x = _ffn(x, wp["norm_ffn1"], wp["ffn1_w_in"], wp["ffn1_w_out"], layer, tm=tm, tf=1408)
        proj = _proj_in(x, wp["norm_mix"], wp["w_all"], layer, tm=_tile(m, 1024), tn=PROJ_COLS // 4)
        if seqlen == 1:
            y_rwkv, y_sc, y_ssm, wkv_n, sc_n, ssmbuf_n, ssm_n = _mixers_step(
                proj, (shift_rkv, shift_sm, wkv, sc_buf, ssm_buf, ssm), wp, layer, batch=batch)
        else:
            y_rwkv, wkv_n = _rwkv_prompt(proj, shift_rkv, shift_sm, wkv, wp, layer, batch=batch, seqlen=seqlen)
            y_sc, sc_n = _sc_prompt(proj, sc_buf, wp, layer, batch=batch, seqlen=seqlen, tl=_tile(seqlen, 256))
            y_ssm, ssmbuf_n, ssm_n = _ssd_prompt(proj, ssm_buf, ssm, wp, layer, batch=batch, seqlen=seqlen)
        shift_n = _shift_state_to_cols(proj.reshape(batch, seqlen, PROJ_COLS)[:, -1])
        x = _proj_out(x, y_rwkv, y_sc, y_ssm, wp["w_out"], layer, tm=tm)
        x = _ffn(x, wp["norm_ffn2"], wp["ffn2_w_in"], wp["ffn2_w_out"], layer,
                 final_g=norm_final if layer == depth - 1 else None, tm=tm, tf=1408)
        for lst, s in zip(new, (shift_n, wkv_n, sc_n, ssmbuf_n, ssm_n)):
            lst.append(s)
    return x.reshape(batch, seqlen, D_MODEL), tuple(jnp.stack(lst) for lst in new)


def kernel(x_prompt, x_sample, state_rwkv_shift, state_rwkv_wkv, state_sc_buf, state_ssm_conv, state_ssm,
           norm_ffn1, ffn1_w_in, ffn1_w_out, norm_mix, w_in, rwkv_mu, rwkv_w0, rwkv_w2, rwkv_a0, rwkv_a2,
           rwkv_g2, rwkv_k_k, rwkv_k_a, rwkv_r_k, rwkv_ln_w, rwkv_ln_b, sc_conv_w, sc_norm, ssm_conv_w,
           ssm_conv_b, ssm_dt_bias, ssm_A_log, ssm_D, ssm_norm, w_out, norm_ffn2, ffn2_w_in, ffn2_w_out,
           norm_final):
    weights = dict(
        norm_ffn1=norm_ffn1, ffn1_w_in=ffn1_w_in, ffn1_w_out=ffn1_w_out, norm_mix=norm_mix, w_in=w_in,
        rwkv_mu=rwkv_mu, rwkv_w0=rwkv_w0, rwkv_w2=rwkv_w2, rwkv_a0=rwkv_a0, rwkv_a2=rwkv_a2, rwkv_g2=rwkv_g2,
        rwkv_k_k=rwkv_k_k, rwkv_k_a=rwkv_k_a, rwkv_r_k=rwkv_r_k, rwkv_ln_w=rwkv_ln_w, rwkv_ln_b=rwkv_ln_b,
        sc_conv_w=sc_conv_w, sc_norm=sc_norm, ssm_conv_w=ssm_conv_w, ssm_conv_b=ssm_conv_b,
        ssm_dt_bias=ssm_dt_bias, ssm_A_log=ssm_A_log, ssm_D=ssm_D, ssm_norm=ssm_norm, w_out=w_out,
        norm_ffn2=norm_ffn2, ffn2_w_in=ffn2_w_in, ffn2_w_out=ffn2_w_out)
    wp = _prepare_weights(weights)
    depth = w_in.shape[0]
    nb, dt_ = x_prompt.shape[0], x_prompt.dtype
    zero_states = (
        jnp.zeros((depth, nb, RWKV_PROJ), dt_),
        jnp.zeros((depth, nb, RWKV_HEADS, RWKV_HEAD, RWKV_HEAD), dt_),
        jnp.zeros((depth, nb, SC_WIDTH - 1, SC_DIM), dt_),
        jnp.zeros((depth, nb, SSM_CONV - 1, SSM_CONV_DIM), dt_),
        jnp.zeros((depth, nb, SSM_HEADS, SSM_HEADDIM, SSM_STATE), dt_),
    )
    y_prompt, p_states = _trunk(x_prompt, zero_states, wp, norm_final)
    sample_states = (state_rwkv_shift, state_rwkv_wkv, state_sc_buf, state_ssm_conv, state_ssm)
    y_sample, s_states = _trunk(x_sample, sample_states, wp, norm_final)
    return (y_prompt, y_sample) + tuple(p_states) + tuple(s_states)
```

```python
import functools

import jax
import jax.numpy as jnp
from jax import lax
from jax.experimental import pallas as pl
from jax.experimental.pallas import tpu as pltpu

F32 = jnp.float32
MXU_DTYPE = jnp.bfloat16
HIGHEST = lax.Precision.HIGHEST

D_MODEL = 1024
D_FF = 2816
RWKV_DIM = 1024
RWKV_HEAD = 64
RWKV_HEADS = 16
DECAY_LORA = 64
AAA_LORA = 64
GATE_LORA = 160
RWKV_PROJ = 3 * RWKV_DIM + DECAY_LORA + AAA_LORA + GATE_LORA
RWKV_GN_EPS = 64e-5
DECAY_SCALE = 0.6065306597126334
SC_DIM = 512
SC_WIDTH = 3
SSM_DIM = 512
SSM_HEADDIM = 64
SSM_HEADS = 8
SSM_GROUPS = 2
SSM_STATE = 128
SSM_CONV = 4
SSM_CONV_DIM = SSM_DIM + 2 * SSM_GROUPS * SSM_STATE
NORM_EPS = 1e-6

COL_RKV = 0
COL_SC = 3 * RWKV_DIM
COL_Z = COL_SC + 3 * SC_DIM
COL_XBC = COL_Z + SSM_DIM
COL_SMALL = COL_XBC + SSM_CONV_DIM
SMALL_W = 512
SM_GL = 128
SM_DT = 384
PROJ_COLS = COL_SMALL + SMALL_W

LANE = 128
SUBLANE = 8
VMEM_LIMIT = 56 * 1024 * 1024

RWKV_CHUNK = 64
RWKV_CHUNKS_PER_STEP = 4
STEP_SEQS = 8
SSD_CHUNK = 128
SSD_CHUNKS_PER_STEP = 1


def _mm(a, b):
    return jnp.dot(a.astype(MXU_DTYPE), b.astype(MXU_DTYPE), preferred_element_type=F32)


def _mm_nt(a, b):
    return lax.dot_general(a.astype(MXU_DTYPE), b.astype(MXU_DTYPE), (((1,), (1,)), ((), ())),
                           preferred_element_type=F32)


def _mm_tn(a, b):
    return lax.dot_general(a.astype(MXU_DTYPE), b.astype(MXU_DTYPE), (((0,), (0,)), ((), ())),
                           preferred_element_type=F32)


def _mmx(a, b):
    return jnp.dot(a, b, precision=HIGHEST, preferred_element_type=F32)


def _mmx_nt(a, b):
    return lax.dot_general(a, b, (((1,), (1,)), ((), ())), precision=HIGHEST, preferred_element_type=F32)


def _mmx_tn(a, b):
    return lax.dot_general(a, b, (((0,), (0,)), ((), ())), precision=HIGHEST, preferred_element_type=F32)


def _sigmoid(x):
    return 1.0 / (1.0 + jnp.exp(-x))


def _silu(x):
    return x * _sigmoid(x)


def _softplus(x):
    return jnp.maximum(x, 0.0) + jnp.log1p(jnp.exp(-jnp.abs(x)))


def _rmsnorm(x, g):
    return x * lax.rsqrt(jnp.mean(x * x, axis=-1, keepdims=True) + NORM_EPS) * g


def _params(*sem):
    return pltpu.CompilerParams(dimension_semantics=sem, vmem_limit_bytes=VMEM_LIMIT)


def _ffn_kernel(x_ref, g_ref, wg_ref, wu_ref, wo_ref, gf_ref, o_ref, h_ref, acc_ref, *, final_norm):
    j = pl.program_id(1)

    @pl.when(j == 0)
    def _():
        h_ref[...] = _rmsnorm(x_ref[...], g_ref[...]).astype(h_ref.dtype)
        acc_ref[...] = jnp.zeros_like(acc_ref)

    h = h_ref[...]
    gate = jnp.dot(h, wg_ref[...], preferred_element_type=F32)
    up = jnp.dot(h, wu_ref[...], preferred_element_type=F32)
    act = (_silu(gate) * up).astype(MXU_DTYPE)
    acc_ref[...] += jnp.dot(act, wo_ref[...], preferred_element_type=F32)

    @pl.when(j == pl.num_programs(1) - 1)
    def _():
        y = x_ref[...] + 0.5 * acc_ref[...]
        if final_norm:
            y = _rmsnorm(y, gf_ref[...])
        o_ref[...] = y


def _ffn(x, norm_g, w_in, w_out, layer, final_g=None, *, tm, tf):
    m = x.shape[0]
    nf = D_FF // tf
    final_norm = final_g is not None
    gf = final_g if final_norm else norm_g[layer]
    return pl.pallas_call(
        functools.partial(_ffn_kernel, final_norm=final_norm),
        grid=(m // tm, nf),
        in_specs=[
            pl.BlockSpec((tm, D_MODEL), lambda i, j: (i, 0)),
            pl.BlockSpec((None, 1, D_MODEL), lambda i, j: (layer, 0, 0)),
            pl.BlockSpec((None, D_MODEL, tf), lambda i, j: (layer, 0, j)),
            pl.BlockSpec((None, D_MODEL, tf), lambda i, j: (layer, 0, nf + j)),
            pl.BlockSpec((None, tf, D_MODEL), lambda i, j: (layer, j, 0)),
            pl.BlockSpec((1, D_MODEL), lambda i, j: (0, 0)),
        ],
        out_specs=pl.BlockSpec((tm, D_MODEL), lambda i, j: (i, 0)),
        out_shape=jax.ShapeDtypeStruct((m, D_MODEL), F32),
        scratch_shapes=[pltpu.VMEM((tm, D_MODEL), MXU_DTYPE), pltpu.VMEM((tm, D_MODEL), F32)],
        compiler_params=_params("parallel", "arbitrary"),
        name="ffn",
    )(x, norm_g.reshape(-1, 1, D_MODEL), w_in, w_in, w_out, gf.reshape(1, D_MODEL))


def _proj_kernel(x_ref, g_ref, w_ref, o_ref, h_ref):
    @pl.when(pl.program_id(1) == 0)
    def _():
        h_ref[...] = _rmsnorm(x_ref[...], g_ref[...]).astype(h_ref.dtype)

    o_ref[...] = jnp.dot(h_ref[...], w_ref[...], preferred_element_type=F32)


def _proj_in(x, norm_g, w_all, layer, *, tm, tn):
    m = x.shape[0]
    return pl.pallas_call(
        _proj_kernel,
        grid=(m // tm, PROJ_COLS // tn),
        in_specs=[
            pl.BlockSpec((tm, D_MODEL), lambda i, j: (i, 0)),
            pl.BlockSpec((None, 1, D_MODEL), lambda i, j: (layer, 0, 0)),
            pl.BlockSpec((None, D_MODEL, tn), lambda i, j: (layer, 0, j)),
        ],
        out_specs=pl.BlockSpec((tm, tn), lambda i, j: (i, j)),
        out_shape=jax.ShapeDtypeStruct((m, PROJ_COLS), F32),
        scratch_shapes=[pltpu.VMEM((tm, D_MODEL), MXU_DTYPE)],
        compiler_params=_params("parallel", "arbitrary"),
        name="proj_in",
    )(x, norm_g.reshape(-1, 1, D_MODEL), w_all)


def _proj_out_kernel(x_ref, y1_ref, y2_ref, y3_ref, w1_ref, w2_ref, w3_ref, o_ref):
    acc = jnp.dot(y1_ref[...].astype(MXU_DTYPE), w1_ref[...], preferred_element_type=F32)
    acc += jnp.dot(y2_ref[...].astype(MXU_DTYPE), w2_ref[...], preferred_element_type=F32)
    acc += jnp.dot(y3_ref[...].astype(MXU_DTYPE), w3_ref[...], preferred_element_type=F32)
    o_ref[...] = x_ref[...] + acc


def _proj_out(x, y_rwkv, y_sc, y_ssm, w_out, layer, *, tm):
    m = x.shape[0]
    return pl.pallas_call(
        _proj_out_kernel,
        grid=(m // tm,),
        in_specs=[
            pl.BlockSpec((tm, D_MODEL), lambda i: (i, 0)),
            pl.BlockSpec((tm, RWKV_DIM), lambda i: (i, 0)),
            pl.BlockSpec((tm, SC_DIM), lambda i: (i, 0)),
            pl.BlockSpec((tm, SSM_DIM), lambda i: (i, 0)),
            pl.BlockSpec((None, RWKV_DIM, D_MODEL), lambda i: (layer, 0, 0)),
            pl.BlockSpec((None, SC_DIM, D_MODEL), lambda i: (layer, RWKV_DIM // SC_DIM, 0)),
            pl.BlockSpec((None, SSM_DIM, D_MODEL), lambda i: (layer, (RWKV_DIM + SC_DIM) // SSM_DIM, 0)),
        ],
        out_specs=pl.BlockSpec((tm, D_MODEL), lambda i: (i, 0)),
        out_shape=jax.ShapeDtypeStruct((m, D_MODEL), F32),
        compiler_params=_params("parallel"),
        name="proj_out",
    )(x, y_rwkv, y_sc, y_ssm, w_out, w_out, w_out)


def _previous_rows(x, carry_row):
    first = lax.broadcasted_iota(jnp.int32, x.shape, 0) == 0
    return jnp.where(first, carry_row, pltpu.roll(x, 1, 0))


def _rows_shifted(x, tail, s):
    rolled = pltpu.roll(x, s, 0)
    first = lax.broadcasted_iota(jnp.int32, tail.shape, 0) < s
    head = jnp.where(first, pltpu.roll(tail, s, 0), rolled[:SUBLANE])
    return jnp.concatenate([head, rolled[SUBLANE:]], axis=0)


def _rwkv_prepare(p_rkv, prev_rkv, p_sm, prev_sm, wts):
    (mu_rkv, mu_sm, w0, w2, a0, a2, g2, k_k, k_a) = wts
    xs = p_rkv + (prev_rkv - p_rkv) * mu_rkv
    xm = p_sm + (prev_sm - p_sm) * mu_sm
    r = xs[:, :RWKV_DIM]
    k = xs[:, RWKV_DIM:2 * RWKV_DIM]
    v = xs[:, 2 * RWKV_DIM:]
    w_lr = xm[:, :DECAY_LORA]
    a_lr = xm[:, DECAY_LORA:DECAY_LORA + AAA_LORA]
    g_lr = xm[:, SM_GL:SM_GL + GATE_LORA]
    log_decay = -DECAY_SCALE * _sigmoid(w0 + _mm(jnp.tanh(w_lr), w2))
    a = _sigmoid(a0 + _mm(a_lr, a2))
    g = _mm(_sigmoid(g_lr), g2)
    kk = k * k_k
    k_mod = k * (1.0 + (a - 1.0) * k_a)
    return r, log_decay, k_mod, v, kk, a, g


def _rwkv_weight_specs(layer):
    def vec(width):
        return pl.BlockSpec((None, 1, width), lambda *idx: (layer, 0, 0))

    def mat(rows):
        return pl.BlockSpec((None, rows, RWKV_DIM), lambda *idx: (layer, 0, 0))

    return [vec(3 * RWKV_DIM), vec(SMALL_W), vec(RWKV_DIM), mat(DECAY_LORA), vec(RWKV_DIM), mat(AAA_LORA),
            mat(GATE_LORA), vec(RWKV_DIM), vec(RWKV_DIM), vec(RWKV_DIM), vec(RWKV_DIM), vec(RWKV_DIM)]


def _rwkv_weight_args(wp):
    return (wp["mu_rkv"], wp["mu_sm"], wp["rwkv_w0"], wp["rwkv_w2"], wp["rwkv_a0"], wp["rwkv_a2"],
            wp["rwkv_g2"], wp["rwkv_k_k"], wp["rwkv_k_a"], wp["rwkv_r_k"], wp["rwkv_ln_w"], wp["rwkv_ln_b"])


HEADS_PER_GROUP = 4
GROUP_W = HEADS_PER_GROUP * RWKV_HEAD
RWKV_GROUPS = RWKV_HEADS // HEADS_PER_GROUP


def _split3(x):
    hi = x.astype(MXU_DTYPE)
    r1 = x - hi.astype(F32)
    mid = r1.astype(MXU_DTYPE)
    lo = (r1 - mid.astype(F32)).astype(MXU_DTYPE)
    return hi, mid, lo


def _block_diag(x, head_masks):
    return jnp.concatenate([x * m for m in head_masks], axis=0)


def _rwkv_chunk_kernel(p_rkv_ref, p_sm_ref, sh_rkv_ref, sh_sm_ref, s0_ref,
                       mu_rkv_ref, mu_sm_ref, w0_ref, w2_ref, a0_ref, a2_ref, g2_ref, kk_ref, ka_ref,
                       rk_ref, lnw_ref, lnb_ref,
                       y_ref, s_out_ref,
                       pad_rkv, pad_sm, state, at_s, rt_s, bt_s, kt_s, bh_s, kh_s, v_s, yn_s,
                       hm_s, ones_s, tri_s):
    c = RWKV_CHUNK
    nc = RWKV_CHUNKS_PER_STEP
    tl = nc * c
    n = RWKV_HEAD
    gw = GROUP_W
    t = pl.program_id(1)

    @pl.when(t == 0)
    def _():
        pad_rkv[...] = sh_rkv_ref[0]
        pad_sm[...] = sh_sm_ref[0]
        for h in range(RWKV_HEADS):
            state[:, h * n:(h + 1) * n] = s0_ref[0, h]
        lane_head = lax.broadcasted_iota(jnp.int32, (c, gw), 1) // n
        for hh in range(HEADS_PER_GROUP):
            hm_s[hh] = (lane_head == hh).astype(MXU_DTYPE)
        sq_r = lax.broadcasted_iota(jnp.int32, (gw, gw), 0) // n
        sq_c = lax.broadcasted_iota(jnp.int32, (gw, gw), 1) // n
        ones_s[...] = (sq_r == sq_c).astype(MXU_DTYPE)
        tri_r = lax.broadcasted_iota(jnp.int32, (tl, tl), 0)
        tri_c = lax.broadcasted_iota(jnp.int32, (tl, tl), 1)
        tri_s[...] = ((tri_r >= tri_c) & (tri_r // c == tri_c // c)).astype(MXU_DTYPE)

    wts = (mu_rkv_ref[...], mu_sm_ref[...], w0_ref[...], w2_ref[...], a0_ref[...], a2_ref[...],
           g2_ref[...], kk_ref[...], ka_ref[...])
    r, log_decay, k_mod, v, kk, a, g = _rwkv_prepare(
        p_rkv_ref[...], _previous_rows(p_rkv_ref[...], pad_rkv[...]),
        p_sm_ref[...], _previous_rows(p_sm_ref[...], pad_sm[...]), wts)
    pad_rkv[...] = p_rkv_ref[tl - 1:tl, :]
    pad_sm[...] = p_sm_ref[tl - 1:tl, :]

    row = lax.broadcasted_iota(jnp.int32, (c, gw), 0)
    lane = lax.broadcasted_iota(jnp.int32, (c, gw), 1)
    pos = lane % n
    strict = row > pos
    incl = row >= pos
    eye = (row == pos).astype(F32)
    head_masks = [hm_s[hh] for hh in range(HEADS_PER_GROUP)]
    ones_blk = ones_s[...]
    tri = tri_s[...]

    def head_sum(x):
        return jnp.concatenate(
            [jnp.dot(x[:, gi * gw:(gi + 1) * gw].astype(MXU_DTYPE), ones_blk, preferred_element_type=F32)
             for gi in range(RWKV_GROUPS)], axis=1)

    hi, mid, lo = _split3(log_decay)
    cum = (jnp.dot(tri, hi, preferred_element_type=F32) + jnp.dot(tri, mid, preferred_element_type=F32)
           + jnp.dot(tri, lo, preferred_element_type=F32))
    cum_end = jnp.concatenate([jnp.broadcast_to(cum[(ci + 1) * c - 1:(ci + 1) * c, :], (c, RWKV_DIM))
                               for ci in range(nc)], axis=0)
    e_out = jnp.exp(-cum)
    e_end = jnp.exp(cum_end - cum)
    kk = kk * lax.rsqrt(jnp.maximum(head_sum(kk * kk), 1e-24))
    b = kk * a
    at_s[...] = (-kk * jnp.exp(cum - log_decay)).astype(MXU_DTYPE)
    rt_s[...] = (r * jnp.exp(cum)).astype(MXU_DTYPE)
    bt_s[...] = (b * e_out).astype(MXU_DTYPE)
    kt_s[...] = (k_mod * e_out).astype(MXU_DTYPE)
    bh_s[...] = (b * e_end).astype(MXU_DTYPE)
    kh_s[...] = (k_mod * e_end).astype(MXU_DTYPE)
    v_s[...] = v
    bonus = head_sum(r * k_mod * rk_ref[...])
    w_end = jnp.exp(cum_end)

    def mm(x, y):
        return jnp.dot(x.astype(MXU_DTYPE), y, preferred_element_type=F32)

    def bd(x):
        return _block_diag(x.astype(MXU_DTYPE), head_masks)

    groups = range(RWKV_GROUPS)
    chunks = range(nc)
    pairs = [(ci, gi) for ci in chunks for gi in groups]
    rws = [slice(ci * c, (ci + 1) * c) for ci in chunks]
    sls = [slice(gi * gw, (gi + 1) * gw) for gi in groups]
    nt = (((1,), (1,)), ((), ()))
    ar = {(ci, gi): jnp.concatenate([at_s[rws[ci], sls[gi]], rt_s[rws[ci], sls[gi]]], axis=0)
          for ci, gi in pairs}
    vg = {(ci, gi): v_s[rws[ci], sls[gi]].astype(MXU_DTYPE) for ci, gi in pairs}
    gram_b = {(ci, gi): lax.dot_general(ar[ci, gi], bd(bt_s[rws[ci], sls[gi]]), nt, preferred_element_type=F32)
              for ci, gi in pairs}
    gram_k = {(ci, gi): lax.dot_general(ar[ci, gi], bd(kt_s[rws[ci], sls[gi]]), nt, preferred_element_type=F32)
              for ci, gi in pairs}
    l_ab = {p: jnp.where(strict, gram_b[p][:c], 0.0) for p in pairs}
    m_rb = {p: jnp.where(incl, gram_b[p][c:], 0.0) for p in pairs}
    l_akrk = {p: jnp.where(jnp.concatenate([strict, incl], axis=0), gram_k[p], 0.0) for p in pairs}
    inv = {p: eye + l_ab[p] for p in pairs}
    power = l_ab
    power_bd = {p: bd(power[p]) for p in pairs}
    span = 2
    while span < c:
        power = {p: mm(power[p], power_bd[p]) for p in pairs}
        power_bd = {p: bd(power[p]) for p in pairs}
        inv = {p: inv[p] + mm(inv[p], power_bd[p]) for p in pairs}
        span *= 2
    y = {}
    for ci in chunks:
        s0 = [state[:, sl] for sl in sls]
        z = [mm(l_akrk[ci, gi], bd(vg[ci, gi]))
             + lax.dot_general(ar[ci, gi], bd(s0[gi]), nt, preferred_element_type=F32) for gi in groups]
        u = [mm(inv[ci, gi], bd(z[gi][:c])) for gi in groups]
        for gi in groups:
            y[ci, gi] = z[gi][c:] + mm(m_rb[ci, gi], bd(u[gi]))
        for gi in groups:
            uv = jnp.concatenate([u[gi].astype(MXU_DTYPE), vg[ci, gi]], axis=0)
            bk = jnp.concatenate([bh_s[rws[ci], sls[gi]], kh_s[rws[ci], sls[gi]]], axis=0)
            upd = lax.dot_general(uv, bk, (((0,), (0,)), ((), ())), preferred_element_type=F32)
            s_new = s0[gi] * w_end[ci * c:ci * c + 1, sls[gi]]
            for hh in range(HEADS_PER_GROUP):
                s_new += upd[hh * n:(hh + 1) * n, :] * head_masks[hh].astype(F32)
            state[:, sls[gi]] = s_new
    mean = {p: mm(y[p], ones_blk) * (1.0 / n) for p in pairs}
    yc = {p: y[p] - mean[p] for p in pairs}
    var = {p: mm(yc[p] * yc[p], ones_blk) * (1.0 / n) for p in pairs}
    for ci, gi in pairs:
        yn_s[rws[ci], sls[gi]] = yc[ci, gi] * lax.rsqrt(var[ci, gi] + RWKV_GN_EPS)

    y_ref[...] = (yn_s[...] * lnw_ref[...] + lnb_ref[...] + bonus * v_s[...]) * g

    @pl.when(t == pl.num_programs(1) - 1)
    def _():
        for h in range(RWKV_HEADS):
            s_out_ref[0, h] = state[:, h * n:(h + 1) * n]


def _rwkv_prompt(proj, shift_rkv, shift_sm, wkv0, wp, layer, *, batch, seqlen):
    c = RWKV_CHUNK * RWKV_CHUNKS_PER_STEP
    nt = seqlen // c
    return pl.pallas_call(
        _rwkv_chunk_kernel,
        grid=(batch, nt),
        in_specs=[
            pl.BlockSpec((c, 3 * RWKV_DIM), lambda b, t: (b * nt + t, COL_RKV // (3 * RWKV_DIM))),
            pl.BlockSpec((c, SMALL_W), lambda b, t: (b * nt + t, COL_SMALL // SMALL_W)),
            pl.BlockSpec((1, 1, 3 * RWKV_DIM), lambda b, t: (b, 0, 0)),
            pl.BlockSpec((1, 1, SMALL_W), lambda b, t: (b, 0, 0)),
            pl.BlockSpec((1, RWKV_HEADS, RWKV_HEAD, RWKV_HEAD), lambda b, t: (b, 0, 0, 0)),
        ] + _rwkv_weight_specs(layer),
        out_specs=[
            pl.BlockSpec((c, RWKV_DIM), lambda b, t: (b * nt + t, 0)),
            pl.BlockSpec((1, RWKV_HEADS, RWKV_HEAD, RWKV_HEAD), lambda b, t: (b, 0, 0, 0)),
        ],
        out_shape=[
            jax.ShapeDtypeStruct((batch * seqlen, RWKV_DIM), F32),
            jax.ShapeDtypeStruct((batch, RWKV_HEADS, RWKV_HEAD, RWKV_HEAD), F32),
        ],
        scratch_shapes=[
            pltpu.VMEM((1, 3 * RWKV_DIM), F32),
            pltpu.VMEM((1, SMALL_W), F32),
            pltpu.VMEM((RWKV_HEAD, RWKV_DIM), F32),
        ] + [pltpu.VMEM((c, RWKV_DIM), MXU_DTYPE) for _ in range(6)]
          + [pltpu.VMEM((c, RWKV_DIM), F32) for _ in range(2)]
          + [pltpu.VMEM((HEADS_PER_GROUP, RWKV_CHUNK, GROUP_W), MXU_DTYPE),
             pltpu.VMEM((GROUP_W, GROUP_W), MXU_DTYPE), pltpu.VMEM((c, c), MXU_DTYPE)],
        compiler_params=_params("parallel", "arbitrary"),
        name="rwkv_chunk",
    )(proj, proj, shift_rkv, shift_sm, wkv0, *_rwkv_weight_args(wp))


def _sc_kernel(p_ref, buf_ref, w_ref, g_ref, y_ref, buf_out_ref, pad, *, tl):
    t = pl.program_id(1)
    hist = SC_WIDTH - 1

    @pl.when(t == 0)
    def _():
        pad[...] = jnp.zeros_like(pad)
        pad[SUBLANE - hist:SUBLANE, :] = buf_ref[0]

    b_gate = p_ref[:, :SC_DIM]
    u = p_ref[:, SC_DIM:2 * SC_DIM] * p_ref[:, 2 * SC_DIM:]
    tail = pad[...]
    conv = u * w_ref[hist:hist + 1, :]
    for j in range(hist):
        conv += _rows_shifted(u, tail, hist - j) * w_ref[j:j + 1, :]
    y_ref[...] = _rmsnorm(b_gate * conv, g_ref[...])
    pad[...] = u[tl - SUBLANE:, :]

    @pl.when(t == pl.num_programs(1) - 1)
    def _():
        buf_out_ref[0] = pad[SUBLANE - hist:SUBLANE, :]


def _sc_prompt(proj, buf0, wp, layer, *, batch, seqlen, tl):
    nt = seqlen // tl
    hist = SC_WIDTH - 1
    return pl.pallas_call(
        functools.partial(_sc_kernel, tl=tl),
        grid=(batch, nt),
        in_specs=[
            pl.BlockSpec((tl, 3 * SC_DIM), lambda b, t: (b * nt + t, COL_SC // (3 * SC_DIM))),
            pl.BlockSpec((1, hist, SC_DIM), lambda b, t: (b, 0, 0)),
            pl.BlockSpec((None, SC_WIDTH, SC_DIM), lambda b, t: (layer, 0, 0)),
            pl.BlockSpec((None, 1, SC_DIM), lambda b, t: (layer, 0, 0)),
        ],
        out_specs=[
            pl.BlockSpec((tl, SC_DIM), lambda b, t: (b * nt + t, 0)),
            pl.BlockSpec((1, hist, SC_DIM), lambda b, t: (b, 0, 0)),
        ],
        out_shape=[
            jax.ShapeDtypeStruct((batch * seqlen, SC_DIM), F32),
            jax.ShapeDtypeStruct((batch, hist, SC_DIM), F32),
        ],
        scratch_shapes=[pltpu.VMEM((SUBLANE, SC_DIM), F32)],
        compiler_params=_params("parallel", "arbitrary"),
        name="short_conv",
    )(proj, buf0, wp["sc_conv_w"], wp["sc_norm"])


def _ssd_kernel(z_ref, xbc_ref, dt_ref, buf_ref, h0_ref, cw_ref, cb_ref, dtb_ref, alog_ref, dskip_ref, g_ref,
                y_ref, buf_out_ref, h_out_ref, pad, state, y_s, exh_s, exs_s):
    q = SSD_CHUNK
    nq = SSD_CHUNKS_PER_STEP
    tq = nq * q
    t = pl.program_id(1)
    hist = SSM_CONV - 1
    gw = SSM_DIM // SSM_GROUPS

    @pl.when(t == 0)
    def _():
        pad[...] = jnp.zeros_like(pad)
        pad[SUBLANE - hist:SUBLANE, :] = buf_ref[0]
        state[...] = h0_ref[0]
        head_row = lax.broadcasted_iota(jnp.int32, (LANE, SSM_DIM), 0)
        exh_s[...] = (lax.broadcasted_iota(jnp.int32, (LANE, SSM_DIM), 1) // SSM_HEADDIM == head_row).astype(MXU_DTYPE)
        seg_row = lax.broadcasted_iota(jnp.int32, (LANE, SSM_HEADS * q), 0)
        exs_s[...] = (lax.broadcasted_iota(jnp.int32, (LANE, SSM_HEADS * q), 1) // q == seg_row).astype(MXU_DTYPE)

    xbc = xbc_ref[...]
    tail = pad[...]
    conv = xbc * cw_ref[hist:hist + 1, :]
    for j in range(hist):
        conv += _rows_shifted(xbc, tail, hist - j) * cw_ref[j:j + 1, :]
    pad[...] = xbc[tq - SUBLANE:, :]
    xc = _silu(conv + cb_ref[...])
    xh = xc[:, :SSM_DIM]

    dt = _softplus(dt_ref[...] + dtb_ref[...])
    neg_a = -jnp.exp(alog_ref[...])
    row = lax.broadcasted_iota(jnp.int32, (tq, tq), 0)
    col = lax.broadcasted_iota(jnp.int32, (tq, tq), 1)
    tri = ((row >= col) & (row // q == col // q)).astype(MXU_DTYPE)
    hi, mid, lo = _split3(dt * neg_a)
    cum = (jnp.dot(tri, hi, preferred_element_type=F32) + jnp.dot(tri, mid, preferred_element_type=F32)
           + jnp.dot(tri, lo, preferred_element_type=F32))
    cum_t = cum.T
    cum_last = jnp.concatenate([jnp.broadcast_to(cum[(ci + 1) * q - 1:(ci + 1) * q, :], (q, LANE))
                                for ci in range(nq)], axis=0)

    def expand(x, e):
        h3 = _split3(x)
        return sum(jnp.dot(part, e, preferred_element_type=F32) for part in h3)

    ex_head = exh_s[...]
    ex_seg = exs_s[...]
    xdt_all = xh * expand(dt, ex_head)
    xdt_end = xdt_all * expand(jnp.exp(cum_last - cum), ex_head)
    ecum_x = expand(jnp.exp(cum), ex_head)
    cum_seg = expand(cum, ex_seg)
    causal = lax.broadcasted_iota(jnp.int32, (q, q), 0) >= lax.broadcasted_iota(jnp.int32, (q, q), 1)
    heads_per_group = SSM_HEADS // SSM_GROUPS

    heads = range(SSM_HEADS)
    chunks = range(nq)
    pairs = [(ci, h) for ci in chunks for h in heads]
    grp_of = [h // heads_per_group for h in heads]
    rws = [slice(ci * q, (ci + 1) * q) for ci in chunks]
    sls = [slice(h * SSM_HEADDIM, (h + 1) * SSM_HEADDIM) for h in heads]
    bm = {(ci, g): xc[rws[ci], SSM_DIM + g * SSM_STATE:SSM_DIM + (g + 1) * SSM_STATE].astype(MXU_DTYPE)
          for ci in chunks for g in range(SSM_GROUPS)}
    cm = {(ci, g): xc[rws[ci], SSM_DIM + (SSM_GROUPS + g) * SSM_STATE:SSM_DIM + (SSM_GROUPS + g + 1) * SSM_STATE]
          .astype(MXU_DTYPE) for ci in chunks for g in range(SSM_GROUPS)}
    gram = {k: _mm_nt(cm[k], bm[k]) for k in bm}
    cum_end = {(ci, h): cum[(ci + 1) * q - 1:(ci + 1) * q, h:h + 1] for ci, h in pairs}
    seg = {(ci, h): cum_seg[rws[ci], h * q:(h + 1) * q] - cum_t[h:h + 1, rws[ci]] for ci, h in pairs}
    decay = {p: jnp.where(causal, jnp.exp(jnp.where(causal, seg[p], 0.0)), 0.0) for p in pairs}
    xdt = {(ci, h): xdt_all[rws[ci], sls[h]] for ci, h in pairs}
    y = {(ci, h): _mm(gram[ci, grp_of[h]] * decay[ci, h], xdt[ci, h]) for ci, h in pairs}
    upd = {(ci, h): _mm_tn(xdt_end[rws[ci], sls[h]], bm[ci, grp_of[h]]) for ci, h in pairs}
    for ci in chunks:
        y_off = [_mm_nt(cm[ci, grp_of[h]], state[h]) * ecum_x[rws[ci], sls[h]] for h in heads]
        for h in heads:
            state[h] = state[h] * jnp.exp(cum_end[ci, h]) + upd[ci, h]
            y_s[rws[ci], sls[h]] = y[ci, h] + y_off[h]

    y = (y_s[...] + expand(dskip_ref[...], ex_head) * xh) * _silu(z_ref[...])
    for grp in range(SSM_GROUPS):
        yg = y[:, grp * gw:(grp + 1) * gw]
        y_ref[:, grp * gw:(grp + 1) * gw] = _rmsnorm(yg, g_ref[:, grp * gw:(grp + 1) * gw])

    @pl.when(t == pl.num_programs(1) - 1)
    def _():
        buf_out_ref[0] = pad[SUBLANE - hist:SUBLANE, :]
        h_out_ref[0] = state[...]


def _ssd_prompt(proj, buf0, h0, wp, layer, *, batch, seqlen):
    q = SSD_CHUNK * SSD_CHUNKS_PER_STEP
    nt = seqlen // q
    hist = SSM_CONV - 1
    lane_vec = pl.BlockSpec((None, 1, LANE), lambda b, t: (layer, 0, 0))
    return pl.pallas_call(
        _ssd_kernel,
        grid=(batch, nt),
        in_specs=[
            pl.BlockSpec((q, SSM_DIM), lambda b, t: (b * nt + t, COL_Z // SSM_DIM)),
            pl.BlockSpec((q, SSM_CONV_DIM), lambda b, t: (b * nt + t, COL_XBC // SSM_CONV_DIM)),
            pl.BlockSpec((q, LANE), lambda b, t: (b * nt + t, (COL_SMALL + SM_DT) // LANE)),
            pl.BlockSpec((1, hist, SSM_CONV_DIM), lambda b, t: (b, 0, 0)),
            pl.BlockSpec((1, SSM_HEADS, SSM_HEADDIM, SSM_STATE), lambda b, t: (b, 0, 0, 0)),
            pl.BlockSpec((None, SSM_CONV, SSM_CONV_DIM), lambda b, t: (layer, 0, 0)),
            pl.BlockSpec((None, 1, SSM_CONV_DIM), lambda b, t: (layer, 0, 0)),
            lane_vec, lane_vec, lane_vec,
            pl.BlockSpec((None, 1, SSM_DIM), lambda b, t: (layer, 0, 0)),
        ],
        out_specs=[
            pl.BlockSpec((q, SSM_DIM), lambda b, t: (b * nt + t, 0)),
            pl.BlockSpec((1, hist, SSM_CONV_DIM), lambda b, t: (b, 0, 0)),
            pl.BlockSpec((1, SSM_HEADS, SSM_HEADDIM, SSM_STATE), lambda b, t: (b, 0, 0, 0)),
        ],
        out_shape=[
            jax.ShapeDtypeStruct((batch * seqlen, SSM_DIM), F32),
            jax.ShapeDtypeStruct((batch, hist, SSM_CONV_DIM), F32),
            jax.ShapeDtypeStruct((batch, SSM_HEADS, SSM_HEADDIM, SSM_STATE), F32),
        ],
        scratch_shapes=[
            pltpu.VMEM((SUBLANE, SSM_CONV_DIM), F32),
            pltpu.VMEM((SSM_HEADS, SSM_HEADDIM, SSM_STATE), F32),
            pltpu.VMEM((q, SSM_DIM), F32),
            pltpu.VMEM((LANE, SSM_DIM), MXU_DTYPE),
            pltpu.VMEM((LANE, SSM_HEADS * SSD_CHUNK), MXU_DTYPE),
        ],
        compiler_params=_params("parallel", "arbitrary"),
        name="ssd_chunk",
    )(proj, proj, proj, buf0, h0, wp["ssm_conv_w"], wp["ssm_conv_b"], wp["dt_bias_pad"], wp["a_log_pad"],
      wp["d_pad"], wp["ssm_norm"])


def _lane_sums(x, ones):
    hi = x.astype(MXU_DTYPE)
    mid = (x - hi.astype(F32)).astype(MXU_DTYPE)
    return jnp.dot(hi, ones, preferred_element_type=F32) + jnp.dot(mid, ones, preferred_element_type=F32)


def _row_sums(sel, x):
    hi = x.astype(MXU_DTYPE)
    mid = (x - hi.astype(F32)).astype(MXU_DTYPE)
    return jnp.dot(sel, hi, preferred_element_type=F32) + jnp.dot(sel, mid, preferred_element_type=F32)


def _head_sums_rows(x, ones_pair):
    return jnp.concatenate([_lane_sums(x[:, q * LANE:(q + 1) * LANE], ones_pair)
                            for q in range(RWKV_DIM // LANE)], axis=1)


def _step_kernel(p_rkv_ref, p_sm_ref, p_sc_ref, z_ref, xbc_ref,
                 sh_rkv_ref, sh_sm_ref, wkv_ref, scbuf_ref, ssmbuf_ref, ssm_ref,
                 mu_rkv_ref, mu_sm_ref, w0_ref, w2_ref, a0_ref, a2_ref, g2_ref, kk_ref, ka_ref,
                 rk_ref, lnw_ref, lnb_ref,
                 scw_ref, scg_ref, cw_ref, cb_ref, dtb_ref, alog_ref, dskip_ref, ssmg_ref,
                 y_rwkv_ref, y_sc_ref, y_ssm_ref, wkv_out_ref, scbuf_out_ref, ssmbuf_out_ref, ssm_out_ref,
                 even_s, odd_s, yrow_s, yssm_s):
    n = RWKV_HEAD
    half = n // 2
    nseq = p_rkv_ref.shape[1]
    seqs = range(nseq)
    i128 = lambda shape, d: lax.broadcasted_iota(jnp.int32, shape, d)
    ones_pair = (i128((LANE, LANE), 0) // n == i128((LANE, LANE), 1) // n).astype(MXU_DTYPE)
    ones_full = jnp.ones((LANE, LANE), MXU_DTYPE)
    diag2 = i128((half, LANE), 1) % n == 2 * i128((half, LANE), 0) + i128((half, LANE), 1) // n
    diag_lo = i128((n, LANE), 1) == i128((n, LANE), 0)
    diag_hi = i128((n, LANE), 1) == i128((n, LANE), 0) + n
    upper_row = (i128((nseq, RWKV_DIM), 1) // n) % 2 == 1
    upper_lane = i128((1, LANE), 1) // n == 1

    wts = (mu_rkv_ref[...], mu_sm_ref[...], w0_ref[...], w2_ref[...], a0_ref[...], a2_ref[...],
           g2_ref[...], kk_ref[...], ka_ref[...])
    r, log_decay, k_mod, v, kk, a, g = _rwkv_prepare(p_rkv_ref[0], sh_rkv_ref[0], p_sm_ref[0], sh_sm_ref[0], wts)
    kk = kk * lax.rsqrt(jnp.maximum(_head_sums_rows(kk * kk, ones_pair), 1e-24))
    bonus = _head_sums_rows(r * k_mod * rk_ref[...], ones_pair)
    for i, x in enumerate((-kk, jnp.exp(log_decay), kk * a, k_mod, r, v)):
        even_s[i] = jnp.where(upper_row, pltpu.roll(x, n, 1), x)
        odd_s[i] = jnp.where(upper_row, x, pltpu.roll(x, RWKV_DIM - n, 1))

    def head_vec(s, h, i):
        src = odd_s if h % 2 else even_s
        return src[i, s:s + 1, (h // 2) * LANE:(h // 2 + 1) * LANE]

    heads = range(RWKV_HEADS)
    sh = [(s, h) for s in seqs for h in heads]
    rs = lambda x, s, h: x[(s * RWKV_HEADS + h) * half:(s * RWKV_HEADS + h + 1) * half]
    sa = _lane_sums(jnp.concatenate([wkv_ref[s, h] * head_vec(s, h, 0) for s, h in sh], axis=0), ones_pair)
    v_col = _lane_sums(jnp.concatenate([jnp.where(diag2, head_vec(s, h, 5), 0.0) for s, h in sh], axis=0),
                       ones_pair)
    for s, h in sh:
        wkv_out_ref[s, h] = (wkv_ref[s, h] * head_vec(s, h, 1) + rs(sa, s, h) * head_vec(s, h, 2)
                             + rs(v_col, s, h) * head_vec(s, h, 3))
    y_b = _lane_sums(jnp.concatenate([wkv_out_ref[s, h] * head_vec(s, h, 4) for s, h in sh], axis=0), ones_pair)
    y_d = jnp.concatenate([jnp.where(diag2, rs(y_b, s, h), 0.0) for s, h in sh], axis=0)
    nrow = nseq * RWKV_HEADS
    sel = (i128((nrow, nrow * half), 1) // half == i128((nrow, nrow * half), 0)).astype(MXU_DTYPE)
    y_h = _row_sums(sel, y_d)
    y_h = y_h + pltpu.roll(y_h, n, 1)
    for s in seqs:
        base = s * RWKV_HEADS
        yrow_s[s:s + 1, :] = jnp.concatenate(
            [jnp.where(upper_lane, y_h[base + 2 * q + 1:base + 2 * q + 2], y_h[base + 2 * q:base + 2 * q + 1])
             for q in range(RWKV_HEADS // 2)], axis=1)
    y = yrow_s[...]
    mean = _head_sums_rows(y, ones_pair) * (1.0 / n)
    yc = y - mean
    var = _head_sums_rows(yc * yc, ones_pair) * (1.0 / n)
    y_rwkv_ref[0] = (yc * lax.rsqrt(var + RWKV_GN_EPS) * lnw_ref[...] + lnb_ref[...] + bonus * v) * g

    p_sc = p_sc_ref[0]
    u = p_sc[:, SC_DIM:2 * SC_DIM] * p_sc[:, 2 * SC_DIM:]
    buf0 = scbuf_ref[0][:, :SC_DIM]
    buf1 = scbuf_ref[0][:, SC_DIM:]
    conv = buf0 * scw_ref[0:1, :] + buf1 * scw_ref[1:2, :] + u * scw_ref[2:3, :]
    y_sc_ref[0] = _rmsnorm(p_sc[:, :SC_DIM] * conv, scg_ref[...])
    scbuf_out_ref[0, :, :SC_DIM] = buf1
    scbuf_out_ref[0, :, SC_DIM:] = u

    xbc = xbc_ref[0]
    cbuf = ssmbuf_ref[0]
    conv = xbc * cw_ref[3:4, :]
    for j in range(SSM_CONV - 1):
        conv += cbuf[:, j * SSM_CONV_DIM:(j + 1) * SSM_CONV_DIM] * cw_ref[j:j + 1, :]
    ssmbuf_out_ref[0, :, :2 * SSM_CONV_DIM] = cbuf[:, SSM_CONV_DIM:]
    ssmbuf_out_ref[0, :, 2 * SSM_CONV_DIM:] = xbc
    xc = _silu(conv + cb_ref[...])
    xh = xc[:, :SSM_DIM]
    dt = _softplus(p_sm_ref[0][:, SM_DT:SM_DT + LANE] + dtb_ref[...])
    decay_all = jnp.exp(dt * (-jnp.exp(alog_ref[...])))
    head_of_lane = i128((nseq, SSM_DIM), 1) // SSM_HEADDIM
    dt_exp = jnp.zeros((nseq, SSM_DIM), F32)
    d_exp = jnp.zeros((nseq, SSM_DIM), F32)
    for h in range(SSM_HEADS):
        dt_exp = jnp.where(head_of_lane == h, dt[:, h:h + 1], dt_exp)
        d_exp = jnp.where(head_of_lane == h, dskip_ref[:, h:h + 1], d_exp)
    xdt = xh * dt_exp
    heads = range(SSM_HEADS)
    sh = [(s, h) for s in seqs for h in heads]
    heads_per_group = SSM_HEADS // SSM_GROUPS
    bm = lambda s, h: xc[s:s + 1, SSM_DIM + (h // heads_per_group) * SSM_STATE:
                         SSM_DIM + (h // heads_per_group + 1) * SSM_STATE]
    cm = lambda s, h: xc[s:s + 1, SSM_DIM + (SSM_GROUPS + h // heads_per_group) * SSM_STATE:
                         SSM_DIM + (SSM_GROUPS + h // heads_per_group + 1) * SSM_STATE]
    diag = lambda h: diag_hi if h % 2 else diag_lo
    slab = lambda x, s, h: x[s:s + 1, (h // 2) * LANE:(h // 2 + 1) * LANE]
    rs = lambda x, s, h: x[(s * SSM_HEADS + h) * SSM_HEADDIM:(s * SSM_HEADS + h + 1) * SSM_HEADDIM]
    xdt_col = _lane_sums(jnp.concatenate([jnp.where(diag(h), slab(xdt, s, h), 0.0) for s, h in sh], axis=0),
                         ones_full)
    for s, h in sh:
        ssm_out_ref[s, h] = ssm_ref[s, h] * decay_all[s:s + 1, h:h + 1] + rs(xdt_col, s, h) * bm(s, h)
    y_b = _lane_sums(jnp.concatenate([ssm_out_ref[s, h] * cm(s, h) for s, h in sh], axis=0), ones_full)
    y_d = jnp.concatenate([jnp.where(diag(h), rs(y_b, s, h), 0.0) for s, h in sh], axis=0)
    nrow = nseq * SSM_HEADS
    sel = (i128((nrow, nrow * SSM_HEADDIM), 1) // SSM_HEADDIM == i128((nrow, nrow * SSM_HEADDIM), 0)).astype(MXU_DTYPE)
    y_h = _row_sums(sel, y_d)
    for s in seqs:
        base = s * SSM_HEADS
        yssm_s[s:s + 1, :] = jnp.concatenate(
            [y_h[base + 2 * q:base + 2 * q + 1] + y_h[base + 2 * q + 1:base + 2 * q + 2]
             for q in range(SSM_HEADS // 2)], axis=1)
    y = (yssm_s[...] + d_exp * xh) * _silu(z_ref[0])
    gw = SSM_DIM // SSM_GROUPS
    for grp in range(SSM_GROUPS):
        yg = y[:, grp * gw:(grp + 1) * gw]
        y_ssm_ref[0, :, grp * gw:(grp + 1) * gw] = _rmsnorm(yg, ssmg_ref[:, grp * gw:(grp + 1) * gw])


def _mixers_step(proj, states, wp, layer, *, batch):
    shift_rkv, shift_sm, wkv, sc_buf, ssm_buf, ssm = states
    nseq = _tile(batch, STEP_SEQS)
    nb = batch // nseq
    grp = lambda a: a.reshape((nb, nseq, -1))
    p3 = grp(proj)
    wkv2 = wkv.reshape(batch, RWKV_HEADS, RWKV_HEAD // 2, LANE)
    row = lambda width, blk: pl.BlockSpec((1, nseq, width), lambda b: (b, 0, blk))
    st4 = lambda d1, d2, d3: pl.BlockSpec((nseq, d1, d2, d3), lambda b: (b, 0, 0, 0))
    vec = lambda width: pl.BlockSpec((None, 1, width), lambda b: (layer, 0, 0))
    mat = lambda rows, width: pl.BlockSpec((None, rows, width), lambda b: (layer, 0, 0))
    sc_w = (SC_WIDTH - 1) * SC_DIM
    ssm_w = (SSM_CONV - 1) * SSM_CONV_DIM
    outs = pl.pallas_call(
        _step_kernel,
        grid=(nb,),
        in_specs=[
            row(3 * RWKV_DIM, COL_RKV // (3 * RWKV_DIM)),
            row(SMALL_W, COL_SMALL // SMALL_W),
            row(3 * SC_DIM, COL_SC // (3 * SC_DIM)),
            row(SSM_DIM, COL_Z // SSM_DIM),
            row(SSM_CONV_DIM, COL_XBC // SSM_CONV_DIM),
            row(3 * RWKV_DIM, 0), row(SMALL_W, 0),
            st4(RWKV_HEADS, RWKV_HEAD // 2, LANE),
            row(sc_w, 0), row(ssm_w, 0),
            st4(SSM_HEADS, SSM_HEADDIM, SSM_STATE),
        ] + [vec(3 * RWKV_DIM), vec(SMALL_W), vec(RWKV_DIM), mat(DECAY_LORA, RWKV_DIM), vec(RWKV_DIM),
             mat(AAA_LORA, RWKV_DIM), mat(GATE_LORA, RWKV_DIM), vec(RWKV_DIM), vec(RWKV_DIM), vec(RWKV_DIM),
             vec(RWKV_DIM), vec(RWKV_DIM),
             mat(SC_WIDTH, SC_DIM), vec(SC_DIM), mat(SSM_CONV, SSM_CONV_DIM), vec(SSM_CONV_DIM),
             vec(LANE), vec(LANE), vec(LANE), vec(SSM_DIM)],
        out_specs=[
            row(RWKV_DIM, 0), row(SC_DIM, 0), row(SSM_DIM, 0),
            st4(RWKV_HEADS, RWKV_HEAD // 2, LANE),
            row(sc_w, 0), row(ssm_w, 0),
            st4(SSM_HEADS, SSM_HEADDIM, SSM_STATE),
        ],
        out_shape=[
            jax.ShapeDtypeStruct((nb, nseq, RWKV_DIM), F32),
            jax.ShapeDtypeStruct((nb, nseq, SC_DIM), F32),
            jax.ShapeDtypeStruct((nb, nseq, SSM_DIM), F32),
            jax.ShapeDtypeStruct(wkv2.shape, F32),
            jax.ShapeDtypeStruct((nb, nseq, sc_w), F32),
            jax.ShapeDtypeStruct((nb, nseq, ssm_w), F32),
            jax.ShapeDtypeStruct(ssm.shape, F32),
        ],
        scratch_shapes=[pltpu.VMEM((6, nseq, RWKV_DIM), F32), pltpu.VMEM((6, nseq, RWKV_DIM), F32),
                        pltpu.VMEM((nseq, RWKV_DIM), F32), pltpu.VMEM((nseq, SSM_DIM), F32)],
        compiler_params=_params("parallel"),
        name="mixers_step",
    )(p3, p3, p3, p3, p3, grp(shift_rkv), grp(shift_sm), wkv2, grp(sc_buf), grp(ssm_buf), ssm,
      *_rwkv_weight_args(wp), wp["sc_conv_w"], wp["sc_norm"], wp["ssm_conv_w"], wp["ssm_conv_b"],
      wp["dt_bias_pad"], wp["a_log_pad"], wp["d_pad"], wp["ssm_norm"])
    y_rwkv, y_sc, y_ssm, wkv_n, sc_n, ssmbuf_n, ssm_n = outs
    return (y_rwkv.reshape(batch, RWKV_DIM), y_sc.reshape(batch, SC_DIM), y_ssm.reshape(batch, SSM_DIM),
            wkv_n.reshape(wkv.shape), sc_n.reshape(sc_buf.shape), ssmbuf_n.reshape(ssm_buf.shape), ssm_n)


def _reorder_cols(a):
    lead = a.shape[:-1]
    rkv = a[..., :3 * RWKV_DIM]
    lora = a[..., 3 * RWKV_DIM:RWKV_PROJ]
    small = jnp.concatenate([lora, jnp.zeros(lead + (SMALL_W - lora.shape[-1],), a.dtype)], axis=-1)
    return rkv, small


def _prepare_weights(w):
    depth = w["w_in"].shape[0]
    wp = {}
    w_in = w["w_in"]
    o_ssm = RWKV_PROJ + 3 * SC_DIM
    o_dt = o_ssm + SSM_DIM + SSM_CONV_DIM
    piece = lambda lo, hi: w_in[..., lo:hi].astype(MXU_DTYPE)
    zeros = lambda width: jnp.zeros(w_in.shape[:-1] + (width,), MXU_DTYPE)
    wp["w_all"] = jnp.concatenate(
        [piece(0, 3 * RWKV_DIM), piece(RWKV_PROJ, o_dt),
         piece(3 * RWKV_DIM, RWKV_PROJ), zeros(SM_DT - (RWKV_PROJ - 3 * RWKV_DIM)),
         piece(o_dt, o_dt + SSM_HEADS), zeros(SMALL_W - SM_DT - SSM_HEADS)],
        axis=-1)
    mu_rkv, mu_sm = _reorder_cols(w["rwkv_mu"])
    wp["mu_rkv"] = mu_rkv.reshape(depth, 1, -1)
    wp["mu_sm"] = mu_sm.reshape(depth, 1, -1)
    for name in ("rwkv_w0", "rwkv_a0", "rwkv_k_k", "rwkv_k_a", "rwkv_ln_w", "rwkv_ln_b", "sc_norm",
                 "ssm_conv_b", "ssm_norm"):
        wp[name] = w[name].reshape(depth, 1, -1)
    wp["rwkv_r_k"] = w["rwkv_r_k"].reshape(depth, 1, RWKV_DIM)
    for name in ("rwkv_w2", "rwkv_a2", "rwkv_g2"):
        wp[name] = w[name].astype(MXU_DTYPE)
    wp["sc_conv_w"] = w["sc_conv_w"]
    wp["ssm_conv_w"] = w["ssm_conv_w"]
    pad_lane = lambda a: jnp.pad(a, ((0, 0), (0, LANE - a.shape[-1]))).reshape(depth, 1, LANE)
    wp["dt_bias_pad"] = pad_lane(w["ssm_dt_bias"])
    wp["a_log_pad"] = pad_lane(w["ssm_A_log"])
    wp["d_pad"] = pad_lane(w["ssm_D"])
    for name in ("ffn1_w_in", "ffn1_w_out", "ffn2_w_in", "ffn2_w_out", "w_out"):
        wp[name] = w[name].astype(MXU_DTYPE)
    for name in ("norm_ffn1", "norm_mix", "norm_ffn2"):
        wp[name] = w[name]
    return wp


def _shift_state_to_cols(proj_last):
    return jnp.concatenate([proj_last[..., :3 * RWKV_DIM],
                            proj_last[..., COL_SMALL:COL_SMALL + RWKV_PROJ - 3 * RWKV_DIM]], axis=-1)


def _tile(m, pref):
    t = min(m, pref)
    while m % t:
        t //= 2
    return t


def _trunk(x3, states, wp, norm_final):
    batch, seqlen, _ = x3.shape
    m = batch * seqlen
    x = x3.reshape(m, D_MODEL)
    depth = wp["w_all"].shape[0]
    tm = _tile(m, 512)
    new = ([], [], [], [], [])
    for layer in range(depth):
        shift, wkv, sc_buf, ssm_buf, ssm = (s[layer] for s in states)
        shift_rkv, shift_sm = _reorder_cols(shift)
        shift_rkv = shift_rkv.reshape(batch, 1, -1)
        shift_sm = shift_sm.reshape(batch, 1, -1)
        x = _ffn(x, wp["norm_ffn1"], wp["ffn1_w_in"], wp["ffn1_w_out"], layer, tm=tm, tf=1408)
        proj = _proj_in(x, wp["norm_mix"], wp["w_all"], layer, tm=_tile(m, 1024), tn=PROJ_COLS // 4)
        if seqlen == 1:
            y_rwkv, y_sc, y_ssm, wkv_n, sc_n, ssmbuf_n, ssm_n = _mixers_step(
                proj, (shift_rkv, shift_sm, wkv, sc_buf, ssm_buf, ssm), wp, layer, batch=batch)
        else:
            y_rwkv, wkv_n = _rwkv_prompt(proj, shift_rkv, shift_sm, wkv, wp, layer, batch=batch, seqlen=seqlen)
            y_sc, sc_n = _sc_prompt(proj, sc_buf, wp, layer, batch=batch, seqlen=seqlen, tl=_tile(seqlen, 256))
            y_ssm, ssmbuf_n, ssm_n = _ssd_prompt(proj, ssm_buf, ssm, wp, layer, batch=batch, seqlen=seqlen)
        shift_n = _shift_state_to_cols(proj.reshape(batch, seqlen, PROJ_COLS)[:, -1])
        x = _proj_out(x, y_rwkv, y_sc, y_ssm, wp["w_out"], layer, tm=tm)
        x = _ffn(x, wp["norm_ffn2"], wp["ffn2_w_in"], wp["ffn2_w_out"], layer,
                 final_g=norm_final if layer == depth - 1 else None, tm=tm, tf=1408)
        for lst, s in zip(new, (shift_n, wkv_n, sc_n, ssmbuf_n, ssm_n)):
            lst.append(s)
    return x.reshape(batch, seqlen, D_MODEL), tuple(jnp.stack(lst) for lst in new)


def kernel(x_prompt, x_sample, state_rwkv_shift, state_rwkv_wkv, state_sc_buf, state_ssm_conv, state_ssm,
           norm_ffn1, ffn1_w_in, ffn1_w_out, norm_mix, w_in, rwkv_mu, rwkv_w0, rwkv_w2, rwkv_a0, rwkv_a2,
           rwkv_g2, rwkv_k_k, rwkv_k_a, rwkv_r_k, rwkv_ln_w, rwkv_ln_b, sc_conv_w, sc_norm, ssm_conv_w,
           ssm_conv_b, ssm_dt_bias, ssm_A_log, ssm_D, ssm_norm, w_out, norm_ffn2, ffn2_w_in, ffn2_w_out,
           norm_final):
    weights = dict(
        norm_ffn1=norm_ffn1, ffn1_w_in=ffn1_w_in, ffn1_w_out=ffn1_w_out, norm_mix=norm_mix, w_in=w_in,
        rwkv_mu=rwkv_mu, rwkv_w0=rwkv_w0, rwkv_w2=rwkv_w2, rwkv_a0=rwkv_a0, rwkv_a2=rwkv_a2, rwkv_g2=rwkv_g2,
        rwkv_k_k=rwkv_k_k, rwkv_k_a=rwkv_k_a, rwkv_r_k=rwkv_r_k, rwkv_ln_w=rwkv_ln_w, rwkv_ln_b=rwkv_ln_b,
        sc_conv_w=sc_conv_w, sc_norm=sc_norm, ssm_conv_w=ssm_conv_w, ssm_conv_b=ssm_conv_b,
        ssm_dt_bias=ssm_dt_bias, ssm_A_log=ssm_A_log, ssm_D=ssm_D, ssm_norm=ssm_norm, w_out=w_out,
        norm_ffn2=norm_ffn2, ffn2_w_in=ffn2_w_in, ffn2_w_out=ffn2_w_out)
    wp = _prepare_weights(weights)
    depth = w_in.shape[0]
    nb, dt_ = x_prompt.shape[0], x_prompt.dtype
    zero_states = (
        jnp.zeros((depth, nb, RWKV_PROJ), dt_),
        jnp.zeros((depth, nb, RWKV_HEADS, RWKV_HEAD, RWKV_HEAD), dt_),
        jnp.zeros((depth, nb, SC_WIDTH - 1, SC_DIM), dt_),
        jnp.zeros((depth, nb, SSM_CONV - 1, SSM_CONV_DIM), dt_),
        jnp.zeros((depth, nb, SSM_HEADS, SSM_HEADDIM, SSM_STATE), dt_),
    )
    y_prompt, p_states = _trunk(x_prompt, zero_states, wp, norm_final)
    sample_states = (state_rwkv_shift, state_rwkv_wkv, state_sc_buf, state_ssm_conv, state_ssm)
    y_sample, s_states = _trunk(x_sample, sample_states, wp, norm_final)
    return (y_prompt, y_sample) + tuple(p_states) + tuple(s_states)
```

```python
import functools

import jax
import jax.numpy as jnp
from jax import lax
from jax.experimental import pallas as pl
from jax.experimental.pallas import tpu as pltpu

F32 = jnp.float32
MXU_DTYPE = jnp.bfloat16

D_MODEL = 1024
D_FF = 2816
RWKV_DIM = 1024
RWKV_HEAD = 64
RWKV_HEADS = 16
DECAY_LORA = 64
AAA_LORA = 64
GATE_LORA = 160
RWKV_PROJ = 3 * RWKV_DIM + DECAY_LORA + AAA_LORA + GATE_LORA
RWKV_GN_EPS = 64e-5
DECAY_SCALE = 0.6065306597126334
SC_DIM = 512
SC_WIDTH = 3
SSM_DIM = 512
SSM_HEADDIM = 64
SSM_HEADS = 8
SSM_GROUPS = 2
SSM_STATE = 128
SSM_CONV = 4
SSM_CONV_DIM = SSM_DIM + 2 * SSM_GROUPS * SSM_STATE
NORM_EPS = 1e-6

COL_RKV = 0
COL_SC = 3 * RWKV_DIM
COL_Z = COL_SC + 3 * SC_DIM
COL_XBC = COL_Z + SSM_DIM
COL_SMALL = COL_XBC + SSM_CONV_DIM
SMALL_W = 512
SM_GL = 128
SM_DT = 384
PROJ_COLS = COL_SMALL + SMALL_W

LANE = 128
SUBLANE = 8
VMEM_LIMIT = 56 * 1024 * 1024

RWKV_CHUNK = 64
RWKV_CHUNKS_PER_STEP = 4
INV_BASE = 8
STEP_SEQS = 8
SSD_CHUNK = 128
SSD_CHUNKS_PER_STEP = 1


def _mm(a, b):
    return jnp.dot(a.astype(MXU_DTYPE), b.astype(MXU_DTYPE), preferred_element_type=F32)


def _mm_nt(a, b):
    return lax.dot_general(a.astype(MXU_DTYPE), b.astype(MXU_DTYPE), (((1,), (1,)), ((), ())),
                           preferred_element_type=F32)


def _mm_tn(a, b):
    return lax.dot_general(a.astype(MXU_DTYPE), b.astype(MXU_DTYPE), (((0,), (0,)), ((), ())),
                           preferred_element_type=F32)


def _sigmoid(x):
    return 1.0 / (1.0 + jnp.exp(-x))


def _silu(x):
    return x * _sigmoid(x)


def _softplus(x):
    return jnp.maximum(x, 0.0) + jnp.log1p(jnp.exp(-jnp.abs(x)))


def _rmsnorm(x, g):
    return x * lax.rsqrt(jnp.mean(x * x, axis=-1, keepdims=True) + NORM_EPS) * g


def _params(*sem):
    return pltpu.CompilerParams(dimension_semantics=sem, vmem_limit_bytes=VMEM_LIMIT)


def _ffn_kernel(x_ref, g_ref, wg_ref, wu_ref, wo_ref, gf_ref, o_ref, h_ref, acc_ref, *, final_norm):
    j = pl.program_id(1)

    @pl.when(j == 0)
    def _():
        h_ref[...] = _rmsnorm(x_ref[...], g_ref[...]).astype(h_ref.dtype)
        acc_ref[...] = jnp.zeros_like(acc_ref)

    h = h_ref[...]
    gate = jnp.dot(h, wg_ref[...], preferred_element_type=F32)
    up = jnp.dot(h, wu_ref[...], preferred_element_type=F32)
    act = (_silu(gate) * up).astype(MXU_DTYPE)
    acc_ref[...] += jnp.dot(act, wo_ref[...], preferred_element_type=F32)

    @pl.when(j == pl.num_programs(1) - 1)
    def _():
        y = x_ref[...] + 0.5 * acc_ref[...]
        if final_norm:
            y = _rmsnorm(y, gf_ref[...])
        o_ref[...] = y


def _ffn(x, norm_g, w_in, w_out, layer, final_g=None, *, tm, tf):
    m = x.shape[0]
    nf = D_FF // tf
    final_norm = final_g is not None
    gf = final_g if final_norm else norm_g[layer]
    return pl.pallas_call(
        functools.partial(_ffn_kernel, final_norm=final_norm),
        grid=(m // tm, nf),
        in_specs=[
            pl.BlockSpec((tm, D_MODEL), lambda i, j: (i, 0)),
            pl.BlockSpec((None, 1, D_MODEL), lambda i, j: (layer, 0, 0)),
            pl.BlockSpec((None, D_MODEL, tf), lambda i, j: (layer, 0, j)),
            pl.BlockSpec((None, D_MODEL, tf), lambda i, j: (layer, 0, nf + j)),
            pl.BlockSpec((None, tf, D_MODEL), lambda i, j: (layer, j, 0)),
            pl.BlockSpec((1, D_MODEL), lambda i, j: (0, 0)),
        ],
        out_specs=pl.BlockSpec((tm, D_MODEL), lambda i, j: (i, 0)),
        out_shape=jax.ShapeDtypeStruct((m, D_MODEL), F32),
        scratch_shapes=[pltpu.VMEM((tm, D_MODEL), MXU_DTYPE), pltpu.VMEM((tm, D_MODEL), F32)],
        compiler_params=_params("parallel", "arbitrary"),
        name="ffn",
    )(x, norm_g.reshape(-1, 1, D_MODEL), w_in, w_in, w_out, gf.reshape(1, D_MODEL))


def _proj_kernel(x_ref, g_ref, w_ref, o_ref, h_ref):
    @pl.when(pl.program_id(1) == 0)
    def _():
        h_ref[...] = _rmsnorm(x_ref[...], g_ref[...]).astype(h_ref.dtype)

    o_ref[...] = jnp.dot(h_ref[...], w_ref[...], preferred_element_type=F32)


def _proj_in(x, norm_g, w_all, layer, *, tm, tn):
    m = x.shape[0]
    return pl.pallas_call(
        _proj_kernel,
        grid=(m // tm, PROJ_COLS // tn),
        in_specs=[
            pl.BlockSpec((tm, D_MODEL), lambda i, j: (i, 0)),
            pl.BlockSpec((None, 1, D_MODEL), lambda i, j: (layer, 0, 0)),
            pl.BlockSpec((None, D_MODEL, tn), lambda i, j: (layer, 0, j)),
        ],
        out_specs=pl.BlockSpec((tm, tn), lambda i, j: (i, j)),
        out_shape=jax.ShapeDtypeStruct((m, PROJ_COLS), F32),
        scratch_shapes=[pltpu.VMEM((tm, D_MODEL), MXU_DTYPE)],
        compiler_params=_params("parallel", "arbitrary"),
        name="proj_in",
    )(x, norm_g.reshape(-1, 1, D_MODEL), w_all)


def _proj_out_kernel(x_ref, y1_ref, y2_ref, y3_ref, w1_ref, w2_ref, w3_ref, o_ref):
    acc = jnp.dot(y1_ref[...].astype(MXU_DTYPE), w1_ref[...], preferred_element_type=F32)
    acc += jnp.dot(y2_ref[...].astype(MXU_DTYPE), w2_ref[...], preferred_element_type=F32)
    acc += jnp.dot(y3_ref[...].astype(MXU_DTYPE), w3_ref[...], preferred_element_type=F32)
    o_ref[...] = x_ref[...] + acc


def _proj_out(x, y_rwkv, y_sc, y_ssm, w_out, layer, *, tm):
    m = x.shape[0]
    return pl.pallas_call(
        _proj_out_kernel,
        grid=(m // tm,),
        in_specs=[
            pl.BlockSpec((tm, D_MODEL), lambda i: (i, 0)),
            pl.BlockSpec((tm, RWKV_DIM), lambda i: (i, 0)),
            pl.BlockSpec((tm, SC_DIM), lambda i: (i, 0)),
            pl.BlockSpec((tm, SSM_DIM), lambda i: (i, 0)),
            pl.BlockSpec((None, RWKV_DIM, D_MODEL), lambda i: (layer, 0, 0)),
            pl.BlockSpec((None, SC_DIM, D_MODEL), lambda i: (layer, RWKV_DIM // SC_DIM, 0)),
            pl.BlockSpec((None, SSM_DIM, D_MODEL), lambda i: (layer, (RWKV_DIM + SC_DIM) // SSM_DIM, 0)),
        ],
        out_specs=pl.BlockSpec((tm, D_MODEL), lambda i: (i, 0)),
        out_shape=jax.ShapeDtypeStruct((m, D_MODEL), F32),
        compiler_params=_params("parallel"),
        name="proj_out",
    )(x, y_rwkv, y_sc, y_ssm, w_out, w_out, w_out)


def _previous_rows(x, carry_row):
    first = lax.broadcasted_iota(jnp.int32, x.shape, 0) == 0
    return jnp.where(first, carry_row, pltpu.roll(x, 1, 0))


def _rows_shifted(x, tail, s):
    rolled = pltpu.roll(x, s, 0)
    first = lax.broadcasted_iota(jnp.int32, tail.shape, 0) < s
    head = jnp.where(first, pltpu.roll(tail, s, 0), rolled[:SUBLANE])
    return jnp.concatenate([head, rolled[SUBLANE:]], axis=0)


def _rwkv_prepare(p_rkv, prev_rkv, p_sm, prev_sm, wts):
    (mu_rkv, mu_sm, w0, w2, a0, a2, g2, k_k, k_a) = wts
    xs = p_rkv + (prev_rkv - p_rkv) * mu_rkv
    xm = p_sm + (prev_sm - p_sm) * mu_sm
    r = xs[:, :RWKV_DIM]
    k = xs[:, RWKV_DIM:2 * RWKV_DIM]
    v = xs[:, 2 * RWKV_DIM:]
    w_lr = xm[:, :DECAY_LORA]
    a_lr = xm[:, DECAY_LORA:DECAY_LORA + AAA_LORA]
    g_lr = xm[:, SM_GL:SM_GL + GATE_LORA]
    log_decay = -DECAY_SCALE * _sigmoid(w0 + _mm(jnp.tanh(w_lr), w2))
    a = _sigmoid(a0 + _mm(a_lr, a2))
    g = _mm(_sigmoid(g_lr), g2)
    kk = k * k_k
    k_mod = k * (1.0 + (a - 1.0) * k_a)
    return r, log_decay, k_mod, v, kk, a, g


def _rwkv_weight_specs(layer):
    def vec(width):
        return pl.BlockSpec((None, 1, width), lambda *idx: (layer, 0, 0))

    def mat(rows):
        return pl.BlockSpec((None, rows, RWKV_DIM), lambda *idx: (layer, 0, 0))

    return [vec(3 * RWKV_DIM), vec(SMALL_W), vec(RWKV_DIM), mat(DECAY_LORA), vec(RWKV_DIM), mat(AAA_LORA),
            mat(GATE_LORA), vec(RWKV_DIM), vec(RWKV_DIM), vec(RWKV_DIM), vec(RWKV_DIM), vec(RWKV_DIM)]


def _rwkv_weight_args(wp):
    return (wp["mu_rkv"], wp["mu_sm"], wp["rwkv_w0"], wp["rwkv_w2"], wp["rwkv_a0"], wp["rwkv_a2"],
            wp["rwkv_g2"], wp["rwkv_k_k"], wp["rwkv_k_a"], wp["rwkv_r_k"], wp["rwkv_ln_w"], wp["rwkv_ln_b"])


HEADS_PER_GROUP = 4
GROUP_W = HEADS_PER_GROUP * RWKV_HEAD
RWKV_GROUPS = RWKV_HEADS // HEADS_PER_GROUP


def _split3(x):
    hi = x.astype(MXU_DTYPE)
    r1 = x - hi.astype(F32)
    mid = r1.astype(MXU_DTYPE)
    lo = (r1 - mid.astype(F32)).astype(MXU_DTYPE)
    return hi, mid, lo


def _block_diag(x, head_masks):
    return jnp.concatenate([x * m for m in head_masks], axis=0)


def _rwkv_chunk_kernel(p_rkv_ref, p_sm_ref, sh_rkv_ref, sh_sm_ref, s0_ref,
                       mu_rkv_ref, mu_sm_ref, w0_ref, w2_ref, a0_ref, a2_ref, g2_ref, kk_ref, ka_ref,
                       rk_ref, lnw_ref, lnb_ref,
                       y_ref, s_out_ref,
                       pad_rkv, pad_sm, state, at_s, rt_s, bt_s, kt_s, bh_s, kh_s, v_s, yn_s,
                       hm_s, ones_s, tri_s):
    c = RWKV_CHUNK
    nc = RWKV_CHUNKS_PER_STEP
    tl = nc * c
    n = RWKV_HEAD
    gw = GROUP_W
    t = pl.program_id(1)

    @pl.when(t == 0)
    def _():
        pad_rkv[...] = sh_rkv_ref[0]
        pad_sm[...] = sh_sm_ref[0]
        for h in range(RWKV_HEADS):
            state[:, h * n:(h + 1) * n] = s0_ref[0, h]
        lane_head = lax.broadcasted_iota(jnp.int32, (c, gw), 1) // n
        for hh in range(HEADS_PER_GROUP):
            hm_s[hh] = (lane_head == hh).astype(MXU_DTYPE)
        sq_r = lax.broadcasted_iota(jnp.int32, (gw, gw), 0) // n
        sq_c = lax.broadcasted_iota(jnp.int32, (gw, gw), 1) // n
        ones_s[...] = (sq_r == sq_c).astype(MXU_DTYPE)
        tri_r = lax.broadcasted_iota(jnp.int32, (tl, tl), 0)
        tri_c = lax.broadcasted_iota(jnp.int32, (tl, tl), 1)
        tri_s[...] = ((tri_r >= tri_c) & (tri_r // c == tri_c // c)).astype(MXU_DTYPE)

    wts = (mu_rkv_ref[...], mu_sm_ref[...], w0_ref[...], w2_ref[...], a0_ref[...], a2_ref[...],
           g2_ref[...], kk_ref[...], ka_ref[...])
    r, log_decay, k_mod, v, kk, a, g = _rwkv_prepare(
        p_rkv_ref[...], _previous_rows(p_rkv_ref[...], pad_rkv[...]),
        p_sm_ref[...], _previous_rows(p_sm_ref[...], pad_sm[...]), wts)
    pad_rkv[...] = p_rkv_ref[tl - 1:tl, :]
    pad_sm[...] = p_sm_ref[tl - 1:tl, :]

    row = lax.broadcasted_iota(jnp.int32, (c, gw), 0)
    lane = lax.broadcasted_iota(jnp.int32, (c, gw), 1)
    pos = lane % n
    strict = row > pos
    incl = row >= pos
    eye = (row == pos).astype(F32)
    head_masks = [hm_s[hh] for hh in range(HEADS_PER_GROUP)]
    ones_blk = ones_s[...]
    tri = tri_s[...]

    def head_sum(x):
        return jnp.concatenate(
            [jnp.dot(x[:, gi * gw:(gi + 1) * gw].astype(MXU_DTYPE), ones_blk, preferred_element_type=F32)
             for gi in range(RWKV_GROUPS)], axis=1)

    hi, mid, lo = _split3(log_decay)
    cum = (jnp.dot(tri, hi, preferred_element_type=F32) + jnp.dot(tri, mid, preferred_element_type=F32)
           + jnp.dot(tri, lo, preferred_element_type=F32))
    cum_end = jnp.concatenate([jnp.broadcast_to(cum[(ci + 1) * c - 1:(ci + 1) * c, :], (c, RWKV_DIM))
                               for ci in range(nc)], axis=0)
    e_out = jnp.exp(-cum)
    e_end = jnp.exp(cum_end - cum)
    kk = kk * lax.rsqrt(jnp.maximum(head_sum(kk * kk), 1e-24))
    b = kk * a
    at_s[...] = (-kk * jnp.exp(cum - log_decay)).astype(MXU_DTYPE)
    rt_s[...] = (r * jnp.exp(cum)).astype(MXU_DTYPE)
    bt_s[...] = (b * e_out).astype(MXU_DTYPE)
    kt_s[...] = (k_mod * e_out).astype(MXU_DTYPE)
    bh_s[...] = (b * e_end).astype(MXU_DTYPE)
    kh_s[...] = (k_mod * e_end).astype(MXU_DTYPE)
    v_s[...] = v
    bonus = head_sum(r * k_mod * rk_ref[...])
    w_end = jnp.exp(cum_end)

    def mm(x, y):
        return jnp.dot(x.astype(MXU_DTYPE), y, preferred_element_type=F32)

    def bd(x):
        return _block_diag(x.astype(MXU_DTYPE), head_masks)

    groups = range(RWKV_GROUPS)
    chunks = range(nc)
    pairs = [(ci, gi) for ci in chunks for gi in groups]
    rws = [slice(ci * c, (ci + 1) * c) for ci in chunks]
    sls = [slice(gi * gw, (gi + 1) * gw) for gi in groups]
    nt = (((1,), (1,)), ((), ()))
    ar = {(ci, gi): jnp.concatenate([at_s[rws[ci], sls[gi]], rt_s[rws[ci], sls[gi]]], axis=0)
          for ci, gi in pairs}
    vg = {(ci, gi): v_s[rws[ci], sls[gi]].astype(MXU_DTYPE) for ci, gi in pairs}
    gram_b = {(ci, gi): lax.dot_general(ar[ci, gi], bd(bt_s[rws[ci], sls[gi]]), nt, preferred_element_type=F32)
              for ci, gi in pairs}
    gram_k = {(ci, gi): lax.dot_general(ar[ci, gi], bd(kt_s[rws[ci], sls[gi]]), nt, preferred_element_type=F32)
              for ci, gi in pairs}
    l_ab = {p: jnp.where(strict, gram_b[p][:c], 0.0) for p in pairs}
    m_rb = {p: jnp.where(incl, gram_b[p][c:], 0.0) for p in pairs}
    l_akrk = {p: jnp.where(jnp.concatenate([strict, incl], axis=0), gram_k[p], 0.0) for p in pairs}
    def same_block(width):
        return row // width == pos // width

    base = {p: jnp.where(same_block(INV_BASE), l_ab[p], 0.0) for p in pairs}
    inv = {p: eye + base[p] for p in pairs}
    power = base
    power_bd = {p: bd(power[p]) for p in pairs}
    span = 2
    while span < INV_BASE:
        power = {p: mm(power[p], power_bd[p]) for p in pairs}
        power_bd = {p: bd(power[p]) for p in pairs}
        inv = {p: inv[p] + mm(inv[p], power_bd[p]) for p in pairs}
        span *= 2
    width = 2 * INV_BASE
    while width <= c:
        below = same_block(width) & jnp.logical_not(same_block(width // 2))
        sub = {p: mm(inv[p], bd(jnp.where(below, l_ab[p], 0.0))) for p in pairs}
        inv = {p: inv[p] + mm(sub[p], bd(inv[p])) for p in pairs}
        width *= 2
    y = {}
    for ci in chunks:
        s0 = [state[:, sl] for sl in sls]
        z = [mm(l_akrk[ci, gi], bd(vg[ci, gi]))
             + lax.dot_general(ar[ci, gi], bd(s0[gi]), nt, preferred_element_type=F32) for gi in groups]
        u = [mm(inv[ci, gi], bd(z[gi][:c])) for gi in groups]
        for gi in groups:
            y[ci, gi] = z[gi][c:] + mm(m_rb[ci, gi], bd(u[gi]))
        for gi in groups:
            uv = jnp.concatenate([u[gi].astype(MXU_DTYPE), vg[ci, gi]], axis=0)
            bk = jnp.concatenate([bh_s[rws[ci], sls[gi]], kh_s[rws[ci], sls[gi]]], axis=0)
            upd = lax.dot_general(uv, bk, (((0,), (0,)), ((), ())), preferred_element_type=F32)
            s_new = s0[gi] * w_end[ci * c:ci * c + 1, sls[gi]]
            for hh in range(HEADS_PER_GROUP):
                s_new += upd[hh * n:(hh + 1) * n, :] * head_masks[hh].astype(F32)
            state[:, sls[gi]] = s_new
    mean = {p: mm(y[p], ones_blk) * (1.0 / n) for p in pairs}
    yc = {p: y[p] - mean[p] for p in pairs}
    var = {p: mm(yc[p] * yc[p], ones_blk) * (1.0 / n) for p in pairs}
    for ci, gi in pairs:
        yn_s[rws[ci], sls[gi]] = yc[ci, gi] * lax.rsqrt(var[ci, gi] + RWKV_GN_EPS)

    y_ref[...] = (yn_s[...] * lnw_ref[...] + lnb_ref[...] + bonus * v_s[...]) * g

    @pl.when(t == pl.num_programs(1) - 1)
    def _():
        for h in range(RWKV_HEADS):
            s_out_ref[0, h] = state[:, h * n:(h + 1) * n]


def _rwkv_prompt(proj, shift_rkv, shift_sm, wkv0, wp, layer, *, batch, seqlen):
    c = RWKV_CHUNK * RWKV_CHUNKS_PER_STEP
    nt = seqlen // c
    return pl.pallas_call(
        _rwkv_chunk_kernel,
        grid=(batch, nt),
        in_specs=[
            pl.BlockSpec((c, 3 * RWKV_DIM), lambda b, t: (b * nt + t, COL_RKV // (3 * RWKV_DIM))),
            pl.BlockSpec((c, SMALL_W), lambda b, t: (b * nt + t, COL_SMALL // SMALL_W)),
            pl.BlockSpec((None, 1, 1, 3 * RWKV_DIM), lambda b, t: (layer, b, 0, 0)),
            pl.BlockSpec((None, 1, 1, SMALL_W), lambda b, t: (layer, b, 0, 0)),
            pl.BlockSpec((None, 1, RWKV_HEADS, RWKV_HEAD, RWKV_HEAD), lambda b, t: (layer, b, 0, 0, 0)),
        ] + _rwkv_weight_specs(layer),
        out_specs=[
            pl.BlockSpec((c, RWKV_DIM), lambda b, t: (b * nt + t, 0)),
            pl.BlockSpec((1, RWKV_HEADS, RWKV_HEAD, RWKV_HEAD), lambda b, t: (b, 0, 0, 0)),
        ],
        out_shape=[
            jax.ShapeDtypeStruct((batch * seqlen, RWKV_DIM), F32),
            jax.ShapeDtypeStruct((batch, RWKV_HEADS, RWKV_HEAD, RWKV_HEAD), F32),
        ],
        scratch_shapes=[
            pltpu.VMEM((1, 3 * RWKV_DIM), F32),
            pltpu.VMEM((1, SMALL_W), F32),
            pltpu.VMEM((RWKV_HEAD, RWKV_DIM), F32),
        ] + [pltpu.VMEM((c, RWKV_DIM), MXU_DTYPE) for _ in range(6)]
          + [pltpu.VMEM((c, RWKV_DIM), F32) for _ in range(2)]
          + [pltpu.VMEM((HEADS_PER_GROUP, RWKV_CHUNK, GROUP_W), MXU_DTYPE),
             pltpu.VMEM((GROUP_W, GROUP_W), MXU_DTYPE), pltpu.VMEM((c, c), MXU_DTYPE)],
        compiler_params=_params("parallel", "arbitrary"),
        name="rwkv_chunk",
    )(proj, proj, shift_rkv, shift_sm, wkv0, *_rwkv_weight_args(wp))


def _sc_kernel(p_ref, buf_ref, w_ref, g_ref, y_ref, buf_out_ref, pad, *, tl):
    t = pl.program_id(1)
    hist = SC_WIDTH - 1

    @pl.when(t == 0)
    def _():
        pad[...] = jnp.zeros_like(pad)
        pad[SUBLANE - hist:SUBLANE, :] = buf_ref[0]

    b_gate = p_ref[:, :SC_DIM]
    u = p_ref[:, SC_DIM:2 * SC_DIM] * p_ref[:, 2 * SC_DIM:]
    tail = pad[...]
    conv = u * w_ref[hist:hist + 1, :]
    for j in range(hist):
        conv += _rows_shifted(u, tail, hist - j) * w_ref[j:j + 1, :]
    y_ref[...] = _rmsnorm(b_gate * conv, g_ref[...])
    pad[...] = u[tl - SUBLANE:, :]

    @pl.when(t == pl.num_programs(1) - 1)
    def _():
        buf_out_ref[0] = pad[SUBLANE - hist:SUBLANE, :]


def _sc_prompt(proj, buf0, wp, layer, *, batch, seqlen, tl):
    nt = seqlen // tl
    hist = SC_WIDTH - 1
    return pl.pallas_call(
        functools.partial(_sc_kernel, tl=tl),
        grid=(batch, nt),
        in_specs=[
            pl.BlockSpec((tl, 3 * SC_DIM), lambda b, t: (b * nt + t, COL_SC // (3 * SC_DIM))),
            pl.BlockSpec((None, 1, hist, SC_DIM), lambda b, t: (layer, b, 0, 0)),
            pl.BlockSpec((None, SC_WIDTH, SC_DIM), lambda b, t: (layer, 0, 0)),
            pl.BlockSpec((None, 1, SC_DIM), lambda b, t: (layer, 0, 0)),
        ],
        out_specs=[
            pl.BlockSpec((tl, SC_DIM), lambda b, t: (b * nt + t, 0)),
            pl.BlockSpec((1, hist, SC_DIM), lambda b, t: (b, 0, 0)),
        ],
        out_shape=[
            jax.ShapeDtypeStruct((batch * seqlen, SC_DIM), F32),
            jax.ShapeDtypeStruct((batch, hist, SC_DIM), F32),
        ],
        scratch_shapes=[pltpu.VMEM((SUBLANE, SC_DIM), F32)],
        compiler_params=_params("parallel", "arbitrary"),
        name="short_conv",
    )(proj, buf0, wp["sc_conv_w"], wp["sc_norm"])


def _ssd_kernel(z_ref, xbc_ref, dt_ref, buf_ref, h0_ref, cw_ref, cb_ref, dtb_ref, alog_ref, dskip_ref, g_ref,
                y_ref, buf_out_ref, h_out_ref, pad, state, y_s, exh_s, exs_s):
    q = SSD_CHUNK
    nq = SSD_CHUNKS_PER_STEP
    tq = nq * q
    t = pl.program_id(1)
    hist = SSM_CONV - 1
    gw = SSM_DIM // SSM_GROUPS

    @pl.when(t == 0)
    def _():
        pad[...] = jnp.zeros_like(pad)
        pad[SUBLANE - hist:SUBLANE, :] = buf_ref[0]
        state[...] = h0_ref[0]
        head_row = lax.broadcasted_iota(jnp.int32, (LANE, SSM_DIM), 0)
        exh_s[...] = (lax.broadcasted_iota(jnp.int32, (LANE, SSM_DIM), 1) // SSM_HEADDIM == head_row).astype(MXU_DTYPE)
        seg_row = lax.broadcasted_iota(jnp.int32, (LANE, SSM_HEADS * q), 0)
        exs_s[...] = (lax.broadcasted_iota(jnp.int32, (LANE, SSM_HEADS * q), 1) // q == seg_row).astype(MXU_DTYPE)

    xbc = xbc_ref[...]
    tail = pad[...]
    conv = xbc * cw_ref[hist:hist + 1, :]
    for j in range(hist):
        conv += _rows_shifted(xbc, tail, hist - j) * cw_ref[j:j + 1, :]
    pad[...] = xbc[tq - SUBLANE:, :]
    xc = _silu(conv + cb_ref[...])
    xh = xc[:, :SSM_DIM]

    dt = _softplus(dt_ref[...] + dtb_ref[...])
    neg_a = -jnp.exp(alog_ref[...])
    row = lax.broadcasted_iota(jnp.int32, (tq, tq), 0)
    col = lax.broadcasted_iota(jnp.int32, (tq, tq), 1)
    tri = ((row >= col) & (row // q == col // q)).astype(MXU_DTYPE)
    hi, mid, lo = _split3(dt * neg_a)
    cum = (jnp.dot(tri, hi, preferred_element_type=F32) + jnp.dot(tri, mid, preferred_element_type=F32)
           + jnp.dot(tri, lo, preferred_element_type=F32))
    cum_t = cum.T
    cum_last = jnp.concatenate([jnp.broadcast_to(cum[(ci + 1) * q - 1:(ci + 1) * q, :], (q, LANE))
                                for ci in range(nq)], axis=0)

    def expand(x, e):
        h3 = _split3(x)
        return sum(jnp.dot(part, e, preferred_element_type=F32) for part in h3)

    ex_head = exh_s[...]
    ex_seg = exs_s[...]
    xdt_all = xh * expand(dt, ex_head)
    xdt_end = xdt_all * expand(jnp.exp(cum_last - cum), ex_head)
    ecum_x = expand(jnp.exp(cum), ex_head)
    cum_seg = expand(cum, ex_seg)
    causal = lax.broadcasted_iota(jnp.int32, (q, q), 0) >= lax.broadcasted_iota(jnp.int32, (q, q), 1)
    heads_per_group = SSM_HEADS // SSM_GROUPS

    heads = range(SSM_HEADS)
    chunks = range(nq)
    pairs = [(ci, h) for ci in chunks for h in heads]
    grp_of = [h // heads_per_group for h in heads]
    rws = [slice(ci * q, (ci + 1) * q) for ci in chunks]
    sls = [slice(h * SSM_HEADDIM, (h + 1) * SSM_HEADDIM) for h in heads]
    bm = {(ci, g): xc[rws[ci], SSM_DIM + g * SSM_STATE:SSM_DIM + (g + 1) * SSM_STATE].astype(MXU_DTYPE)
          for ci in chunks for g in range(SSM_GROUPS)}
    cm = {(ci, g): xc[rws[ci], SSM_DIM + (SSM_GROUPS + g) * SSM_STATE:SSM_DIM + (SSM_GROUPS + g + 1) * SSM_STATE]
          .astype(MXU_DTYPE) for ci in chunks for g in range(SSM_GROUPS)}
    gram = {k: _mm_nt(cm[k], bm[k]) for k in bm}
    cum_end = {(ci, h): cum[(ci + 1) * q - 1:(ci + 1) * q, h:h + 1] for ci, h in pairs}
    seg = {(ci, h): cum_seg[rws[ci], h * q:(h + 1) * q] - cum_t[h:h + 1, rws[ci]] for ci, h in pairs}
    decay = {p: jnp.where(causal, jnp.exp(jnp.where(causal, seg[p], 0.0)), 0.0) for p in pairs}
    xdt = {(ci, h): xdt_all[rws[ci], sls[h]] for ci, h in pairs}
    y = {(ci, h): _mm(gram[ci, grp_of[h]] * decay[ci, h], xdt[ci, h]) for ci, h in pairs}
    upd = {(ci, h): _mm_tn(xdt_end[rws[ci], sls[h]], bm[ci, grp_of[h]]) for ci, h in pairs}
    for ci in chunks:
        y_off = [_mm_nt(cm[ci, grp_of[h]], state[h]) * ecum_x[rws[ci], sls[h]] for h in heads]
        for h in heads:
            state[h] = state[h] * jnp.exp(cum_end[ci, h]) + upd[ci, h]
            y_s[rws[ci], sls[h]] = y[ci, h] + y_off[h]

    y = (y_s[...] + expand(dskip_ref[...], ex_head) * xh) * _silu(z_ref[...])
    for grp in range(SSM_GROUPS):
        yg = y[:, grp * gw:(grp + 1) * gw]
        y_ref[:, grp * gw:(grp + 1) * gw] = _rmsnorm(yg, g_ref[:, grp * gw:(grp + 1) * gw])

    @pl.when(t == pl.num_programs(1) - 1)
    def _():
        buf_out_ref[0] = pad[SUBLANE - hist:SUBLANE, :]
        h_out_ref[0] = state[...]


def _ssd_prompt(proj, buf0, h0, wp, layer, *, batch, seqlen):
    q = SSD_CHUNK * SSD_CHUNKS_PER_STEP
    nt = seqlen // q
    hist = SSM_CONV - 1
    lane_vec = pl.BlockSpec((None, 1, LANE), lambda b, t: (layer, 0, 0))
    return pl.pallas_call(
        _ssd_kernel,
        grid=(batch, nt),
        in_specs=[
            pl.BlockSpec((q, SSM_DIM), lambda b, t: (b * nt + t, COL_Z // SSM_DIM)),
            pl.BlockSpec((q, SSM_CONV_DIM), lambda b, t: (b * nt + t, COL_XBC // SSM_CONV_DIM)),
            pl.BlockSpec((q, LANE), lambda b, t: (b * nt + t, (COL_SMALL + SM_DT) // LANE)),
            pl.BlockSpec((None, 1, hist, SSM_CONV_DIM), lambda b, t: (layer, b, 0, 0)),
            pl.BlockSpec((None, 1, SSM_HEADS, SSM_HEADDIM, SSM_STATE), lambda b, t: (layer, b, 0, 0, 0)),
            pl.BlockSpec((None, SSM_CONV, SSM_CONV_DIM), lambda b, t: (layer, 0, 0)),
            pl.BlockSpec((None, 1, SSM_CONV_DIM), lambda b, t: (layer, 0, 0)),
            lane_vec, lane_vec, lane_vec,
            pl.BlockSpec((None, 1, SSM_DIM), lambda b, t: (layer, 0, 0)),
        ],
        out_specs=[
            pl.BlockSpec((q, SSM_DIM), lambda b, t: (b * nt + t, 0)),
            pl.BlockSpec((1, hist, SSM_CONV_DIM), lambda b, t: (b, 0, 0)),
            pl.BlockSpec((1, SSM_HEADS, SSM_HEADDIM, SSM_STATE), lambda b, t: (b, 0, 0, 0)),
        ],
        out_shape=[
            jax.ShapeDtypeStruct((batch * seqlen, SSM_DIM), F32),
            jax.ShapeDtypeStruct((batch, hist, SSM_CONV_DIM), F32),
            jax.ShapeDtypeStruct((batch, SSM_HEADS, SSM_HEADDIM, SSM_STATE), F32),
        ],
        scratch_shapes=[
            pltpu.VMEM((SUBLANE, SSM_CONV_DIM), F32),
            pltpu.VMEM((SSM_HEADS, SSM_HEADDIM, SSM_STATE), F32),
            pltpu.VMEM((q, SSM_DIM), F32),
            pltpu.VMEM((LANE, SSM_DIM), MXU_DTYPE),
            pltpu.VMEM((LANE, SSM_HEADS * SSD_CHUNK), MXU_DTYPE),
        ],
        compiler_params=_params("parallel", "arbitrary"),
        name="ssd_chunk",
    )(proj, proj, proj, buf0, h0, wp["ssm_conv_w"], wp["ssm_conv_b"], wp["dt_bias_pad"], wp["a_log_pad"],
      wp["d_pad"], wp["ssm_norm"])


def _lane_sums(x, ones):
    hi = x.astype(MXU_DTYPE)
    mid = (x - hi.astype(F32)).astype(MXU_DTYPE)
    return jnp.dot(hi, ones, preferred_element_type=F32) + jnp.dot(mid, ones, preferred_element_type=F32)


def _row_sums(sel, x):
    hi = x.astype(MXU_DTYPE)
    mid = (x - hi.astype(F32)).astype(MXU_DTYPE)
    return jnp.dot(sel, hi, preferred_element_type=F32) + jnp.dot(sel, mid, preferred_element_type=F32)


def _head_sums_rows(x, ones_pair):
    return jnp.concatenate([_lane_sums(x[:, q * LANE:(q + 1) * LANE], ones_pair)
                            for q in range(RWKV_DIM // LANE)], axis=1)


def _step_kernel(p_rkv_ref, p_sm_ref, p_sc_ref, z_ref, xbc_ref,
                 sh_rkv_ref, sh_sm_ref, wkv_ref, scbuf_ref, ssmbuf_ref, ssm_ref,
                 mu_rkv_ref, mu_sm_ref, w0_ref, w2_ref, a0_ref, a2_ref, g2_ref, kk_ref, ka_ref,
                 rk_ref, lnw_ref, lnb_ref,
                 scw_ref, scg_ref, cw_ref, cb_ref, dtb_ref, alog_ref, dskip_ref, ssmg_ref,
                 y_rwkv_ref, y_sc_ref, y_ssm_ref, wkv_out_ref, scbuf_out_ref, ssmbuf_out_ref, ssm_out_ref,
                 even_s, odd_s, yrow_s, yssm_s):
    n = RWKV_HEAD
    half = n // 2
    nseq = p_rkv_ref.shape[1]
    seqs = range(nseq)
    i128 = lambda shape, d: lax.broadcasted_iota(jnp.int32, shape, d)
    ones_pair = (i128((LANE, LANE), 0) // n == i128((LANE, LANE), 1) // n).astype(MXU_DTYPE)
    ones_full = jnp.ones((LANE, LANE), MXU_DTYPE)
    diag2 = i128((half, LANE), 1) % n == 2 * i128((half, LANE), 0) + i128((half, LANE), 1) // n
    diag_lo = i128((n, LANE), 1) == i128((n, LANE), 0)
    diag_hi = i128((n, LANE), 1) == i128((n, LANE), 0) + n
    upper_row = (i128((nseq, RWKV_DIM), 1) // n) % 2 == 1
    upper_lane = i128((1, LANE), 1) // n == 1

    wts = (mu_rkv_ref[...], mu_sm_ref[...], w0_ref[...], w2_ref[...], a0_ref[...], a2_ref[...],
           g2_ref[...], kk_ref[...], ka_ref[...])
    r, log_decay, k_mod, v, kk, a, g = _rwkv_prepare(p_rkv_ref[0], sh_rkv_ref[0], p_sm_ref[0], sh_sm_ref[0], wts)
    kk = kk * lax.rsqrt(jnp.maximum(_head_sums_rows(kk * kk, ones_pair), 1e-24))
    bonus = _head_sums_rows(r * k_mod * rk_ref[...], ones_pair)
    for i, x in enumerate((-kk, jnp.exp(log_decay), kk * a, k_mod, r, v)):
        even_s[i] = jnp.where(upper_row, pltpu.roll(x, n, 1), x)
        odd_s[i] = jnp.where(upper_row, x, pltpu.roll(x, RWKV_DIM - n, 1))

    def head_vec(s, h, i):
        src = odd_s if h % 2 else even_s
        return src[i, s:s + 1, (h // 2) * LANE:(h // 2 + 1) * LANE]

    heads = range(RWKV_HEADS)
    sh = [(s, h) for s in seqs for h in heads]
    rs = lambda x, s, h: x[(s * RWKV_HEADS + h) * half:(s * RWKV_HEADS + h + 1) * half]
    sa = _lane_sums(jnp.concatenate([wkv_ref[s, h] * head_vec(s, h, 0) for s, h in sh], axis=0), ones_pair)
    v_col = _lane_sums(jnp.concatenate([jnp.where(diag2, head_vec(s, h, 5), 0.0) for s, h in sh], axis=0),
                       ones_pair)
    for s, h in sh:
        wkv_out_ref[s, h] = (wkv_ref[s, h] * head_vec(s, h, 1) + rs(sa, s, h) * head_vec(s, h, 2)
                             + rs(v_col, s, h) * head_vec(s, h, 3))
    y_b = _lane_sums(jnp.concatenate([wkv_out_ref[s, h] * head_vec(s, h, 4) for s, h in sh], axis=0), ones_pair)
    y_d = jnp.concatenate([jnp.where(diag2, rs(y_b, s, h), 0.0) for s, h in sh], axis=0)
    nrow = nseq * RWKV_HEADS
    sel = (i128((nrow, nrow * half), 1) // half == i128((nrow, nrow * half), 0)).astype(MXU_DTYPE)
    y_h = _row_sums(sel, y_d)
    y_h = y_h + pltpu.roll(y_h, n, 1)
    for s in seqs:
        base = s * RWKV_HEADS
        yrow_s[s:s + 1, :] = jnp.concatenate(
            [jnp.where(upper_lane, y_h[base + 2 * q + 1:base + 2 * q + 2], y_h[base + 2 * q:base + 2 * q + 1])
             for q in range(RWKV_HEADS // 2)], axis=1)
    y = yrow_s[...]
    mean = _head_sums_rows(y, ones_pair) * (1.0 / n)
    yc = y - mean
    var = _head_sums_rows(yc * yc, ones_pair) * (1.0 / n)
    y_rwkv_ref[0] = (yc * lax.rsqrt(var + RWKV_GN_EPS) * lnw_ref[...] + lnb_ref[...] + bonus * v) * g

    p_sc = p_sc_ref[0]
    u = p_sc[:, SC_DIM:2 * SC_DIM] * p_sc[:, 2 * SC_DIM:]
    buf0 = scbuf_ref[0][:, :SC_DIM]
    buf1 = scbuf_ref[0][:, SC_DIM:]
    conv = buf0 * scw_ref[0:1, :] + buf1 * scw_ref[1:2, :] + u * scw_ref[2:3, :]
    y_sc_ref[0] = _rmsnorm(p_sc[:, :SC_DIM] * conv, scg_ref[...])
    scbuf_out_ref[0, :, :SC_DIM] = buf1
    scbuf_out_ref[0, :, SC_DIM:] = u

    xbc = xbc_ref[0]
    cbuf = ssmbuf_ref[0]
    conv = xbc * cw_ref[3:4, :]
    for j in range(SSM_CONV - 1):
        conv += cbuf[:, j * SSM_CONV_DIM:(j + 1) * SSM_CONV_DIM] * cw_ref[j:j + 1, :]
    ssmbuf_out_ref[0, :, :2 * SSM_CONV_DIM] = cbuf[:, SSM_CONV_DIM:]
    ssmbuf_out_ref[0, :, 2 * SSM_CONV_DIM:] = xbc
    xc = _silu(conv + cb_ref[...])
    xh = xc[:, :SSM_DIM]
    dt = _softplus(p_sm_ref[0][:, SM_DT:SM_DT + LANE] + dtb_ref[...])
    decay_all = jnp.exp(dt * (-jnp.exp(alog_ref[...])))
    head_of_lane = i128((nseq, SSM_DIM), 1) // SSM_HEADDIM
    dt_exp = jnp.zeros((nseq, SSM_DIM), F32)
    d_exp = jnp.zeros((nseq, SSM_DIM), F32)
    for h in range(SSM_HEADS):
        dt_exp = jnp.where(head_of_lane == h, dt[:, h:h + 1], dt_exp)
        d_exp = jnp.where(head_of_lane == h, dskip_ref[:, h:h + 1], d_exp)
    xdt = xh * dt_exp
    heads = range(SSM_HEADS)
    sh = [(s, h) for s in seqs for h in heads]
    heads_per_group = SSM_HEADS // SSM_GROUPS
    bm = lambda s, h: xc[s:s + 1, SSM_DIM + (h // heads_per_group) * SSM_STATE:
                         SSM_DIM + (h // heads_per_group + 1) * SSM_STATE]
    cm = lambda s, h: xc[s:s + 1, SSM_DIM + (SSM_GROUPS + h // heads_per_group) * SSM_STATE:
                         SSM_DIM + (SSM_GROUPS + h // heads_per_group + 1) * SSM_STATE]
    diag = lambda h: diag_hi if h % 2 else diag_lo
    slab = lambda x, s, h: x[s:s + 1, (h // 2) * LANE:(h // 2 + 1) * LANE]
    rs = lambda x, s, h: x[(s * SSM_HEADS + h) * SSM_HEADDIM:(s * SSM_HEADS + h + 1) * SSM_HEADDIM]
    xdt_col = _lane_sums(jnp.concatenate([jnp.where(diag(h), slab(xdt, s, h), 0.0) for s, h in sh], axis=0),
                         ones_full)
    for s, h in sh:
        ssm_out_ref[s, h] = ssm_ref[s, h] * decay_all[s:s + 1, h:h + 1] + rs(xdt_col, s, h) * bm(s, h)
    y_b = _lane_sums(jnp.concatenate([ssm_out_ref[s, h] * cm(s, h) for s, h in sh], axis=0), ones_full)
    y_d = jnp.concatenate([jnp.where(diag(h), rs(y_b, s, h), 0.0) for s, h in sh], axis=0)
    nrow = nseq * SSM_HEADS
    sel = (i128((nrow, nrow * SSM_HEADDIM), 1) // SSM_HEADDIM == i128((nrow, nrow * SSM_HEADDIM), 0)).astype(MXU_DTYPE)
    y_h = _row_sums(sel, y_d)
    for s in seqs:
        base = s * SSM_HEADS
        yssm_s[s:s + 1, :] = jnp.concatenate(
            [y_h[base + 2 * q:base + 2 * q + 1] + y_h[base + 2 * q + 1:base + 2 * q + 2]
             for q in range(SSM_HEADS // 2)], axis=1)
    y = (yssm_s[...] + d_exp * xh) * _silu(z_ref[0])
    gw = SSM_DIM // SSM_GROUPS
    for grp in range(SSM_GROUPS):
        yg = y[:, grp * gw:(grp + 1) * gw]
        y_ssm_ref[0, :, grp * gw:(grp + 1) * gw] = _rmsnorm(yg, ssmg_ref[:, grp * gw:(grp + 1) * gw])


def _mixers_step(proj, states, wp, layer, *, batch):
    shift_rkv, shift_sm, wkv2, sc_buf, ssm_buf, ssm = states
    depth = ssm.shape[0]
    nseq = _tile(batch, STEP_SEQS)
    nb = batch // nseq
    p3 = proj.reshape(nb, nseq, PROJ_COLS)
    grp = lambda a: a.reshape((depth, nb, nseq, -1))
    row = lambda width, blk: pl.BlockSpec((1, nseq, width), lambda b: (b, 0, blk))
    srow = lambda width: pl.BlockSpec((None, 1, nseq, width), lambda b: (layer, b, 0, 0))
    st4 = lambda d1, d2, d3: pl.BlockSpec((nseq, d1, d2, d3), lambda b: (b, 0, 0, 0))
    sst4 = lambda d1, d2, d3: pl.BlockSpec((None, nseq, d1, d2, d3), lambda b: (layer, b, 0, 0, 0))
    vec = lambda width: pl.BlockSpec((None, 1, width), lambda b: (layer, 0, 0))
    mat = lambda rows, width: pl.BlockSpec((None, rows, width), lambda b: (layer, 0, 0))
    sc_w = (SC_WIDTH - 1) * SC_DIM
    ssm_w = (SSM_CONV - 1) * SSM_CONV_DIM
    outs = pl.pallas_call(
        _step_kernel,
        grid=(nb,),
        in_specs=[
            row(3 * RWKV_DIM, COL_RKV // (3 * RWKV_DIM)),
            row(SMALL_W, COL_SMALL // SMALL_W),
            row(3 * SC_DIM, COL_SC // (3 * SC_DIM)),
            row(SSM_DIM, COL_Z // SSM_DIM),
            row(SSM_CONV_DIM, COL_XBC // SSM_CONV_DIM),
            srow(3 * RWKV_DIM), srow(SMALL_W),
            sst4(RWKV_HEADS, RWKV_HEAD // 2, LANE),
            srow(sc_w), srow(ssm_w),
            sst4(SSM_HEADS, SSM_HEADDIM, SSM_STATE),
        ] + [vec(3 * RWKV_DIM), vec(SMALL_W), vec(RWKV_DIM), mat(DECAY_LORA, RWKV_DIM), vec(RWKV_DIM),
             mat(AAA_LORA, RWKV_DIM), mat(GATE_LORA, RWKV_DIM), vec(RWKV_DIM), vec(RWKV_DIM), vec(RWKV_DIM),
             vec(RWKV_DIM), vec(RWKV_DIM),
             mat(SC_WIDTH, SC_DIM), vec(SC_DIM), mat(SSM_CONV, SSM_CONV_DIM), vec(SSM_CONV_DIM),
             vec(LANE), vec(LANE), vec(LANE), vec(SSM_DIM)],
        out_specs=[
            row(RWKV_DIM, 0), row(SC_DIM, 0), row(SSM_DIM, 0),
            st4(RWKV_HEADS, RWKV_HEAD // 2, LANE),
            row(sc_w, 0), row(ssm_w, 0),
            st4(SSM_HEADS, SSM_HEADDIM, SSM_STATE),
        ],
        out_shape=[
            jax.ShapeDtypeStruct((nb, nseq, RWKV_DIM), F32),
            jax.ShapeDtypeStruct((nb, nseq, SC_DIM), F32),
            jax.ShapeDtypeStruct((nb, nseq, SSM_DIM), F32),
            jax.ShapeDtypeStruct(wkv2.shape[1:], F32),
            jax.ShapeDtypeStruct((nb, nseq, sc_w), F32),
            jax.ShapeDtypeStruct((nb, nseq, ssm_w), F32),
            jax.ShapeDtypeStruct(ssm.shape[1:], F32),
        ],
        scratch_shapes=[pltpu.VMEM((6, nseq, RWKV_DIM), F32), pltpu.VMEM((6, nseq, RWKV_DIM), F32),
                        pltpu.VMEM((nseq, RWKV_DIM), F32), pltpu.VMEM((nseq, SSM_DIM), F32)],
        compiler_params=_params("parallel"),
        name="mixers_step",
    )(p3, p3, p3, p3, p3, grp(shift_rkv), grp(shift_sm), wkv2, grp(sc_buf), grp(ssm_buf), ssm,
      *_rwkv_weight_args(wp), wp["sc_conv_w"], wp["sc_norm"], wp["ssm_conv_w"], wp["ssm_conv_b"],
      wp["dt_bias_pad"], wp["a_log_pad"], wp["d_pad"], wp["ssm_norm"])
    y_rwkv, y_sc, y_ssm, wkv_n, sc_n, ssmbuf_n, ssm_n = outs
    return (y_rwkv.reshape(batch, RWKV_DIM), y_sc.reshape(batch, SC_DIM), y_ssm.reshape(batch, SSM_DIM),
            wkv_n, sc_n.reshape(sc_buf.shape[1:]), ssmbuf_n.reshape(ssm_buf.shape[1:]), ssm_n)


def _reorder_cols(a):
    lead = a.shape[:-1]
    rkv = a[..., :3 * RWKV_DIM]
    lora = a[..., 3 * RWKV_DIM:RWKV_PROJ]
    small = jnp.concatenate([lora, jnp.zeros(lead + (SMALL_W - lora.shape[-1],), a.dtype)], axis=-1)
    return rkv, small


def _prepare_weights(w):
    depth = w["w_in"].shape[0]
    wp = {}
    w_in = w["w_in"]
    o_ssm = RWKV_PROJ + 3 * SC_DIM
    o_dt = o_ssm + SSM_DIM + SSM_CONV_DIM
    piece = lambda lo, hi: w_in[..., lo:hi].astype(MXU_DTYPE)
    zeros = lambda width: jnp.zeros(w_in.shape[:-1] + (width,), MXU_DTYPE)
    wp["w_all"] = jnp.concatenate(
        [piece(0, 3 * RWKV_DIM), piece(RWKV_PROJ, o_dt),
         piece(3 * RWKV_DIM, RWKV_PROJ), zeros(SM_DT - (RWKV_PROJ - 3 * RWKV_DIM)),
         piece(o_dt, o_dt + SSM_HEADS), zeros(SMALL_W - SM_DT - SSM_HEADS)],
        axis=-1)
    mu_rkv, mu_sm = _reorder_cols(w["rwkv_mu"])
    wp["mu_rkv"] = mu_rkv.reshape(depth, 1, -1)
    wp["mu_sm"] = mu_sm.reshape(depth, 1, -1)
    for name in ("rwkv_w0", "rwkv_a0", "rwkv_k_k", "rwkv_k_a", "rwkv_ln_w", "rwkv_ln_b", "sc_norm",
                 "ssm_conv_b", "ssm_norm"):
        wp[name] = w[name].reshape(depth, 1, -1)
    wp["rwkv_r_k"] = w["rwkv_r_k"].reshape(depth, 1, RWKV_DIM)
    for name in ("rwkv_w2", "rwkv_a2", "rwkv_g2"):
        wp[name] = w[name].astype(MXU_DTYPE)
    wp["sc_conv_w"] = w["sc_conv_w"]
    wp["ssm_conv_w"] = w["ssm_conv_w"]
    pad_lane = lambda a: jnp.pad(a, ((0, 0), (0, LANE - a.shape[-1]))).reshape(depth, 1, LANE)
    wp["dt_bias_pad"] = pad_lane(w["ssm_dt_bias"])
    wp["a_log_pad"] = pad_lane(w["ssm_A_log"])
    wp["d_pad"] = pad_lane(w["ssm_D"])
    for name in ("ffn1_w_in", "ffn1_w_out", "ffn2_w_in", "ffn2_w_out", "w_out"):
        wp[name] = w[name].astype(MXU_DTYPE)
    for name in ("norm_ffn1", "norm_mix", "norm_ffn2"):
        wp[name] = w[name]
    return wp


def _shift_state_to_cols(proj_last):
    return jnp.concatenate([proj_last[..., :3 * RWKV_DIM],
                            proj_last[..., COL_SMALL:COL_SMALL + RWKV_PROJ - 3 * RWKV_DIM]], axis=-1)


def _tile(m, pref):
    t = min(m, pref)
    while m % t:
        t //= 2
    return t


def _tile_plan(m, seqlen):
    return dict(
        ffn_rows=_tile(m, 512), ffn_cols=D_FF // 2,
        proj_rows=_tile(m, 1024), proj_cols=PROJ_COLS // 4,
        out_rows=_tile(m, 512),
        conv_rows=_tile(seqlen, 256),
    )


def _trunk(x3, states, wp, norm_final):
    batch, seqlen, _ = x3.shape
    m = batch * seqlen
    x = x3.reshape(m, D_MODEL)
    depth = wp["w_all"].shape[0]
    tiles = _tile_plan(m, seqlen)
    new = ([], [], [], [], [])
    shift, wkv, sc_buf, ssm_buf, ssm = states
    shift_rkv, shift_sm = _reorder_cols(shift)
    shift_rkv = shift_rkv.reshape(depth, batch, 1, -1)
    shift_sm = shift_sm.reshape(depth, batch, 1, -1)
    if seqlen == 1:
        wkv = wkv.reshape(depth, batch, RWKV_HEADS, RWKV_HEAD // 2, LANE)
    for layer in range(depth):
        x = _ffn(x, wp["norm_ffn1"], wp["ffn1_w_in"], wp["ffn1_w_out"], layer,
                 tm=tiles["ffn_rows"], tf=tiles["ffn_cols"])
        proj = _proj_in(x, wp["norm_mix"], wp["w_all"], layer, tm=tiles["proj_rows"], tn=tiles["proj_cols"])
        if seqlen == 1:
            y_rwkv, y_sc, y_ssm, wkv_n, sc_n, ssmbuf_n, ssm_n = _mixers_step(
                proj, (shift_rkv, shift_sm, wkv, sc_buf, ssm_buf, ssm), wp, layer, batch=batch)
        else:
            y_rwkv, wkv_n = _rwkv_prompt(proj, shift_rkv, shift_sm, wkv, wp, layer, batch=batch, seqlen=seqlen)
            y_sc, sc_n = _sc_prompt(proj, sc_buf, wp, layer, batch=batch, seqlen=seqlen,
                                    tl=tiles["conv_rows"])
            y_ssm, ssmbuf_n, ssm_n = _ssd_prompt(proj, ssm_buf, ssm, wp, layer, batch=batch, seqlen=seqlen)
        shift_n = _shift_state_to_cols(proj.reshape(batch, seqlen, PROJ_COLS)[:, -1])
        x = _proj_out(x, y_rwkv, y_sc, y_ssm, wp["w_out"], layer, tm=tiles["out_rows"])
        x = _ffn(x, wp["norm_ffn2"], wp["ffn2_w_in"], wp["ffn2_w_out"], layer,
                 final_g=norm_final if layer == depth - 1 else None,
                 tm=tiles["ffn_rows"], tf=tiles["ffn_cols"])
        for lst, s in zip(new, (shift_n, wkv_n, sc_n, ssmbuf_n, ssm_n)):
            lst.append(s)
    new = [jnp.stack(lst) for lst in new]
    new[1] = new[1].reshape(depth, batch, RWKV_HEADS, RWKV_HEAD, RWKV_HEAD)
    return x.reshape(batch, seqlen, D_MODEL), tuple(new)


def kernel(x_prompt, x_sample, state_rwkv_shift, state_rwkv_wkv, state_sc_buf, state_ssm_conv, state_ssm,
           norm_ffn1, ffn1_w_in, ffn1_w_out, norm_mix, w_in, rwkv_mu, rwkv_w0, rwkv_w2, rwkv_a0, rwkv_a2,
           rwkv_g2, rwkv_k_k, rwkv_k_a, rwkv_r_k, rwkv_ln_w, rwkv_ln_b, sc_conv_w, sc_norm, ssm_conv_w,
           ssm_conv_b, ssm_dt_bias, ssm_A_log, ssm_D, ssm_norm, w_out, norm_ffn2, ffn2_w_in, ffn2_w_out,
           norm_final):
    weights = dict(
        norm_ffn1=norm_ffn1, ffn1_w_in=ffn1_w_in, ffn1_w_out=ffn1_w_out, norm_mix=norm_mix, w_in=w_in,
        rwkv_mu=rwkv_mu, rwkv_w0=rwkv_w0, rwkv_w2=rwkv_w2, rwkv_a0=rwkv_a0, rwkv_a2=rwkv_a2, rwkv_g2=rwkv_g2,
        rwkv_k_k=rwkv_k_k, rwkv_k_a=rwkv_k_a, rwkv_r_k=rwkv_r_k, rwkv_ln_w=rwkv_ln_w, rwkv_ln_b=rwkv_ln_b,
        sc_conv_w=sc_conv_w, sc_norm=sc_norm, ssm_conv_w=ssm_conv_w, ssm_conv_b=ssm_conv_b,
        ssm_dt_bias=ssm_dt_bias, ssm_A_log=ssm_A_log, ssm_D=ssm_D, ssm_norm=ssm_norm, w_out=w_out,
        norm_ffn2=norm_ffn2, ffn2_w_in=ffn2_w_in, ffn2_w_out=ffn2_w_out)
    wp = _prepare_weights(weights)
    depth = w_in.shape[0]
    nb, dt_ = x_prompt.shape[0], x_prompt.dtype
    zero_states = (
        jnp.zeros((depth, nb, RWKV_PROJ), dt_),
        jnp.zeros((depth, nb, RWKV_HEADS, RWKV_HEAD, RWKV_HEAD), dt_),
        jnp.zeros((depth, nb, SC_WIDTH - 1, SC_DIM), dt_),
        jnp.zeros((depth, nb, SSM_CONV - 1, SSM_CONV_DIM), dt_),
        jnp.zeros((depth, nb, SSM_HEADS, SSM_HEADDIM, SSM_STATE), dt_),
    )
    y_prompt, p_states = _trunk(x_prompt, zero_states, wp, norm_final)
    sample_states = (state_rwkv_shift, state_rwkv_wkv, state_sc_buf, state_ssm_conv, state_ssm)
    y_sample, s_states = _trunk(x_sample, sample_states, wp, norm_final)
    return (y_prompt, y_sample) + tuple(p_states) + tuple(s_states)
```

```python
import functools

import jax
import jax.numpy as jnp
from jax import lax
from jax.experimental import pallas as pl
from jax.experimental.pallas import tpu as pltpu

F32 = jnp.float32
MXU_DTYPE = jnp.bfloat16

D_MODEL = 1024
D_FF = 2816
RWKV_DIM = 1024
RWKV_HEAD = 64
RWKV_HEADS = 16
DECAY_LORA = 64
AAA_LORA = 64
GATE_LORA = 160
RWKV_PROJ = 3 * RWKV_DIM + DECAY_LORA + AAA_LORA + GATE_LORA
RWKV_GN_EPS = 64e-5
DECAY_SCALE = 0.6065306597126334
SC_DIM = 512
SC_WIDTH = 3
SSM_DIM = 512
SSM_HEADDIM = 64
SSM_HEADS = 8
SSM_GROUPS = 2
SSM_STATE = 128
SSM_CONV = 4
SSM_CONV_DIM = SSM_DIM + 2 * SSM_GROUPS * SSM_STATE
NORM_EPS = 1e-6

COL_RKV = 0
COL_SC = 3 * RWKV_DIM
COL_Z = COL_SC + 3 * SC_DIM
COL_XBC = COL_Z + SSM_DIM
COL_SMALL = COL_XBC + SSM_CONV_DIM
SMALL_W = 512
SM_GL = 128
SM_DT = 384
PROJ_COLS = COL_SMALL + SMALL_W

LANE = 128
SUBLANE = 8
VMEM_LIMIT = 56 * 1024 * 1024

RWKV_CHUNK = 64
RWKV_CHUNKS_PER_STEP = 4
INV_BASE = 8
STEP_SEQS = 8
SSD_CHUNK = 128
SSD_CHUNKS_PER_STEP = 1


def _mm(a, b):
    return jnp.dot(a.astype(MXU_DTYPE), b.astype(MXU_DTYPE), preferred_element_type=F32)


def _mm_nt(a, b):
    return lax.dot_general(a.astype(MXU_DTYPE), b.astype(MXU_DTYPE), (((1,), (1,)), ((), ())),
                           preferred_element_type=F32)


def _mm_tn(a, b):
    return lax.dot_general(a.astype(MXU_DTYPE), b.astype(MXU_DTYPE), (((0,), (0,)), ((), ())),
                           preferred_element_type=F32)


def _sigmoid(x):
    return 1.0 / (1.0 + jnp.exp(-x))


def _silu(x):
    return x * _sigmoid(x)


def _softplus(x):
    return jnp.maximum(x, 0.0) + jnp.log1p(jnp.exp(-jnp.abs(x)))


def _rmsnorm(x, g):
    return x * lax.rsqrt(jnp.mean(x * x, axis=-1, keepdims=True) + NORM_EPS) * g


def _params(*sem):
    return pltpu.CompilerParams(dimension_semantics=sem, vmem_limit_bytes=VMEM_LIMIT)


def _ffn_kernel(x_ref, g_ref, wg_ref, wu_ref, wo_ref, gf_ref, o_ref, h_ref, acc_ref, *, final_norm):
    j = pl.program_id(1)

    @pl.when(j == 0)
    def _():
        h_ref[...] = _rmsnorm(x_ref[...], g_ref[...]).astype(h_ref.dtype)
        acc_ref[...] = jnp.zeros_like(acc_ref)

    h = h_ref[...]
    gate = jnp.dot(h, wg_ref[...], preferred_element_type=F32)
    up = jnp.dot(h, wu_ref[...], preferred_element_type=F32)
    act = (_silu(gate) * up).astype(MXU_DTYPE)
    acc_ref[...] += jnp.dot(act, wo_ref[...], preferred_element_type=F32)

    @pl.when(j == pl.num_programs(1) - 1)
    def _():
        y = x_ref[...] + 0.5 * acc_ref[...]
        if final_norm:
            y = _rmsnorm(y, gf_ref[...])
        o_ref[...] = y


def _ffn(x, norm_g, w_in, w_out, layer, final_g=None, *, tm, tf):
    m = x.shape[0]
    nf = D_FF // tf
    final_norm = final_g is not None
    gf = final_g if final_norm else norm_g[layer]
    return pl.pallas_call(
        functools.partial(_ffn_kernel, final_norm=final_norm),
        grid=(m // tm, nf),
        in_specs=[
            pl.BlockSpec((tm, D_MODEL), lambda i, j: (i, 0)),
            pl.BlockSpec((None, 1, D_MODEL), lambda i, j: (layer, 0, 0)),
            pl.BlockSpec((None, D_MODEL, tf), lambda i, j: (layer, 0, j)),
            pl.BlockSpec((None, D_MODEL, tf), lambda i, j: (layer, 0, nf + j)),
            pl.BlockSpec((None, tf, D_MODEL), lambda i, j: (layer, j, 0)),
            pl.BlockSpec((1, D_MODEL), lambda i, j: (0, 0)),
        ],
        out_specs=pl.BlockSpec((tm, D_MODEL), lambda i, j: (i, 0)),
        out_shape=jax.ShapeDtypeStruct((m, D_MODEL), F32),
        scratch_shapes=[pltpu.VMEM((tm, D_MODEL), MXU_DTYPE), pltpu.VMEM((tm, D_MODEL), F32)],
        compiler_params=_params("parallel", "arbitrary"),
        name="ffn",
    )(x, norm_g.reshape(-1, 1, D_MODEL), w_in, w_in, w_out, gf.reshape(1, D_MODEL))


def _proj_kernel(x_ref, g_ref, w_ref, o_ref, h_ref):
    @pl.when(pl.program_id(1) == 0)
    def _():
        h_ref[...] = _rmsnorm(x_ref[...], g_ref[...]).astype(h_ref.dtype)

    o_ref[...] = jnp.dot(h_ref[...], w_ref[...], preferred_element_type=F32)


def _proj_in(x, norm_g, w_all, layer, *, tm, tn):
    m = x.shape[0]
    return pl.pallas_call(
        _proj_kernel,
        grid=(m // tm, PROJ_COLS // tn),
        in_specs=[
            pl.BlockSpec((tm, D_MODEL), lambda i, j: (i, 0)),
            pl.BlockSpec((None, 1, D_MODEL), lambda i, j: (layer, 0, 0)),
            pl.BlockSpec((None, D_MODEL, tn), lambda i, j: (layer, 0, j)),
        ],
        out_specs=pl.BlockSpec((tm, tn), lambda i, j: (i, j)),
        out_shape=jax.ShapeDtypeStruct((m, PROJ_COLS), F32),
        scratch_shapes=[pltpu.VMEM((tm, D_MODEL), MXU_DTYPE)],
        compiler_params=_params("parallel", "arbitrary"),
        name="proj_in",
    )(x, norm_g.reshape(-1, 1, D_MODEL), w_all)


def _proj_out_kernel(x_ref, y1_ref, y2_ref, y3_ref, w1_ref, w2_ref, w3_ref, o_ref):
    acc = jnp.dot(y1_ref[...].astype(MXU_DTYPE), w1_ref[...], preferred_element_type=F32)
    acc += jnp.dot(y2_ref[...].astype(MXU_DTYPE), w2_ref[...], preferred_element_type=F32)
    acc += jnp.dot(y3_ref[...].astype(MXU_DTYPE), w3_ref[...], preferred_element_type=F32)
    o_ref[...] = x_ref[...] + acc


def _proj_out(x, y_rwkv, y_sc, y_ssm, w_out, layer, *, tm):
    m = x.shape[0]
    return pl.pallas_call(
        _proj_out_kernel,
        grid=(m // tm,),
        in_specs=[
            pl.BlockSpec((tm, D_MODEL), lambda i: (i, 0)),
            pl.BlockSpec((tm, RWKV_DIM), lambda i: (i, 0)),
            pl.BlockSpec((tm, SC_DIM), lambda i: (i, 0)),
            pl.BlockSpec((tm, SSM_DIM), lambda i: (i, 0)),
            pl.BlockSpec((None, RWKV_DIM, D_MODEL), lambda i: (layer, 0, 0)),
            pl.BlockSpec((None, SC_DIM, D_MODEL), lambda i: (layer, RWKV_DIM // SC_DIM, 0)),
            pl.BlockSpec((None, SSM_DIM, D_MODEL), lambda i: (layer, (RWKV_DIM + SC_DIM) // SSM_DIM, 0)),
        ],
        out_specs=pl.BlockSpec((tm, D_MODEL), lambda i: (i, 0)),
        out_shape=jax.ShapeDtypeStruct((m, D_MODEL), F32),
        compiler_params=_params("parallel"),
        name="proj_out",
    )(x, y_rwkv, y_sc, y_ssm, w_out, w_out, w_out)


def _previous_rows(x, carry_row):
    first = lax.broadcasted_iota(jnp.int32, x.shape, 0) == 0
    return jnp.where(first, carry_row, pltpu.roll(x, 1, 0))


def _rows_shifted(x, tail, s):
    rolled = pltpu.roll(x, s, 0)
    first = lax.broadcasted_iota(jnp.int32, tail.shape, 0) < s
    head = jnp.where(first, pltpu.roll(tail, s, 0), rolled[:SUBLANE])
    return jnp.concatenate([head, rolled[SUBLANE:]], axis=0)


def _rwkv_prepare(p_rkv, prev_rkv, p_sm, prev_sm, wts):
    (mu_rkv, mu_sm, w0, w2, a0, a2, g2, k_k, k_a) = wts
    xs = p_rkv + (prev_rkv - p_rkv) * mu_rkv
    xm = p_sm + (prev_sm - p_sm) * mu_sm
    r = xs[:, :RWKV_DIM]
    k = xs[:, RWKV_DIM:2 * RWKV_DIM]
    v = xs[:, 2 * RWKV_DIM:]
    w_lr = xm[:, :DECAY_LORA]
    a_lr = xm[:, DECAY_LORA:DECAY_LORA + AAA_LORA]
    g_lr = xm[:, SM_GL:SM_GL + GATE_LORA]
    log_decay = -DECAY_SCALE * _sigmoid(w0 + _mm(jnp.tanh(w_lr), w2))
    a = _sigmoid(a0 + _mm(a_lr, a2))
    g = _mm(_sigmoid(g_lr), g2)
    kk = k * k_k
    k_mod = k * (1.0 + (a - 1.0) * k_a)
    return r, log_decay, k_mod, v, kk, a, g


def _rwkv_weight_specs(layer):
    def vec(width):
        return pl.BlockSpec((None, 1, width), lambda *idx: (layer, 0, 0))

    def mat(rows):
        return pl.BlockSpec((None, rows, RWKV_DIM), lambda *idx: (layer, 0, 0))

    return [vec(3 * RWKV_DIM), vec(SMALL_W), vec(RWKV_DIM), mat(DECAY_LORA), vec(RWKV_DIM), mat(AAA_LORA),
            mat(GATE_LORA), vec(RWKV_DIM), vec(RWKV_DIM), vec(RWKV_DIM), vec(RWKV_DIM), vec(RWKV_DIM)]


def _rwkv_weight_args(wp):
    return (wp["mu_rkv"], wp["mu_sm"], wp["rwkv_w0"], wp["rwkv_w2"], wp["rwkv_a0"], wp["rwkv_a2"],
            wp["rwkv_g2"], wp["rwkv_k_k"], wp["rwkv_k_a"], wp["rwkv_r_k"], wp["rwkv_ln_w"], wp["rwkv_ln_b"])


HEADS_PER_GROUP = 4
GROUP_W = HEADS_PER_GROUP * RWKV_HEAD
RWKV_GROUPS = RWKV_HEADS // HEADS_PER_GROUP


def _split3(x):
    hi = x.astype(MXU_DTYPE)
    r1 = x - hi.astype(F32)
    mid = r1.astype(MXU_DTYPE)
    lo = (r1 - mid.astype(F32)).astype(MXU_DTYPE)
    return hi, mid, lo


def _block_diag(x, head_masks):
    return jnp.concatenate([x * m for m in head_masks], axis=0)


def _rwkv_chunk_kernel(p_rkv_ref, p_sm_ref, sh_rkv_ref, sh_sm_ref, s0_ref,
                       mu_rkv_ref, mu_sm_ref, w0_ref, w2_ref, a0_ref, a2_ref, g2_ref, kk_ref, ka_ref,
                       rk_ref, lnw_ref, lnb_ref,
                       y_ref, s_out_ref,
                       pad_rkv, pad_sm, state, at_s, rt_s, bt_s, kt_s, bh_s, kh_s, v_s, yn_s,
                       hm_s, ones_s, tri_s):
    c = RWKV_CHUNK
    nc = RWKV_CHUNKS_PER_STEP
    tl = nc * c
    n = RWKV_HEAD
    gw = GROUP_W
    t = pl.program_id(1)

    @pl.when(t == 0)
    def _():
        pad_rkv[...] = sh_rkv_ref[0]
        pad_sm[...] = sh_sm_ref[0]
        for h in range(RWKV_HEADS):
            state[:, h * n:(h + 1) * n] = s0_ref[0, h]
        lane_head = lax.broadcasted_iota(jnp.int32, (c, gw), 1) // n
        for hh in range(HEADS_PER_GROUP):
            hm_s[hh] = (lane_head == hh).astype(MXU_DTYPE)
        sq_r = lax.broadcasted_iota(jnp.int32, (gw, gw), 0) // n
        sq_c = lax.broadcasted_iota(jnp.int32, (gw, gw), 1) // n
        ones_s[...] = (sq_r == sq_c).astype(MXU_DTYPE)
        tri_r = lax.broadcasted_iota(jnp.int32, (tl, tl), 0)
        tri_c = lax.broadcasted_iota(jnp.int32, (tl, tl), 1)
        tri_s[...] = ((tri_r >= tri_c) & (tri_r // c == tri_c // c)).astype(MXU_DTYPE)

    wts = (mu_rkv_ref[...], mu_sm_ref[...], w0_ref[...], w2_ref[...], a0_ref[...], a2_ref[...],
           g2_ref[...], kk_ref[...], ka_ref[...])
    r, log_decay, k_mod, v, kk, a, g = _rwkv_prepare(
        p_rkv_ref[...], _previous_rows(p_rkv_ref[...], pad_rkv[...]),
        p_sm_ref[...], _previous_rows(p_sm_ref[...], pad_sm[...]), wts)
    pad_rkv[...] = p_rkv_ref[tl - 1:tl, :]
    pad_sm[...] = p_sm_ref[tl - 1:tl, :]

    row = lax.broadcasted_iota(jnp.int32, (c, gw), 0)
    lane = lax.broadcasted_iota(jnp.int32, (c, gw), 1)
    pos = lane % n
    strict = row > pos
    incl = row >= pos
    eye = (row == pos).astype(F32)
    head_masks = [hm_s[hh] for hh in range(HEADS_PER_GROUP)]
    ones_blk = ones_s[...]
    tri = tri_s[...]

    def head_sum(x):
        return jnp.concatenate(
            [jnp.dot(x[:, gi * gw:(gi + 1) * gw].astype(MXU_DTYPE), ones_blk, preferred_element_type=F32)
             for gi in range(RWKV_GROUPS)], axis=1)

    hi, mid, _ = _split3(log_decay)
    cum = (jnp.dot(tri, hi, preferred_element_type=F32)
           + jnp.dot(tri, mid, preferred_element_type=F32))
    cum_end = jnp.concatenate([jnp.broadcast_to(cum[(ci + 1) * c - 1:(ci + 1) * c, :], (c, RWKV_DIM))
                               for ci in range(nc)], axis=0)
    e_out = jnp.exp(-cum)
    e_end = jnp.exp(cum_end - cum)
    kk = kk * lax.rsqrt(jnp.maximum(head_sum(kk * kk), 1e-24))
    b = kk * a
    at_s[...] = (-kk * jnp.exp(cum - log_decay)).astype(MXU_DTYPE)
    rt_s[...] = (r * jnp.exp(cum)).astype(MXU_DTYPE)
    bt_s[...] = (b * e_out).astype(MXU_DTYPE)
    kt_s[...] = (k_mod * e_out).astype(MXU_DTYPE)
    bh_s[...] = (b * e_end).astype(MXU_DTYPE)
    kh_s[...] = (k_mod * e_end).astype(MXU_DTYPE)
    v_s[...] = v
    bonus = head_sum(r * k_mod * rk_ref[...])
    w_end = jnp.exp(cum_end)

    def mm(x, y):
        return jnp.dot(x.astype(MXU_DTYPE), y, preferred_element_type=F32)

    def bd(x):
        return _block_diag(x.astype(MXU_DTYPE), head_masks)

    groups = range(RWKV_GROUPS)
    chunks = range(nc)
    pairs = [(ci, gi) for ci in chunks for gi in groups]
    rws = [slice(ci * c, (ci + 1) * c) for ci in chunks]
    sls = [slice(gi * gw, (gi + 1) * gw) for gi in groups]
    nt = (((1,), (1,)), ((), ()))
    ar = {(ci, gi): jnp.concatenate([at_s[rws[ci], sls[gi]], rt_s[rws[ci], sls[gi]]], axis=0)
          for ci, gi in pairs}
    vg = {(ci, gi): v_s[rws[ci], sls[gi]].astype(MXU_DTYPE) for ci, gi in pairs}
    gram_b = {(ci, gi): lax.dot_general(ar[ci, gi], bd(bt_s[rws[ci], sls[gi]]), nt, preferred_element_type=F32)
              for ci, gi in pairs}
    gram_k = {(ci, gi): lax.dot_general(ar[ci, gi], bd(kt_s[rws[ci], sls[gi]]), nt, preferred_element_type=F32)
              for ci, gi in pairs}
    l_ab = {p: jnp.where(strict, gram_b[p][:c], 0.0) for p in pairs}
    m_rb = {p: jnp.where(incl, gram_b[p][c:], 0.0) for p in pairs}
    l_akrk = {p: jnp.where(jnp.concatenate([strict, incl], axis=0), gram_k[p], 0.0) for p in pairs}
    def same_block(width):
        return row // width == pos // width

    base = {p: jnp.where(same_block(INV_BASE), l_ab[p], 0.0) for p in pairs}
    inv = {p: eye + base[p] for p in pairs}
    power = base
    power_bd = {p: bd(power[p]) for p in pairs}
    span = 2
    while span < INV_BASE:
        power = {p: mm(power[p], power_bd[p]) for p in pairs}
        power_bd = {p: bd(power[p]) for p in pairs}
        inv = {p: inv[p] + mm(inv[p], power_bd[p]) for p in pairs}
        span *= 2
    width = 2 * INV_BASE
    while width <= c:
        below = same_block(width) & jnp.logical_not(same_block(width // 2))
        sub = {p: mm(inv[p], bd(jnp.where(below, l_ab[p], 0.0))) for p in pairs}
        inv = {p: inv[p] + mm(sub[p], bd(inv[p])) for p in pairs}
        width *= 2
    y = {}
    for ci in chunks:
        s0 = [state[:, sl] for sl in sls]
        z = [mm(l_akrk[ci, gi], bd(vg[ci, gi]))
             + lax.dot_general(ar[ci, gi], bd(s0[gi]), nt, preferred_element_type=F32) for gi in groups]
        u = [mm(inv[ci, gi], bd(z[gi][:c])) for gi in groups]
        for gi in groups:
            y[ci, gi] = z[gi][c:] + mm(m_rb[ci, gi], bd(u[gi]))
        for gi in groups:
            uv = jnp.concatenate([u[gi].astype(MXU_DTYPE), vg[ci, gi]], axis=0)
            bk = jnp.concatenate([bh_s[rws[ci], sls[gi]], kh_s[rws[ci], sls[gi]]], axis=0)
            upd = lax.dot_general(uv, bk, (((0,), (0,)), ((), ())), preferred_element_type=F32)
            s_new = s0[gi] * w_end[ci * c:ci * c + 1, sls[gi]]
            for hh in range(HEADS_PER_GROUP):
                s_new += upd[hh * n:(hh + 1) * n, :] * head_masks[hh].astype(F32)
            state[:, sls[gi]] = s_new
    mean = {p: mm(y[p], ones_blk) * (1.0 / n) for p in pairs}
    yc = {p: y[p] - mean[p] for p in pairs}
    var = {p: mm(yc[p] * yc[p], ones_blk) * (1.0 / n) for p in pairs}
    for ci, gi in pairs:
        yn_s[rws[ci], sls[gi]] = yc[ci, gi] * lax.rsqrt(var[ci, gi] + RWKV_GN_EPS)

    y_ref[...] = (yn_s[...] * lnw_ref[...] + lnb_ref[...] + bonus * v_s[...]) * g

    @pl.when(t == pl.num_programs(1) - 1)
    def _():
        for h in range(RWKV_HEADS):
            s_out_ref[0, h] = state[:, h * n:(h + 1) * n]


def _rwkv_prompt(proj, shift_rkv, shift_sm, wkv0, wp, layer, *, batch, seqlen):
    c = RWKV_CHUNK * RWKV_CHUNKS_PER_STEP
    nt = seqlen // c
    return pl.pallas_call(
        _rwkv_chunk_kernel,
        grid=(batch, nt),
        in_specs=[
            pl.BlockSpec((c, 3 * RWKV_DIM), lambda b, t: (b * nt + t, COL_RKV // (3 * RWKV_DIM))),
            pl.BlockSpec((c, SMALL_W), lambda b, t: (b * nt + t, COL_SMALL // SMALL_W)),
            pl.BlockSpec((None, 1, 1, 3 * RWKV_DIM), lambda b, t: (layer, b, 0, 0)),
            pl.BlockSpec((None, 1, 1, SMALL_W), lambda b, t: (layer, b, 0, 0)),
            pl.BlockSpec((None, 1, RWKV_HEADS, RWKV_HEAD, RWKV_HEAD), lambda b, t: (layer, b, 0, 0, 0)),
        ] + _rwkv_weight_specs(layer),
        out_specs=[
            pl.BlockSpec((c, RWKV_DIM), lambda b, t: (b * nt + t, 0)),
            pl.BlockSpec((1, RWKV_HEADS, RWKV_HEAD, RWKV_HEAD), lambda b, t: (b, 0, 0, 0)),
        ],
        out_shape=[
            jax.ShapeDtypeStruct((batch * seqlen, RWKV_DIM), F32),
            jax.ShapeDtypeStruct((batch, RWKV_HEADS, RWKV_HEAD, RWKV_HEAD), F32),
        ],
        scratch_shapes=[
            pltpu.VMEM((1, 3 * RWKV_DIM), F32),
            pltpu.VMEM((1, SMALL_W), F32),
            pltpu.VMEM((RWKV_HEAD, RWKV_DIM), F32),
        ] + [pltpu.VMEM((c, RWKV_DIM), MXU_DTYPE) for _ in range(6)]
          + [pltpu.VMEM((c, RWKV_DIM), F32) for _ in range(2)]
          + [pltpu.VMEM((HEADS_PER_GROUP, RWKV_CHUNK, GROUP_W), MXU_DTYPE),
             pltpu.VMEM((GROUP_W, GROUP_W), MXU_DTYPE), pltpu.VMEM((c, c), MXU_DTYPE)],
        compiler_params=_params("parallel", "arbitrary"),
        name="rwkv_chunk",
    )(proj, proj, shift_rkv, shift_sm, wkv0, *_rwkv_weight_args(wp))


def _sc_kernel(p_ref, buf_ref, w_ref, g_ref, y_ref, buf_out_ref, pad, *, tl):
    t = pl.program_id(1)
    hist = SC_WIDTH - 1

    @pl.when(t == 0)
    def _():
        pad[...] = jnp.zeros_like(pad)
        pad[SUBLANE - hist:SUBLANE, :] = buf_ref[0]

    b_gate = p_ref[:, :SC_DIM]
    u = p_ref[:, SC_DIM:2 * SC_DIM] * p_ref[:, 2 * SC_DIM:]
    tail = pad[...]
    conv = u * w_ref[hist:hist + 1, :]
    for j in range(hist):
        conv += _rows_shifted(u, tail, hist - j) * w_ref[j:j + 1, :]
    y_ref[...] = _rmsnorm(b_gate * conv, g_ref[...])
    pad[...] = u[tl - SUBLANE:, :]

    @pl.when(t == pl.num_programs(1) - 1)
    def _():
        buf_out_ref[0] = pad[SUBLANE - hist:SUBLANE, :]


def _sc_prompt(proj, buf0, wp, layer, *, batch, seqlen, tl):
    nt = seqlen // tl
    hist = SC_WIDTH - 1
    return pl.pallas_call(
        functools.partial(_sc_kernel, tl=tl),
        grid=(batch, nt),
        in_specs=[
            pl.BlockSpec((tl, 3 * SC_DIM), lambda b, t: (b * nt + t, COL_SC // (3 * SC_DIM))),
            pl.BlockSpec((None, 1, hist, SC_DIM), lambda b, t: (layer, b, 0, 0)),
            pl.BlockSpec((None, SC_WIDTH, SC_DIM), lambda b, t: (layer, 0, 0)),
            pl.BlockSpec((None, 1, SC_DIM), lambda b, t: (layer, 0, 0)),
        ],
        out_specs=[
            pl.BlockSpec((tl, SC_DIM), lambda b, t: (b * nt + t, 0)),
            pl.BlockSpec((1, hist, SC_DIM), lambda b, t: (b, 0, 0)),
        ],
        out_shape=[
            jax.ShapeDtypeStruct((batch * seqlen, SC_DIM), F32),
            jax.ShapeDtypeStruct((batch, hist, SC_DIM), F32),
        ],
        scratch_shapes=[pltpu.VMEM((SUBLANE, SC_DIM), F32)],
        compiler_params=_params("parallel", "arbitrary"),
        name="short_conv",
    )(proj, buf0, wp["sc_conv_w"], wp["sc_norm"])


def _ssd_kernel(z_ref, xbc_ref, dt_ref, buf_ref, h0_ref, cw_ref, cb_ref, dtb_ref, alog_ref, dskip_ref, g_ref,
                y_ref, buf_out_ref, h_out_ref, pad, state, y_s, exh_s, exs_s):
    q = SSD_CHUNK
    nq = SSD_CHUNKS_PER_STEP
    tq = nq * q
    t = pl.program_id(1)
    hist = SSM_CONV - 1
    gw = SSM_DIM // SSM_GROUPS

    @pl.when(t == 0)
    def _():
        pad[...] = jnp.zeros_like(pad)
        pad[SUBLANE - hist:SUBLANE, :] = buf_ref[0]
        state[...] = h0_ref[0]
        head_row = lax.broadcasted_iota(jnp.int32, (LANE, SSM_DIM), 0)
        exh_s[...] = (lax.broadcasted_iota(jnp.int32, (LANE, SSM_DIM), 1) // SSM_HEADDIM == head_row).astype(MXU_DTYPE)
        seg_row = lax.broadcasted_iota(jnp.int32, (LANE, SSM_HEADS * q), 0)
        exs_s[...] = (lax.broadcasted_iota(jnp.int32, (LANE, SSM_HEADS * q), 1) // q == seg_row).astype(MXU_DTYPE)

    xbc = xbc_ref[...]
    tail = pad[...]
    conv = xbc * cw_ref[hist:hist + 1, :]
    for j in range(hist):
        conv += _rows_shifted(xbc, tail, hist - j) * cw_ref[j:j + 1, :]
    pad[...] = xbc[tq - SUBLANE:, :]
    xc = _silu(conv + cb_ref[...])
    xh = xc[:, :SSM_DIM]

    dt = _softplus(dt_ref[...] + dtb_ref[...])
    neg_a = -jnp.exp(alog_ref[...])
    row = lax.broadcasted_iota(jnp.int32, (tq, tq), 0)
    col = lax.broadcasted_iota(jnp.int32, (tq, tq), 1)
    tri = ((row >= col) & (row // q == col // q)).astype(MXU_DTYPE)
    hi, mid, lo = _split3(dt * neg_a)
    cum = (jnp.dot(tri, hi, preferred_element_type=F32) + jnp.dot(tri, mid, preferred_element_type=F32)
           + jnp.dot(tri, lo, preferred_element_type=F32))
    cum_t = cum.T
    cum_last = jnp.concatenate([jnp.broadcast_to(cum[(ci + 1) * q - 1:(ci + 1) * q, :], (q, LANE))
                                for ci in range(nq)], axis=0)

    def expand(x, e):
        h3 = _split3(x)
        return sum(jnp.dot(part, e, preferred_element_type=F32) for part in h3)

    ex_head = exh_s[...]
    ex_seg = exs_s[...]
    xdt_all = xh * expand(dt, ex_head)
    xdt_end = xdt_all * expand(jnp.exp(cum_last - cum), ex_head)
    ecum_x = expand(jnp.exp(cum), ex_head)
    cum_seg = expand(cum, ex_seg)
    causal = lax.broadcasted_iota(jnp.int32, (q, q), 0) >= lax.broadcasted_iota(jnp.int32, (q, q), 1)
    heads_per_group = SSM_HEADS // SSM_GROUPS

    heads = range(SSM_HEADS)
    chunks = range(nq)
    pairs = [(ci, h) for ci in chunks for h in heads]
    grp_of = [h // heads_per_group for h in heads]
    rws = [slice(ci * q, (ci + 1) * q) for ci in chunks]
    sls = [slice(h * SSM_HEADDIM, (h + 1) * SSM_HEADDIM) for h in heads]
    bm = {(ci, g): xc[rws[ci], SSM_DIM + g * SSM_STATE:SSM_DIM + (g + 1) * SSM_STATE].astype(MXU_DTYPE)
          for ci in chunks for g in range(SSM_GROUPS)}
    cm = {(ci, g): xc[rws[ci], SSM_DIM + (SSM_GROUPS + g) * SSM_STATE:SSM_DIM + (SSM_GROUPS + g + 1) * SSM_STATE]
          .astype(MXU_DTYPE) for ci in chunks for g in range(SSM_GROUPS)}
    gram = {k: _mm_nt(cm[k], bm[k]) for k in bm}
    cum_end = {(ci, h): cum[(ci + 1) * q - 1:(ci + 1) * q, h:h + 1] for ci, h in pairs}
    seg = {(ci, h): cum_seg[rws[ci], h * q:(h + 1) * q] - cum_t[h:h + 1, rws[ci]] for ci, h in pairs}
    decay = {p: jnp.where(causal, jnp.exp(jnp.where(causal, seg[p], 0.0)), 0.0) for p in pairs}
    xdt = {(ci, h): xdt_all[rws[ci], sls[h]] for ci, h in pairs}
    y = {(ci, h): _mm(gram[ci, grp_of[h]] * decay[ci, h], xdt[ci, h]) for ci, h in pairs}
    upd = {(ci, h): _mm_tn(xdt_end[rws[ci], sls[h]], bm[ci, grp_of[h]]) for ci, h in pairs}
    for ci in chunks:
        y_off = [_mm_nt(cm[ci, grp_of[h]], state[h]) * ecum_x[rws[ci], sls[h]] for h in heads]
        for h in heads:
            state[h] = state[h] * jnp.exp(cum_end[ci, h]) + upd[ci, h]
            y_s[rws[ci], sls[h]] = y[ci, h] + y_off[h]

    y = (y_s[...] + expand(dskip_ref[...], ex_head) * xh) * _silu(z_ref[...])
    for grp in range(SSM_GROUPS):
        yg = y[:, grp * gw:(grp + 1) * gw]
        y_ref[:, grp * gw:(grp + 1) * gw] = _rmsnorm(yg, g_ref[:, grp * gw:(grp + 1) * gw])

    @pl.when(t == pl.num_programs(1) - 1)
    def _():
        buf_out_ref[0] = pad[SUBLANE - hist:SUBLANE, :]
        h_out_ref[0] = state[...]


def _ssd_prompt(proj, buf0, h0, wp, layer, *, batch, seqlen):
    q = SSD_CHUNK * SSD_CHUNKS_PER_STEP
    nt = seqlen // q
    hist = SSM_CONV - 1
    lane_vec = pl.BlockSpec((None, 1, LANE), lambda b, t: (layer, 0, 0))
    return pl.pallas_call(
        _ssd_kernel,
        grid=(batch, nt),
        in_specs=[
            pl.BlockSpec((q, SSM_DIM), lambda b, t: (b * nt + t, COL_Z // SSM_DIM)),
            pl.BlockSpec((q, SSM_CONV_DIM), lambda b, t: (b * nt + t, COL_XBC // SSM_CONV_DIM)),
            pl.BlockSpec((q, LANE), lambda b, t: (b * nt + t, (COL_SMALL + SM_DT) // LANE)),
            pl.BlockSpec((None, 1, hist, SSM_CONV_DIM), lambda b, t: (layer, b, 0, 0)),
            pl.BlockSpec((None, 1, SSM_HEADS, SSM_HEADDIM, SSM_STATE), lambda b, t: (layer, b, 0, 0, 0)),
            pl.BlockSpec((None, SSM_CONV, SSM_CONV_DIM), lambda b, t: (layer, 0, 0)),
            pl.BlockSpec((None, 1, SSM_CONV_DIM), lambda b, t: (layer, 0, 0)),
            lane_vec, lane_vec, lane_vec,
            pl.BlockSpec((None, 1, SSM_DIM), lambda b, t: (layer, 0, 0)),
        ],
        out_specs=[
            pl.BlockSpec((q, SSM_DIM), lambda b, t: (b * nt + t, 0)),
            pl.BlockSpec((1, hist, SSM_CONV_DIM), lambda b, t: (b, 0, 0)),
            pl.BlockSpec((1, SSM_HEADS, SSM_HEADDIM, SSM_STATE), lambda b, t: (b, 0, 0, 0)),
        ],
        out_shape=[
            jax.ShapeDtypeStruct((batch * seqlen, SSM_DIM), F32),
            jax.ShapeDtypeStruct((batch, hist, SSM_CONV_DIM), F32),
            jax.ShapeDtypeStruct((batch, SSM_HEADS, SSM_HEADDIM, SSM_STATE), F32),
        ],
        scratch_shapes=[
            pltpu.VMEM((SUBLANE, SSM_CONV_DIM), F32),
            pltpu.VMEM((SSM_HEADS, SSM_HEADDIM, SSM_STATE), F32),
            pltpu.VMEM((q, SSM_DIM), F32),
            pltpu.VMEM((LANE, SSM_DIM), MXU_DTYPE),
            pltpu.VMEM((LANE, SSM_HEADS * SSD_CHUNK), MXU_DTYPE),
        ],
        compiler_params=_params("parallel", "arbitrary"),
        name="ssd_chunk",
    )(proj, proj, proj, buf0, h0, wp["ssm_conv_w"], wp["ssm_conv_b"], wp["dt_bias_pad"], wp["a_log_pad"],
      wp["d_pad"], wp["ssm_norm"])


def _lane_sums(x, ones):
    hi = x.astype(MXU_DTYPE)
    mid = (x - hi.astype(F32)).astype(MXU_DTYPE)
    return jnp.dot(hi, ones, preferred_element_type=F32) + jnp.dot(mid, ones, preferred_element_type=F32)


def _row_sums(sel, x):
    hi = x.astype(MXU_DTYPE)
    mid = (x - hi.astype(F32)).astype(MXU_DTYPE)
    return jnp.dot(sel, hi, preferred_element_type=F32) + jnp.dot(sel, mid, preferred_element_type=F32)


def _head_sums_rows(x, ones_pair):
    return jnp.concatenate([_lane_sums(x[:, q * LANE:(q + 1) * LANE], ones_pair)
                            for q in range(RWKV_DIM // LANE)], axis=1)


def _step_kernel(p_rkv_ref, p_sm_ref, p_sc_ref, z_ref, xbc_ref,
                 sh_rkv_ref, sh_sm_ref, wkv_ref, scbuf_ref, ssmbuf_ref, ssm_ref,
                 mu_rkv_ref, mu_sm_ref, w0_ref, w2_ref, a0_ref, a2_ref, g2_ref, kk_ref, ka_ref,
                 rk_ref, lnw_ref, lnb_ref,
                 scw_ref, scg_ref, cw_ref, cb_ref, dtb_ref, alog_ref, dskip_ref, ssmg_ref,
                 y_rwkv_ref, y_sc_ref, y_ssm_ref, wkv_out_ref, scbuf_out_ref, ssmbuf_out_ref, ssm_out_ref,
                 even_s, odd_s, yrow_s, yssm_s):
    n = RWKV_HEAD
    half = n // 2
    nseq = p_rkv_ref.shape[1]
    seqs = range(nseq)
    i128 = lambda shape, d: lax.broadcasted_iota(jnp.int32, shape, d)
    ones_pair = (i128((LANE, LANE), 0) // n == i128((LANE, LANE), 1) // n).astype(MXU_DTYPE)
    ones_full = jnp.ones((LANE, LANE), MXU_DTYPE)
    diag2 = i128((half, LANE), 1) % n == 2 * i128((half, LANE), 0) + i128((half, LANE), 1) // n
    diag_lo = i128((n, LANE), 1) == i128((n, LANE), 0)
    diag_hi = i128((n, LANE), 1) == i128((n, LANE), 0) + n
    upper_row = (i128((nseq, RWKV_DIM), 1) // n) % 2 == 1
    upper_lane = i128((1, LANE), 1) // n == 1

    wts = (mu_rkv_ref[...], mu_sm_ref[...], w0_ref[...], w2_ref[...], a0_ref[...], a2_ref[...],
           g2_ref[...], kk_ref[...], ka_ref[...])
    r, log_decay, k_mod, v, kk, a, g = _rwkv_prepare(p_rkv_ref[0], sh_rkv_ref[0], p_sm_ref[0], sh_sm_ref[0], wts)
    kk = kk * lax.rsqrt(jnp.maximum(_head_sums_rows(kk * kk, ones_pair), 1e-24))
    bonus = _head_sums_rows(r * k_mod * rk_ref[...], ones_pair)
    for i, x in enumerate((-kk, jnp.exp(log_decay), kk * a, k_mod, r, v)):
        even_s[i] = jnp.where(upper_row, pltpu.roll(x, n, 1), x)
        odd_s[i] = jnp.where(upper_row, x, pltpu.roll(x, RWKV_DIM - n, 1))

    def head_vec(s, h, i):
        src = odd_s if h % 2 else even_s
        return src[i, s:s + 1, (h // 2) * LANE:(h // 2 + 1) * LANE]

    heads = range(RWKV_HEADS)
    sh = [(s, h) for s in seqs for h in heads]
    rs = lambda x, s, h: x[(s * RWKV_HEADS + h) * half:(s * RWKV_HEADS + h + 1) * half]
    sa = _lane_sums(jnp.concatenate([wkv_ref[s, h] * head_vec(s, h, 0) for s, h in sh], axis=0), ones_pair)
    v_col = _lane_sums(jnp.concatenate([jnp.where(diag2, head_vec(s, h, 5), 0.0) for s, h in sh], axis=0),
                       ones_pair)
    for s, h in sh:
        wkv_out_ref[s, h] = (wkv_ref[s, h] * head_vec(s, h, 1) + rs(sa, s, h) * head_vec(s, h, 2)
                             + rs(v_col, s, h) * head_vec(s, h, 3))
    y_b = _lane_sums(jnp.concatenate([wkv_out_ref[s, h] * head_vec(s, h, 4) for s, h in sh], axis=0), ones_pair)
    y_d = jnp.concatenate([jnp.where(diag2, rs(y_b, s, h), 0.0) for s, h in sh], axis=0)
    nrow = nseq * RWKV_HEADS
    sel = (i128((nrow, nrow * half), 1) // half == i128((nrow, nrow * half), 0)).astype(MXU_DTYPE)
    y_h = _row_sums(sel, y_d)
    y_h = y_h + pltpu.roll(y_h, n, 1)
    for s in seqs:
        base = s * RWKV_HEADS
        yrow_s[s:s + 1, :] = jnp.concatenate(
            [jnp.where(upper_lane, y_h[base + 2 * q + 1:base + 2 * q + 2], y_h[base + 2 * q:base + 2 * q + 1])
             for q in range(RWKV_HEADS // 2)], axis=1)
    y = yrow_s[...]
    mean = _head_sums_rows(y, ones_pair) * (1.0 / n)
    yc = y - mean
    var = _head_sums_rows(yc * yc, ones_pair) * (1.0 / n)
    y_rwkv_ref[0] = (yc * lax.rsqrt(var + RWKV_GN_EPS) * lnw_ref[...] + lnb_ref[...] + bonus * v) * g

    p_sc = p_sc_ref[0]
    u = p_sc[:, SC_DIM:2 * SC_DIM] * p_sc[:, 2 * SC_DIM:]
    buf0 = scbuf_ref[0][:, :SC_DIM]
    buf1 = scbuf_ref[0][:, SC_DIM:]
    conv = buf0 * scw_ref[0:1, :] + buf1 * scw_ref[1:2, :] + u * scw_ref[2:3, :]
    y_sc_ref[0] = _rmsnorm(p_sc[:, :SC_DIM] * conv, scg_ref[...])
    scbuf_out_ref[0, :, :SC_DIM] = buf1
    scbuf_out_ref[0, :, SC_DIM:] = u

    xbc = xbc_ref[0]
    cbuf = ssmbuf_ref[0]
    conv = xbc * cw_ref[3:4, :]
    for j in range(SSM_CONV - 1):
        conv += cbuf[:, j * SSM_CONV_DIM:(j + 1) * SSM_CONV_DIM] * cw_ref[j:j + 1, :]
    ssmbuf_out_ref[0, :, :2 * SSM_CONV_DIM] = cbuf[:, SSM_CONV_DIM:]
    ssmbuf_out_ref[0, :, 2 * SSM_CONV_DIM:] = xbc
    xc = _silu(conv + cb_ref[...])
    xh = xc[:, :SSM_DIM]
    dt = _softplus(p_sm_ref[0][:, SM_DT:SM_DT + LANE] + dtb_ref[...])
    decay_all = jnp.exp(dt * (-jnp.exp(alog_ref[...])))
    head_of_lane = i128((nseq, SSM_DIM), 1) // SSM_HEADDIM
    dt_exp = jnp.zeros((nseq, SSM_DIM), F32)
    d_exp = jnp.zeros((nseq, SSM_DIM), F32)
    for h in range(SSM_HEADS):
        dt_exp = jnp.where(head_of_lane == h, dt[:, h:h + 1], dt_exp)
        d_exp = jnp.where(head_of_lane == h, dskip_ref[:, h:h + 1], d_exp)
    xdt = xh * dt_exp
    heads = range(SSM_HEADS)
    sh = [(s, h) for s in seqs for h in heads]
    heads_per_group = SSM_HEADS // SSM_GROUPS
    bm = lambda s, h: xc[s:s + 1, SSM_DIM + (h // heads_per_group) * SSM_STATE:
                         SSM_DIM + (h // heads_per_group + 1) * SSM_STATE]
    cm = lambda s, h: xc[s:s + 1, SSM_DIM + (SSM_GROUPS + h // heads_per_group) * SSM_STATE:
                         SSM_DIM + (SSM_GROUPS + h // heads_per_group + 1) * SSM_STATE]
    diag = lambda h: diag_hi if h % 2 else diag_lo
    slab = lambda x, s, h: x[s:s + 1, (h // 2) * LANE:(h // 2 + 1) * LANE]
    rs = lambda x, s, h: x[(s * SSM_HEADS + h) * SSM_HEADDIM:(s * SSM_HEADS + h + 1) * SSM_HEADDIM]
    xdt_col = _lane_sums(jnp.concatenate([jnp.where(diag(h), slab(xdt, s, h), 0.0) for s, h in sh], axis=0),
                         ones_full)
    for s, h in sh:
        ssm_out_ref[s, h] = ssm_ref[s, h] * decay_all[s:s + 1, h:h + 1] + rs(xdt_col, s, h) * bm(s, h)
    y_b = _lane_sums(jnp.concatenate([ssm_out_ref[s, h] * cm(s, h) for s, h in sh], axis=0), ones_full)
    y_d = jnp.concatenate([jnp.where(diag(h), rs(y_b, s, h), 0.0) for s, h in sh], axis=0)
    nrow = nseq * SSM_HEADS
    sel = (i128((nrow, nrow * SSM_HEADDIM), 1) // SSM_HEADDIM == i128((nrow, nrow * SSM_HEADDIM), 0)).astype(MXU_DTYPE)
    y_h = _row_sums(sel, y_d)
    for s in seqs:
        base = s * SSM_HEADS
        yssm_s[s:s + 1, :] = jnp.concatenate(
            [y_h[base + 2 * q:base + 2 * q + 1] + y_h[base + 2 * q + 1:base + 2 * q + 2]
             for q in range(SSM_HEADS // 2)], axis=1)
    y = (yssm_s[...] + d_exp * xh) * _silu(z_ref[0])
    gw = SSM_DIM // SSM_GROUPS
    for grp in range(SSM_GROUPS):
        yg = y[:, grp * gw:(grp + 1) * gw]
        y_ssm_ref[0, :, grp * gw:(grp + 1) * gw] = _rmsnorm(yg, ssmg_ref[:, grp * gw:(grp + 1) * gw])


def _mixers_step(proj, states, wp, layer, *, batch):
    shift_rkv, shift_sm, wkv2, sc_buf, ssm_buf, ssm = states
    depth = ssm.shape[0]
    nseq = _tile(batch, STEP_SEQS)
    nb = batch // nseq
    p3 = proj.reshape(nb, nseq, PROJ_COLS)
    grp = lambda a: a.reshape((depth, nb, nseq, -1))
    row = lambda width, blk: pl.BlockSpec((1, nseq, width), lambda b: (b, 0, blk))
    srow = lambda width: pl.BlockSpec((None, 1, nseq, width), lambda b: (layer, b, 0, 0))
    st4 = lambda d1, d2, d3: pl.BlockSpec((nseq, d1, d2, d3), lambda b: (b, 0, 0, 0))
    sst4 = lambda d1, d2, d3: pl.BlockSpec((None, nseq, d1, d2, d3), lambda b: (layer, b, 0, 0, 0))
    vec = lambda width: pl.BlockSpec((None, 1, width), lambda b: (layer, 0, 0))
    mat = lambda rows, width: pl.BlockSpec((None, rows, width), lambda b: (layer, 0, 0))
    sc_w = (SC_WIDTH - 1) * SC_DIM
    ssm_w = (SSM_CONV - 1) * SSM_CONV_DIM
    outs = pl.pallas_call(
        _step_kernel,
        grid=(nb,),
        in_specs=[
            row(3 * RWKV_DIM, COL_RKV // (3 * RWKV_DIM)),
            row(SMALL_W, COL_SMALL // SMALL_W),
            row(3 * SC_DIM, COL_SC // (3 * SC_DIM)),
            row(SSM_DIM, COL_Z // SSM_DIM),
            row(SSM_CONV_DIM, COL_XBC // SSM_CONV_DIM),
            srow(3 * RWKV_DIM), srow(SMALL_W),
            sst4(RWKV_HEADS, RWKV_HEAD // 2, LANE),
            srow(sc_w), srow(ssm_w),
            sst4(SSM_HEADS, SSM_HEADDIM, SSM_STATE),
        ] + [vec(3 * RWKV_DIM), vec(SMALL_W), vec(RWKV_DIM), mat(DECAY_LORA, RWKV_DIM), vec(RWKV_DIM),
             mat(AAA_LORA, RWKV_DIM), mat(GATE_LORA, RWKV_DIM), vec(RWKV_DIM), vec(RWKV_DIM), vec(RWKV_DIM),
             vec(RWKV_DIM), vec(RWKV_DIM),
             mat(SC_WIDTH, SC_DIM), vec(SC_DIM), mat(SSM_CONV, SSM_CONV_DIM), vec(SSM_CONV_DIM),
             vec(LANE), vec(LANE), vec(LANE), vec(SSM_DIM)],
        out_specs=[
            row(RWKV_DIM, 0), row(SC_DIM, 0), row(SSM_DIM, 0),
            st4(RWKV_HEADS, RWKV_HEAD // 2, LANE),
            row(sc_w, 0), row(ssm_w, 0),
            st4(SSM_HEADS, SSM_HEADDIM, SSM_STATE),
        ],
        out_shape=[
            jax.ShapeDtypeStruct((nb, nseq, RWKV_DIM), F32),
            jax.ShapeDtypeStruct((nb, nseq, SC_DIM), F32),
            jax.ShapeDtypeStruct((nb, nseq, SSM_DIM), F32),
            jax.ShapeDtypeStruct(wkv2.shape[1:], F32),
            jax.ShapeDtypeStruct((nb, nseq, sc_w), F32),
            jax.ShapeDtypeStruct((nb, nseq, ssm_w), F32),
            jax.ShapeDtypeStruct(ssm.shape[1:], F32),
        ],
        scratch_shapes=[pltpu.VMEM((6, nseq, RWKV_DIM), F32), pltpu.VMEM((6, nseq, RWKV_DIM), F32),
                        pltpu.VMEM((nseq, RWKV_DIM), F32), pltpu.VMEM((nseq, SSM_DIM), F32)],
        compiler_params=_params("parallel"),
        name="mixers_step",
    )(p3, p3, p3, p3, p3, grp(shift_rkv), grp(shift_sm), wkv2, grp(sc_buf), grp(ssm_buf), ssm,
      *_rwkv_weight_args(wp), wp["sc_conv_w"], wp["sc_norm"], wp["ssm_conv_w"], wp["ssm_conv_b"],
      wp["dt_bias_pad"], wp["a_log_pad"], wp["d_pad"], wp["ssm_norm"])
    y_rwkv, y_sc, y_ssm, wkv_n, sc_n, ssmbuf_n, ssm_n = outs
    return (y_rwkv.reshape(batch, RWKV_DIM), y_sc.reshape(batch, SC_DIM), y_ssm.reshape(batch, SSM_DIM),
            wkv_n, sc_n.reshape(sc_buf.shape[1:]), ssmbuf_n.reshape(ssm_buf.shape[1:]), ssm_n)


def _reorder_cols(a):
    lead = a.shape[:-1]
    rkv = a[..., :3 * RWKV_DIM]
    lora = a[..., 3 * RWKV_DIM:RWKV_PROJ]
    small = jnp.concatenate([lora, jnp.zeros(lead + (SMALL_W - lora.shape[-1],), a.dtype)], axis=-1)
    return rkv, small


def _prepare_weights(w):
    depth = w["w_in"].shape[0]
    wp = {}
    w_in = w["w_in"]
    o_ssm = RWKV_PROJ + 3 * SC_DIM
    o_dt = o_ssm + SSM_DIM + SSM_CONV_DIM
    piece = lambda lo, hi: w_in[..., lo:hi].astype(MXU_DTYPE)
    zeros = lambda width: jnp.zeros(w_in.shape[:-1] + (width,), MXU_DTYPE)
    wp["w_all"] = jnp.concatenate(
        [piece(0, 3 * RWKV_DIM), piece(RWKV_PROJ, o_dt),
         piece(3 * RWKV_DIM, RWKV_PROJ), zeros(SM_DT - (RWKV_PROJ - 3 * RWKV_DIM)),
         piece(o_dt, o_dt + SSM_HEADS), zeros(SMALL_W - SM_DT - SSM_HEADS)],
        axis=-1)
    mu_rkv, mu_sm = _reorder_cols(w["rwkv_mu"])
    wp["mu_rkv"] = mu_rkv.reshape(depth, 1, -1)
    wp["mu_sm"] = mu_sm.reshape(depth, 1, -1)
    for name in ("rwkv_w0", "rwkv_a0", "rwkv_k_k", "rwkv_k_a", "rwkv_ln_w", "rwkv_ln_b", "sc_norm",
                 "ssm_conv_b", "ssm_norm"):
        wp[name] = w[name].reshape(depth, 1, -1)
    wp["rwkv_r_k"] = w["rwkv_r_k"].reshape(depth, 1, RWKV_DIM)
    for name in ("rwkv_w2", "rwkv_a2", "rwkv_g2"):
        wp[name] = w[name].astype(MXU_DTYPE)
    wp["sc_conv_w"] = w["sc_conv_w"]
    wp["ssm_conv_w"] = w["ssm_conv_w"]
    pad_lane = lambda a: jnp.pad(a, ((0, 0), (0, LANE - a.shape[-1]))).reshape(depth, 1, LANE)
    wp["dt_bias_pad"] = pad_lane(w["ssm_dt_bias"])
    wp["a_log_pad"] = pad_lane(w["ssm_A_log"])
    wp["d_pad"] = pad_lane(w["ssm_D"])
    for name in ("ffn1_w_in", "ffn1_w_out", "ffn2_w_in", "ffn2_w_out", "w_out"):
        wp[name] = w[name].astype(MXU_DTYPE)
    for name in ("norm_ffn1", "norm_mix", "norm_ffn2"):
        wp[name] = w[name]
    return wp


def _shift_state_to_cols(proj_last):
    return jnp.concatenate([proj_last[..., :3 * RWKV_DIM],
                            proj_last[..., COL_SMALL:COL_SMALL + RWKV_PROJ - 3 * RWKV_DIM]], axis=-1)


def _tile(m, pref):
    t = min(m, pref)
    while m % t:
        t //= 2
    return t


def _tile_plan(m, seqlen):
    return dict(
        ffn_rows=_tile(m, 512), ffn_cols=D_FF // 2,
        proj_rows=_tile(m, 1024), proj_cols=PROJ_COLS // 4,
        out_rows=_tile(m, 1024),
        conv_rows=_tile(seqlen, 1024),
    )


def _trunk(x3, states, wp, norm_final):
    batch, seqlen, _ = x3.shape
    m = batch * seqlen
    x = x3.reshape(m, D_MODEL)
    depth = wp["w_all"].shape[0]
    tiles = _tile_plan(m, seqlen)
    new = ([], [], [], [], [])
    shift, wkv, sc_buf, ssm_buf, ssm = states
    shift_rkv, shift_sm = _reorder_cols(shift)
    shift_rkv = shift_rkv.reshape(depth, batch, 1, -1)
    shift_sm = shift_sm.reshape(depth, batch, 1, -1)
    if seqlen == 1:
        wkv = wkv.reshape(depth, batch, RWKV_HEADS, RWKV_HEAD // 2, LANE)
    for layer in range(depth):
        x = _ffn(x, wp["norm_ffn1"], wp["ffn1_w_in"], wp["ffn1_w_out"], layer,
                 tm=tiles["ffn_rows"], tf=tiles["ffn_cols"])
        proj = _proj_in(x, wp["norm_mix"], wp["w_all"], layer, tm=tiles["proj_rows"], tn=tiles["proj_cols"])
        if seqlen == 1:
            y_rwkv, y_sc, y_ssm, wkv_n, sc_n, ssmbuf_n, ssm_n = _mixers_step(
                proj, (shift_rkv, shift_sm, wkv, sc_buf, ssm_buf, ssm), wp, layer, batch=batch)
        else:
            y_rwkv, wkv_n = _rwkv_prompt(proj, shift_rkv, shift_sm, wkv, wp, layer, batch=batch, seqlen=seqlen)
            y_sc, sc_n = _sc_prompt(proj, sc_buf, wp, layer, batch=batch, seqlen=seqlen,
                                    tl=tiles["conv_rows"])
            y_ssm, ssmbuf_n, ssm_n = _ssd_prompt(proj, ssm_buf, ssm, wp, layer, batch=batch, seqlen=seqlen)
        shift_n = _shift_state_to_cols(proj.reshape(batch, seqlen, PROJ_COLS)[:, -1])
        x = _proj_out(x, y_rwkv, y_sc, y_ssm, wp["w_out"], layer, tm=tiles["out_rows"])
        x = _ffn(x, wp["norm_ffn2"], wp["ffn2_w_in"], wp["ffn2_w_out"], layer,
                 final_g=norm_final if layer == depth - 1 else None,
                 tm=tiles["ffn_rows"], tf=tiles["ffn_cols"])
        for lst, s in zip(new, (shift_n, wkv_n, sc_n, ssmbuf_n, ssm_n)):
            lst.append(s)
    new = [jnp.stack(lst) for lst in new]
    new[1] = new[1].reshape(depth, batch, RWKV_HEADS, RWKV_HEAD, RWKV_HEAD)
    return x.reshape(batch, seqlen, D_MODEL), tuple(new)


def kernel(x_prompt, x_sample, state_rwkv_shift, state_rwkv_wkv, state_sc_buf, state_ssm_conv, state_ssm,
           norm_ffn1, ffn1_w_in, ffn1_w_out, norm_mix, w_in, rwkv_mu, rwkv_w0, rwkv_w2, rwkv_a0, rwkv_a2,
           rwkv_g2, rwkv_k_k, rwkv_k_a, rwkv_r_k, rwkv_ln_w, rwkv_ln_b, sc_conv_w, sc_norm, ssm_conv_w,
           ssm_conv_b, ssm_dt_bias, ssm_A_log, ssm_D, ssm_norm, w_out, norm_ffn2, ffn2_w_in, ffn2_w_out,
           norm_final):
    weights = dict(
        norm_ffn1=norm_ffn1, ffn1_w_in=ffn1_w_in, ffn1_w_out=ffn1_w_out, norm_mix=norm_mix, w_in=w_in,
        rwkv_mu=rwkv_mu, rwkv_w0=rwkv_w0, rwkv_w2=rwkv_w2, rwkv_a0=rwkv_a0, rwkv_a2=rwkv_a2, rwkv_g2=rwkv_g2,
        rwkv_k_k=rwkv_k_k, rwkv_k_a=rwkv_k_a, rwkv_r_k=rwkv_r_k, rwkv_ln_w=rwkv_ln_w, rwkv_ln_b=rwkv_ln_b,
        sc_conv_w=sc_conv_w, sc_norm=sc_norm, ssm_conv_w=ssm_conv_w, ssm_conv_b=ssm_conv_b,
        ssm_dt_bias=ssm_dt_bias, ssm_A_log=ssm_A_log, ssm_D=ssm_D, ssm_norm=ssm_norm, w_out=w_out,
        norm_ffn2=norm_ffn2, ffn2_w_in=ffn2_w_in, ffn2_w_out=ffn2_w_out)
    wp = _prepare_weights(weights)
    depth = w_in.shape[0]
    nb, dt_ = x_prompt.shape[0], x_prompt.dtype
    zero_states = (
        jnp.zeros((depth, nb, RWKV_PROJ), dt_),
        jnp.zeros((depth, nb, RWKV_HEADS, RWKV_HEAD, RWKV_HEAD), dt_),
        jnp.zeros((depth, nb, SC_WIDTH - 1, SC_DIM), dt_),
        jnp.zeros((depth, nb, SSM_CONV - 1, SSM_CONV_DIM), dt_),
        jnp.zeros((depth, nb, SSM_HEADS, SSM_HEADDIM, SSM_STATE), dt_),
    )
    y_prompt, p_states = _trunk(x_prompt, zero_states, wp, norm_final)
    sample_states = (state_rwkv_shift, state_rwkv_wkv, state_sc_buf, state_ssm_conv, state_ssm)
    y_sample, s_states = _trunk(x_sample, sample_states, wp, norm_final)
    return (y_prompt, y_sample) + tuple(p_states) + tuple(s_states)
```

```python
import functools

import jax
import jax.numpy as jnp
from jax import lax
from jax.experimental import pallas as pl
from jax.experimental.pallas import tpu as pltpu

F32 = jnp.float32
MXU_DTYPE = jnp.bfloat16

D_MODEL = 1024
D_FF = 2816
RWKV_DIM = 1024
RWKV_HEAD = 64
RWKV_HEADS = 16
DECAY_LORA = 64
AAA_LORA = 64
GATE_LORA = 160
RWKV_PROJ = 3 * RWKV_DIM + DECAY_LORA + AAA_LORA + GATE_LORA
RWKV_GN_EPS = 64e-5
DECAY_SCALE = 0.6065306597126334
SC_DIM = 512
SC_WIDTH = 3
SSM_DIM = 512
SSM_HEADDIM = 64
SSM_HEADS = 8
SSM_GROUPS = 2
SSM_STATE = 128
SSM_CONV = 4
SSM_CONV_DIM = SSM_DIM + 2 * SSM_GROUPS * SSM_STATE
NORM_EPS = 1e-6

COL_RKV = 0
COL_SC = 3 * RWKV_DIM
COL_Z = COL_SC + 3 * SC_DIM
COL_XBC = COL_Z + SSM_DIM
COL_SMALL = COL_XBC + SSM_CONV_DIM
SMALL_W = 512
SM_GL = 128
SM_DT = 384
PROJ_COLS = COL_SMALL + SMALL_W

LANE = 128
SUBLANE = 8
VMEM_LIMIT = 56 * 1024 * 1024

RWKV_CHUNK = 64
RWKV_CHUNKS_PER_STEP = 2
INV_BASE = 8
STEP_SEQS = 8
SSD_CHUNK = 128
SSD_CHUNKS_PER_STEP = 1


def _mm(a, b):
    return jnp.dot(a.astype(MXU_DTYPE), b.astype(MXU_DTYPE), preferred_element_type=F32)


def _mm_nt(a, b):
    return lax.dot_general(a.astype(MXU_DTYPE), b.astype(MXU_DTYPE), (((1,), (1,)), ((), ())),
                           preferred_element_type=F32)


def _mm_tn(a, b):
    return lax.dot_general(a.astype(MXU_DTYPE), b.astype(MXU_DTYPE), (((0,), (0,)), ((), ())),
                           preferred_element_type=F32)


def _sigmoid(x):
    return 1.0 / (1.0 + jnp.exp(-x))


def _silu(x):
    return x * _sigmoid(x)


def _softplus(x):
    return jnp.maximum(x, 0.0) + jnp.log1p(jnp.exp(-jnp.abs(x)))


def _rmsnorm(x, g):
    return x * lax.rsqrt(jnp.mean(x * x, axis=-1, keepdims=True) + NORM_EPS) * g


def _params(*sem):
    return pltpu.CompilerParams(dimension_semantics=sem, vmem_limit_bytes=VMEM_LIMIT)


def _ffn_kernel(x_ref, g_ref, wg_ref, wu_ref, wo_ref, gf_ref, o_ref, h_ref, acc_ref, *, final_norm):
    j = pl.program_id(1)

    @pl.when(j == 0)
    def _():
        h_ref[...] = _rmsnorm(x_ref[...], g_ref[...]).astype(h_ref.dtype)
        acc_ref[...] = jnp.zeros_like(acc_ref)

    h = h_ref[...]
    gate = jnp.dot(h, wg_ref[...], preferred_element_type=F32)
    up = jnp.dot(h, wu_ref[...], preferred_element_type=F32)
    act = (_silu(gate) * up).astype(MXU_DTYPE)
    acc_ref[...] += jnp.dot(act, wo_ref[...], preferred_element_type=F32)

    @pl.when(j == pl.num_programs(1) - 1)
    def _():
        y = x_ref[...] + 0.5 * acc_ref[...]
        if final_norm:
            y = _rmsnorm(y, gf_ref[...])
        o_ref[...] = y


def _ffn(x, norm_g, w_in, w_out, layer, final_g=None, *, tm, tf):
    m = x.shape[0]
    nf = D_FF // tf
    final_norm = final_g is not None
    gf = final_g if final_norm else norm_g[layer]
    return pl.pallas_call(
        functools.partial(_ffn_kernel, final_norm=final_norm),
        grid=(m // tm, nf),
        in_specs=[
            pl.BlockSpec((tm, D_MODEL), lambda i, j: (i, 0)),
            pl.BlockSpec((None, 1, D_MODEL), lambda i, j: (layer, 0, 0)),
            pl.BlockSpec((None, D_MODEL, tf), lambda i, j: (layer, 0, j)),
            pl.BlockSpec((None, D_MODEL, tf), lambda i, j: (layer, 0, nf + j)),
            pl.BlockSpec((None, tf, D_MODEL), lambda i, j: (layer, j, 0)),
            pl.BlockSpec((1, D_MODEL), lambda i, j: (0, 0)),
        ],
        out_specs=pl.BlockSpec((tm, D_MODEL), lambda i, j: (i, 0)),
        out_shape=jax.ShapeDtypeStruct((m, D_MODEL), F32),
        scratch_shapes=[pltpu.VMEM((tm, D_MODEL), MXU_DTYPE), pltpu.VMEM((tm, D_MODEL), F32)],
        compiler_params=_params("parallel", "arbitrary"),
        name="ffn",
    )(x, norm_g.reshape(-1, 1, D_MODEL), w_in, w_in, w_out, gf.reshape(1, D_MODEL))


def _proj_kernel(x_ref, g_ref, w_ref, o_ref, h_ref):
    @pl.when(pl.program_id(1) == 0)
    def _():
        h_ref[...] = _rmsnorm(x_ref[...], g_ref[...]).astype(h_ref.dtype)

    o_ref[...] = jnp.dot(h_ref[...], w_ref[...], preferred_element_type=F32)


def _proj_in(x, norm_g, w_all, layer, *, tm, tn):
    m = x.shape[0]
    return pl.pallas_call(
        _proj_kernel,
        grid=(m // tm, PROJ_COLS // tn),
        in_specs=[
            pl.BlockSpec((tm, D_MODEL), lambda i, j: (i, 0)),
            pl.BlockSpec((None, 1, D_MODEL), lambda i, j: (layer, 0, 0)),
            pl.BlockSpec((None, D_MODEL, tn), lambda i, j: (layer, 0, j)),
        ],
        out_specs=pl.BlockSpec((tm, tn), lambda i, j: (i, j)),
        out_shape=jax.ShapeDtypeStruct((m, PROJ_COLS), F32),
        scratch_shapes=[pltpu.VMEM((tm, D_MODEL), MXU_DTYPE)],
        compiler_params=_params("parallel", "arbitrary"),
        name="proj_in",
    )(x, norm_g.reshape(-1, 1, D_MODEL), w_all)


def _proj_out_kernel(x_ref, y1_ref, y2_ref, y3_ref, w1_ref, w2_ref, w3_ref, o_ref):
    acc = jnp.dot(y1_ref[...].astype(MXU_DTYPE), w1_ref[...], preferred_element_type=F32)
    acc += jnp.dot(y2_ref[...].astype(MXU_DTYPE), w2_ref[...], preferred_element_type=F32)
    acc += jnp.dot(y3_ref[...].astype(MXU_DTYPE), w3_ref[...], preferred_element_type=F32)
    o_ref[...] = x_ref[...] + acc


def _proj_out(x, y_rwkv, y_sc, y_ssm, w_out, layer, *, tm):
    m = x.shape[0]
    return pl.pallas_call(
        _proj_out_kernel,
        grid=(m // tm,),
        in_specs=[
            pl.BlockSpec((tm, D_MODEL), lambda i: (i, 0)),
            pl.BlockSpec((tm, RWKV_DIM), lambda i: (i, 0)),
            pl.BlockSpec((tm, SC_DIM), lambda i: (i, 0)),
            pl.BlockSpec((tm, SSM_DIM), lambda i: (i, 0)),
            pl.BlockSpec((None, RWKV_DIM, D_MODEL), lambda i: (layer, 0, 0)),
            pl.BlockSpec((None, SC_DIM, D_MODEL), lambda i: (layer, RWKV_DIM // SC_DIM, 0)),
            pl.BlockSpec((None, SSM_DIM, D_MODEL), lambda i: (layer, (RWKV_DIM + SC_DIM) // SSM_DIM, 0)),
        ],
        out_specs=pl.BlockSpec((tm, D_MODEL), lambda i: (i, 0)),
        out_shape=jax.ShapeDtypeStruct((m, D_MODEL), F32),
        compiler_params=_params("parallel"),
        name="proj_out",
    )(x, y_rwkv, y_sc, y_ssm, w_out, w_out, w_out)


def _previous_rows(x, carry_row):
    first = lax.broadcasted_iota(jnp.int32, x.shape, 0) == 0
    return jnp.where(first, carry_row, pltpu.roll(x, 1, 0))


def _rows_shifted(x, tail, s):
    rolled = pltpu.roll(x, s, 0)
    first = lax.broadcasted_iota(jnp.int32, tail.shape, 0) < s
    head = jnp.where(first, pltpu.roll(tail, s, 0), rolled[:SUBLANE])
    return jnp.concatenate([head, rolled[SUBLANE:]], axis=0)


def _rwkv_prepare(p_rkv, prev_rkv, p_sm, prev_sm, wts):
    (mu_rkv, mu_sm, w0, w2, a0, a2, g2, k_k, k_a) = wts
    xs = p_rkv + (prev_rkv - p_rkv) * mu_rkv
    xm = p_sm + (prev_sm - p_sm) * mu_sm
    r = xs[:, :RWKV_DIM]
    k = xs[:, RWKV_DIM:2 * RWKV_DIM]
    v = xs[:, 2 * RWKV_DIM:]
    w_lr = xm[:, :DECAY_LORA]
    a_lr = xm[:, DECAY_LORA:DECAY_LORA + AAA_LORA]
    g_lr = xm[:, SM_GL:SM_GL + GATE_LORA]
    log_decay = -DECAY_SCALE * _sigmoid(w0 + _mm(jnp.tanh(w_lr), w2))
    a = _sigmoid(a0 + _mm(a_lr, a2))
    g = _mm(_sigmoid(g_lr), g2)
    kk = k * k_k
    k_mod = k * (1.0 + (a - 1.0) * k_a)
    return r, log_decay, k_mod, v, kk, a, g


def _rwkv_weight_specs(layer):
    def vec(width):
        return pl.BlockSpec((None, 1, width), lambda *idx: (layer, 0, 0))

    def mat(rows):
        return pl.BlockSpec((None, rows, RWKV_DIM), lambda *idx: (layer, 0, 0))

    return [vec(3 * RWKV_DIM), vec(SMALL_W), vec(RWKV_DIM), mat(DECAY_LORA), vec(RWKV_DIM), mat(AAA_LORA),
            mat(GATE_LORA), vec(RWKV_DIM), vec(RWKV_DIM), vec(RWKV_DIM), vec(RWKV_DIM), vec(RWKV_DIM)]


def _rwkv_weight_args(wp):
    return (wp["mu_rkv"], wp["mu_sm"], wp["rwkv_w0"], wp["rwkv_w2"], wp["rwkv_a0"], wp["rwkv_a2"],
            wp["rwkv_g2"], wp["rwkv_k_k"], wp["rwkv_k_a"], wp["rwkv_r_k"], wp["rwkv_ln_w"], wp["rwkv_ln_b"])


HEADS_PER_GROUP = 4
GROUP_W = HEADS_PER_GROUP * RWKV_HEAD
RWKV_GROUPS = RWKV_HEADS // HEADS_PER_GROUP


def _split3(x):
    hi = x.astype(MXU_DTYPE)
    r1 = x - hi.astype(F32)
    mid = r1.astype(MXU_DTYPE)
    lo = (r1 - mid.astype(F32)).astype(MXU_DTYPE)
    return hi, mid, lo


def _block_diag(x, head_masks):
    return jnp.concatenate([x * m for m in head_masks], axis=0)


def _rwkv_chunk_kernel(p_rkv_ref, p_sm_ref, sh_rkv_ref, sh_sm_ref, s0_ref,
                       mu_rkv_ref, mu_sm_ref, w0_ref, w2_ref, a0_ref, a2_ref, g2_ref, kk_ref, ka_ref,
                       rk_ref, lnw_ref, lnb_ref,
                       y_ref, s_out_ref,
                       pad_rkv, pad_sm, state, at_s, rt_s, bt_s, kt_s, bh_s, kh_s, v_s, yn_s,
                       hm_s, ones_s, tri_s):
    c = RWKV_CHUNK
    nc = RWKV_CHUNKS_PER_STEP
    tl = nc * c
    n = RWKV_HEAD
    gw = GROUP_W
    t = pl.program_id(1)

    @pl.when(t == 0)
    def _():
        pad_rkv[...] = sh_rkv_ref[0]
        pad_sm[...] = sh_sm_ref[0]
        for h in range(RWKV_HEADS):
            state[:, h * n:(h + 1) * n] = s0_ref[0, h]
        lane_head = lax.broadcasted_iota(jnp.int32, (c, gw), 1) // n
        for hh in range(HEADS_PER_GROUP):
            hm_s[hh] = (lane_head == hh).astype(MXU_DTYPE)
        sq_r = lax.broadcasted_iota(jnp.int32, (gw, gw), 0) // n
        sq_c = lax.broadcasted_iota(jnp.int32, (gw, gw), 1) // n
        ones_s[...] = (sq_r == sq_c).astype(MXU_DTYPE)
        tri_r = lax.broadcasted_iota(jnp.int32, (tl, tl), 0)
        tri_c = lax.broadcasted_iota(jnp.int32, (tl, tl), 1)
        tri_s[...] = ((tri_r >= tri_c) & (tri_r // c == tri_c // c)).astype(MXU_DTYPE)

    wts = (mu_rkv_ref[...], mu_sm_ref[...], w0_ref[...], w2_ref[...], a0_ref[...], a2_ref[...],
           g2_ref[...], kk_ref[...], ka_ref[...])
    r, log_decay, k_mod, v, kk, a, g = _rwkv_prepare(
        p_rkv_ref[...], _previous_rows(p_rkv_ref[...], pad_rkv[...]),
        p_sm_ref[...], _previous_rows(p_sm_ref[...], pad_sm[...]), wts)
    pad_rkv[...] = p_rkv_ref[tl - 1:tl, :]
    pad_sm[...] = p_sm_ref[tl - 1:tl, :]

    row = lax.broadcasted_iota(jnp.int32, (c, gw), 0)
    lane = lax.broadcasted_iota(jnp.int32, (c, gw), 1)
    pos = lane % n
    strict = row > pos
    incl = row >= pos
    eye = (row == pos).astype(F32)
    head_masks = [hm_s[hh] for hh in range(HEADS_PER_GROUP)]
    ones_blk = ones_s[...]
    tri = tri_s[...]

    def head_sum(x):
        return jnp.concatenate(
            [jnp.dot(x[:, gi * gw:(gi + 1) * gw].astype(MXU_DTYPE), ones_blk, preferred_element_type=F32)
             for gi in range(RWKV_GROUPS)], axis=1)

    hi, mid, _ = _split3(log_decay)
    cum = (jnp.dot(tri, hi, preferred_element_type=F32)
           + jnp.dot(tri, mid, preferred_element_type=F32))
    cum_end = jnp.concatenate([jnp.broadcast_to(cum[(ci + 1) * c - 1:(ci + 1) * c, :], (c, RWKV_DIM))
                               for ci in range(nc)], axis=0)
    e_out = jnp.exp(-cum)
    e_end = jnp.exp(cum_end - cum)
    kk = kk * lax.rsqrt(jnp.maximum(head_sum(kk * kk), 1e-24))
    b = kk * a
    at_s[...] = (-kk * jnp.exp(cum - log_decay)).astype(MXU_DTYPE)
    rt_s[...] = (r * jnp.exp(cum)).astype(MXU_DTYPE)
    bt_s[...] = (b * e_out).astype(MXU_DTYPE)
    kt_s[...] = (k_mod * e_out).astype(MXU_DTYPE)
    bh_s[...] = (b * e_end).astype(MXU_DTYPE)
    kh_s[...] = (k_mod * e_end).astype(MXU_DTYPE)
    v_s[...] = v
    bonus = head_sum(r * k_mod * rk_ref[...])
    w_end = jnp.exp(cum_end)

    def mm(x, y):
        return jnp.dot(x.astype(MXU_DTYPE), y, preferred_element_type=F32)

    def bd(x):
        return _block_diag(x.astype(MXU_DTYPE), head_masks)

    groups = range(RWKV_GROUPS)
    chunks = range(nc)
    pairs = [(ci, gi) for ci in chunks for gi in groups]
    rws = [slice(ci * c, (ci + 1) * c) for ci in chunks]
    sls = [slice(gi * gw, (gi + 1) * gw) for gi in groups]
    nt = (((1,), (1,)), ((), ()))
    ar = {(ci, gi): jnp.concatenate([at_s[rws[ci], sls[gi]], rt_s[rws[ci], sls[gi]]], axis=0)
          for ci, gi in pairs}
    vg = {(ci, gi): v_s[rws[ci], sls[gi]].astype(MXU_DTYPE) for ci, gi in pairs}
    gram_b = {(ci, gi): lax.dot_general(ar[ci, gi], bd(bt_s[rws[ci], sls[gi]]), nt, preferred_element_type=F32)
              for ci, gi in pairs}
    gram_k = {(ci, gi): lax.dot_general(ar[ci, gi], bd(kt_s[rws[ci], sls[gi]]), nt, preferred_element_type=F32)
              for ci, gi in pairs}
    l_ab = {p: jnp.where(strict, gram_b[p][:c], 0.0) for p in pairs}
    m_rb = {p: jnp.where(incl, gram_b[p][c:], 0.0) for p in pairs}
    l_akrk = {p: jnp.where(jnp.concatenate([strict, incl], axis=0), gram_k[p], 0.0) for p in pairs}
    def same_block(width):
        return row // width == pos // width

    base = {p: jnp.where(same_block(INV_BASE), l_ab[p], 0.0) for p in pairs}
    inv = {p: eye + base[p] for p in pairs}
    power = base
    power_bd = {p: bd(power[p]) for p in pairs}
    span = 2
    while span < INV_BASE:
        power = {p: mm(power[p], power_bd[p]) for p in pairs}
        power_bd = {p: bd(power[p]) for p in pairs}
        inv = {p: inv[p] + mm(inv[p], power_bd[p]) for p in pairs}
        span *= 2
    width = 2 * INV_BASE
    while width <= c:
        below = same_block(width) & jnp.logical_not(same_block(width // 2))
        sub = {p: mm(inv[p], bd(jnp.where(below, l_ab[p], 0.0))) for p in pairs}
        inv = {p: inv[p] + mm(sub[p], bd(inv[p])) for p in pairs}
        width *= 2
    y = {}
    for ci in chunks:
        s0 = [state[:, sl] for sl in sls]
        z = [mm(l_akrk[ci, gi], bd(vg[ci, gi]))
             + lax.dot_general(ar[ci, gi], bd(s0[gi]), nt, preferred_element_type=F32) for gi in groups]
        u = [mm(inv[ci, gi], bd(z[gi][:c])) for gi in groups]
        for gi in groups:
            y[ci, gi] = z[gi][c:] + mm(m_rb[ci, gi], bd(u[gi]))
        for gi in groups:
            uv = jnp.concatenate([u[gi].astype(MXU_DTYPE), vg[ci, gi]], axis=0)
            bk = jnp.concatenate([bh_s[rws[ci], sls[gi]], kh_s[rws[ci], sls[gi]]], axis=0)
            upd = lax.dot_general(uv, bk, (((0,), (0,)), ((), ())), preferred_element_type=F32)
            s_new = s0[gi] * w_end[ci * c:ci * c + 1, sls[gi]]
            for hh in range(HEADS_PER_GROUP):
                s_new += upd[hh * n:(hh + 1) * n, :] * head_masks[hh].astype(F32)
            state[:, sls[gi]] = s_new
    mean = {p: mm(y[p], ones_blk) * (1.0 / n) for p in pairs}
    yc = {p: y[p] - mean[p] for p in pairs}
    var = {p: mm(yc[p] * yc[p], ones_blk) * (1.0 / n) for p in pairs}
    for ci, gi in pairs:
        yn_s[rws[ci], sls[gi]] = yc[ci, gi] * lax.rsqrt(var[ci, gi] + RWKV_GN_EPS)

    y_ref[...] = (yn_s[...] * lnw_ref[...] + lnb_ref[...] + bonus * v_s[...]) * g

    @pl.when(t == pl.num_programs(1) - 1)
    def _():
        for h in range(RWKV_HEADS):
            s_out_ref[0, h] = state[:, h * n:(h + 1) * n]


def _rwkv_prompt(proj, shift_rkv, shift_sm, wkv0, wp, layer, *, batch, seqlen):
    c = RWKV_CHUNK * RWKV_CHUNKS_PER_STEP
    nt = seqlen // c
    return pl.pallas_call(
        _rwkv_chunk_kernel,
        grid=(batch, nt),
        in_specs=[
            pl.BlockSpec((c, 3 * RWKV_DIM), lambda b, t: (b * nt + t, COL_RKV // (3 * RWKV_DIM))),
            pl.BlockSpec((c, SMALL_W), lambda b, t: (b * nt + t, COL_SMALL // SMALL_W)),
            pl.BlockSpec((None, 1, 1, 3 * RWKV_DIM), lambda b, t: (layer, b, 0, 0)),
            pl.BlockSpec((None, 1, 1, SMALL_W), lambda b, t: (layer, b, 0, 0)),
            pl.BlockSpec((None, 1, RWKV_HEADS, RWKV_HEAD, RWKV_HEAD), lambda b, t: (layer, b, 0, 0, 0)),
        ] + _rwkv_weight_specs(layer),
        out_specs=[
            pl.BlockSpec((c, RWKV_DIM), lambda b, t: (b * nt + t, 0)),
            pl.BlockSpec((1, RWKV_HEADS, RWKV_HEAD, RWKV_HEAD), lambda b, t: (b, 0, 0, 0)),
        ],
        out_shape=[
            jax.ShapeDtypeStruct((batch * seqlen, RWKV_DIM), F32),
            jax.ShapeDtypeStruct((batch, RWKV_HEADS, RWKV_HEAD, RWKV_HEAD), F32),
        ],
        scratch_shapes=[
            pltpu.VMEM((1, 3 * RWKV_DIM), F32),
            pltpu.VMEM((1, SMALL_W), F32),
            pltpu.VMEM((RWKV_HEAD, RWKV_DIM), F32),
        ] + [pltpu.VMEM((c, RWKV_DIM), MXU_DTYPE) for _ in range(6)]
          + [pltpu.VMEM((c, RWKV_DIM), F32) for _ in range(2)]
          + [pltpu.VMEM((HEADS_PER_GROUP, RWKV_CHUNK, GROUP_W), MXU_DTYPE),
             pltpu.VMEM((GROUP_W, GROUP_W), MXU_DTYPE), pltpu.VMEM((c, c), MXU_DTYPE)],
        compiler_params=_params("parallel", "arbitrary"),
        name="rwkv_chunk",
    )(proj, proj, shift_rkv, shift_sm, wkv0, *_rwkv_weight_args(wp))


def _sc_kernel(p_ref, buf_ref, w_ref, g_ref, y_ref, buf_out_ref, pad, *, tl):
    t = pl.program_id(1)
    hist = SC_WIDTH - 1

    @pl.when(t == 0)
    def _():
        pad[...] = jnp.zeros_like(pad)
        pad[SUBLANE - hist:SUBLANE, :] = buf_ref[0]

    b_gate = p_ref[:, :SC_DIM]
    u = p_ref[:, SC_DIM:2 * SC_DIM] * p_ref[:, 2 * SC_DIM:]
    tail = pad[...]
    conv = u * w_ref[hist:hist + 1, :]
    for j in range(hist):
        conv += _rows_shifted(u, tail, hist - j) * w_ref[j:j + 1, :]
    y_ref[...] = _rmsnorm(b_gate * conv, g_ref[...])
    pad[...] = u[tl - SUBLANE:, :]

    @pl.when(t == pl.num_programs(1) - 1)
    def _():
        buf_out_ref[0] = pad[SUBLANE - hist:SUBLANE, :]


def _sc_prompt(proj, buf0, wp, layer, *, batch, seqlen, tl):
    nt = seqlen // tl
    hist = SC_WIDTH - 1
    return pl.pallas_call(
        functools.partial(_sc_kernel, tl=tl),
        grid=(batch, nt),
        in_specs=[
            pl.BlockSpec((tl, 3 * SC_DIM), lambda b, t: (b * nt + t, COL_SC // (3 * SC_DIM))),
            pl.BlockSpec((None, 1, hist, SC_DIM), lambda b, t: (layer, b, 0, 0)),
            pl.BlockSpec((None, SC_WIDTH, SC_DIM), lambda b, t: (layer, 0, 0)),
            pl.BlockSpec((None, 1, SC_DIM), lambda b, t: (layer, 0, 0)),
        ],
        out_specs=[
            pl.BlockSpec((tl, SC_DIM), lambda b, t: (b * nt + t, 0)),
            pl.BlockSpec((1, hist, SC_DIM), lambda b, t: (b, 0, 0)),
        ],
        out_shape=[
            jax.ShapeDtypeStruct((batch * seqlen, SC_DIM), F32),
            jax.ShapeDtypeStruct((batch, hist, SC_DIM), F32),
        ],
        scratch_shapes=[pltpu.VMEM((SUBLANE, SC_DIM), F32)],
        compiler_params=_params("parallel", "arbitrary"),
        name="short_conv",
    )(proj, buf0, wp["sc_conv_w"], wp["sc_norm"])


def _ssd_kernel(z_ref, xbc_ref, dt_ref, buf_ref, h0_ref, cw_ref, cb_ref, dtb_ref, alog_ref, dskip_ref, g_ref,
                y_ref, buf_out_ref, h_out_ref, pad, state, y_s, exh_s, exs_s):
    q = SSD_CHUNK
    nq = SSD_CHUNKS_PER_STEP
    tq = nq * q
    t = pl.program_id(1)
    hist = SSM_CONV - 1
    gw = SSM_DIM // SSM_GROUPS

    @pl.when(t == 0)
    def _():
        pad[...] = jnp.zeros_like(pad)
        pad[SUBLANE - hist:SUBLANE, :] = buf_ref[0]
        state[...] = h0_ref[0]
        head_row = lax.broadcasted_iota(jnp.int32, (LANE, SSM_DIM), 0)
        exh_s[...] = (lax.broadcasted_iota(jnp.int32, (LANE, SSM_DIM), 1) // SSM_HEADDIM == head_row).astype(MXU_DTYPE)
        seg_row = lax.broadcasted_iota(jnp.int32, (LANE, SSM_HEADS * q), 0)
        exs_s[...] = (lax.broadcasted_iota(jnp.int32, (LANE, SSM_HEADS * q), 1) // q == seg_row).astype(MXU_DTYPE)

    xbc = xbc_ref[...]
    tail = pad[...]
    conv = xbc * cw_ref[hist:hist + 1, :]
    for j in range(hist):
        conv += _rows_shifted(xbc, tail, hist - j) * cw_ref[j:j + 1, :]
    pad[...] = xbc[tq - SUBLANE:, :]
    xc = _silu(conv + cb_ref[...])
    xh = xc[:, :SSM_DIM]

    dt = _softplus(dt_ref[...] + dtb_ref[...])
    neg_a = -jnp.exp(alog_ref[...])
    row = lax.broadcasted_iota(jnp.int32, (tq, tq), 0)
    col = lax.broadcasted_iota(jnp.int32, (tq, tq), 1)
    tri = ((row >= col) & (row // q == col // q)).astype(MXU_DTYPE)
    hi, mid, lo = _split3(dt * neg_a)
    cum = (jnp.dot(tri, hi, preferred_element_type=F32) + jnp.dot(tri, mid, preferred_element_type=F32)
           + jnp.dot(tri, lo, preferred_element_type=F32))
    cum_t = cum.T
    cum_last = jnp.concatenate([jnp.broadcast_to(cum[(ci + 1) * q - 1:(ci + 1) * q, :], (q, LANE))
                                for ci in range(nq)], axis=0)

    def expand(x, e):
        h3 = _split3(x)
        return sum(jnp.dot(part, e, preferred_element_type=F32) for part in h3)

    ex_head = exh_s[...]
    ex_seg = exs_s[...]
    xdt_all = xh * expand(dt, ex_head)
    xdt_end = xdt_all * expand(jnp.exp(cum_last - cum), ex_head)
    ecum_x = expand(jnp.exp(cum), ex_head)
    cum_seg = expand(cum, ex_seg)
    causal = lax.broadcasted_iota(jnp.int32, (q, q), 0) >= lax.broadcasted_iota(jnp.int32, (q, q), 1)
    heads_per_group = SSM_HEADS // SSM_GROUPS

    heads = range(SSM_HEADS)
    chunks = range(nq)
    pairs = [(ci, h) for ci in chunks for h in heads]
    grp_of = [h // heads_per_group for h in heads]
    rws = [slice(ci * q, (ci + 1) * q) for ci in chunks]
    sls = [slice(h * SSM_HEADDIM, (h + 1) * SSM_HEADDIM) for h in heads]
    bm = {(ci, g): xc[rws[ci], SSM_DIM + g * SSM_STATE:SSM_DIM + (g + 1) * SSM_STATE].astype(MXU_DTYPE)
          for ci in chunks for g in range(SSM_GROUPS)}
    cm = {(ci, g): xc[rws[ci], SSM_DIM + (SSM_GROUPS + g) * SSM_STATE:SSM_DIM + (SSM_GROUPS + g + 1) * SSM_STATE]
          .astype(MXU_DTYPE) for ci in chunks for g in range(SSM_GROUPS)}
    gram = {k: _mm_nt(cm[k], bm[k]) for k in bm}
    cum_end = {(ci, h): cum[(ci + 1) * q - 1:(ci + 1) * q, h:h + 1] for ci, h in pairs}
    seg = {(ci, h): cum_seg[rws[ci], h * q:(h + 1) * q] - cum_t[h:h + 1, rws[ci]] for ci, h in pairs}
    decay = {p: jnp.where(causal, jnp.exp(jnp.where(causal, seg[p], 0.0)), 0.0) for p in pairs}
    xdt = {(ci, h): xdt_all[rws[ci], sls[h]] for ci, h in pairs}
    y = {(ci, h): _mm(gram[ci, grp_of[h]] * decay[ci, h], xdt[ci, h]) for ci, h in pairs}
    upd = {(ci, h): _mm_tn(xdt_end[rws[ci], sls[h]], bm[ci, grp_of[h]]) for ci, h in pairs}
    for ci in chunks:
        y_off = [_mm_nt(cm[ci, grp_of[h]], state[h]) * ecum_x[rws[ci], sls[h]] for h in heads]
        for h in heads:
            state[h] = state[h] * jnp.exp(cum_end[ci, h]) + upd[ci, h]
            y_s[rws[ci], sls[h]] = y[ci, h] + y_off[h]

    y = (y_s[...] + expand(dskip_ref[...], ex_head) * xh) * _silu(z_ref[...])
    for grp in range(SSM_GROUPS):
        yg = y[:, grp * gw:(grp + 1) * gw]
        y_ref[:, grp * gw:(grp + 1) * gw] = _rmsnorm(yg, g_ref[:, grp * gw:(grp + 1) * gw])

    @pl.when(t == pl.num_programs(1) - 1)
    def _():
        buf_out_ref[0] = pad[SUBLANE - hist:SUBLANE, :]
        h_out_ref[0] = state[...]


def _ssd_prompt(proj, buf0, h0, wp, layer, *, batch, seqlen):
    q = SSD_CHUNK * SSD_CHUNKS_PER_STEP
    nt = seqlen // q
    hist = SSM_CONV - 1
    lane_vec = pl.BlockSpec((None, 1, LANE), lambda b, t: (layer, 0, 0))
    return pl.pallas_call(
        _ssd_kernel,
        grid=(batch, nt),
        in_specs=[
            pl.BlockSpec((q, SSM_DIM), lambda b, t: (b * nt + t, COL_Z // SSM_DIM)),
            pl.BlockSpec((q, SSM_CONV_DIM), lambda b, t: (b * nt + t, COL_XBC // SSM_CONV_DIM)),
            pl.BlockSpec((q, LANE), lambda b, t: (b * nt + t, (COL_SMALL + SM_DT) // LANE)),
            pl.BlockSpec((None, 1, hist, SSM_CONV_DIM), lambda b, t: (layer, b, 0, 0)),
            pl.BlockSpec((None, 1, SSM_HEADS, SSM_HEADDIM, SSM_STATE), lambda b, t: (layer, b, 0, 0, 0)),
            pl.BlockSpec((None, SSM_CONV, SSM_CONV_DIM), lambda b, t: (layer, 0, 0)),
            pl.BlockSpec((None, 1, SSM_CONV_DIM), lambda b, t: (layer, 0, 0)),
            lane_vec, lane_vec, lane_vec,
            pl.BlockSpec((None, 1, SSM_DIM), lambda b, t: (layer, 0, 0)),
        ],
        out_specs=[
            pl.BlockSpec((q, SSM_DIM), lambda b, t: (b * nt + t, 0)),
            pl.BlockSpec((1, hist, SSM_CONV_DIM), lambda b, t: (b, 0, 0)),
            pl.BlockSpec((1, SSM_HEADS, SSM_HEADDIM, SSM_STATE), lambda b, t: (b, 0, 0, 0)),
        ],
        out_shape=[
            jax.ShapeDtypeStruct((batch * seqlen, SSM_DIM), F32),
            jax.ShapeDtypeStruct((batch, hist, SSM_CONV_DIM), F32),
            jax.ShapeDtypeStruct((batch, SSM_HEADS, SSM_HEADDIM, SSM_STATE), F32),
        ],
        scratch_shapes=[
            pltpu.VMEM((SUBLANE, SSM_CONV_DIM), F32),
            pltpu.VMEM((SSM_HEADS, SSM_HEADDIM, SSM_STATE), F32),
            pltpu.VMEM((q, SSM_DIM), F32),
            pltpu.VMEM((LANE, SSM_DIM), MXU_DTYPE),
            pltpu.VMEM((LANE, SSM_HEADS * SSD_CHUNK), MXU_DTYPE),
        ],
        compiler_params=_params("parallel", "arbitrary"),
        name="ssd_chunk",
    )(proj, proj, proj, buf0, h0, wp["ssm_conv_w"], wp["ssm_conv_b"], wp["dt_bias_pad"], wp["a_log_pad"],
      wp["d_pad"], wp["ssm_norm"])


def _lane_sums(x, ones):
    hi = x.astype(MXU_DTYPE)
    mid = (x - hi.astype(F32)).astype(MXU_DTYPE)
    return jnp.dot(hi, ones, preferred_element_type=F32) + jnp.dot(mid, ones, preferred_element_type=F32)


def _row_sums(sel, x):
    hi = x.astype(MXU_DTYPE)
    mid = (x - hi.astype(F32)).astype(MXU_DTYPE)
    return jnp.dot(sel, hi, preferred_element_type=F32) + jnp.dot(sel, mid, preferred_element_type=F32)


def _head_sums_rows(x, ones_pair):
    return jnp.concatenate([_lane_sums(x[:, q * LANE:(q + 1) * LANE], ones_pair)
                            for q in range(RWKV_DIM // LANE)], axis=1)


def _step_kernel(p_rkv_ref, p_sm_ref, p_sc_ref, z_ref, xbc_ref,
                 sh_rkv_ref, sh_sm_ref, wkv_ref, scbuf_ref, ssmbuf_ref, ssm_ref,
                 mu_rkv_ref, mu_sm_ref, w0_ref, w2_ref, a0_ref, a2_ref, g2_ref, kk_ref, ka_ref,
                 rk_ref, lnw_ref, lnb_ref,
                 scw_ref, scg_ref, cw_ref, cb_ref, dtb_ref, alog_ref, dskip_ref, ssmg_ref,
                 y_rwkv_ref, y_sc_ref, y_ssm_ref, wkv_out_ref, scbuf_out_ref, ssmbuf_out_ref, ssm_out_ref,
                 even_s, odd_s, yrow_s, yssm_s):
    n = RWKV_HEAD
    half = n // 2
    nseq = p_rkv_ref.shape[1]
    seqs = range(nseq)
    i128 = lambda shape, d: lax.broadcasted_iota(jnp.int32, shape, d)
    ones_pair = (i128((LANE, LANE), 0) // n == i128((LANE, LANE), 1) // n).astype(MXU_DTYPE)
    ones_full = jnp.ones((LANE, LANE), MXU_DTYPE)
    diag2 = i128((half, LANE), 1) % n == 2 * i128((half, LANE), 0) + i128((half, LANE), 1) // n
    diag_lo = i128((n, LANE), 1) == i128((n, LANE), 0)
    diag_hi = i128((n, LANE), 1) == i128((n, LANE), 0) + n
    upper_row = (i128((nseq, RWKV_DIM), 1) // n) % 2 == 1
    upper_lane = i128((1, LANE), 1) // n == 1

    wts = (mu_rkv_ref[...], mu_sm_ref[...], w0_ref[...], w2_ref[...], a0_ref[...], a2_ref[...],
           g2_ref[...], kk_ref[...], ka_ref[...])
    r, log_decay, k_mod, v, kk, a, g = _rwkv_prepare(p_rkv_ref[0], sh_rkv_ref[0], p_sm_ref[0], sh_sm_ref[0], wts)
    kk = kk * lax.rsqrt(jnp.maximum(_head_sums_rows(kk * kk, ones_pair), 1e-24))
    bonus = _head_sums_rows(r * k_mod * rk_ref[...], ones_pair)
    for i, x in enumerate((-kk, jnp.exp(log_decay), kk * a, k_mod, r, v)):
        even_s[i] = jnp.where(upper_row, pltpu.roll(x, n, 1), x)
        odd_s[i] = jnp.where(upper_row, x, pltpu.roll(x, RWKV_DIM - n, 1))

    def head_vec(s, h, i):
        src = odd_s if h % 2 else even_s
        return src[i, s:s + 1, (h // 2) * LANE:(h // 2 + 1) * LANE]

    heads = range(RWKV_HEADS)
    sh = [(s, h) for s in seqs for h in heads]
    rs = lambda x, s, h: x[(s * RWKV_HEADS + h) * half:(s * RWKV_HEADS + h + 1) * half]
    sa = _lane_sums(jnp.concatenate([wkv_ref[s, h] * head_vec(s, h, 0) for s, h in sh], axis=0), ones_pair)
    v_col = _lane_sums(jnp.concatenate([jnp.where(diag2, head_vec(s, h, 5), 0.0) for s, h in sh], axis=0),
                       ones_pair)
    for s, h in sh:
        wkv_out_ref[s, h] = (wkv_ref[s, h] * head_vec(s, h, 1) + rs(sa, s, h) * head_vec(s, h, 2)
                             + rs(v_col, s, h) * head_vec(s, h, 3))
    y_b = _lane_sums(jnp.concatenate([wkv_out_ref[s, h] * head_vec(s, h, 4) for s, h in sh], axis=0), ones_pair)
    y_d = jnp.concatenate([jnp.where(diag2, rs(y_b, s, h), 0.0) for s, h in sh], axis=0)
    nrow = nseq * RWKV_HEADS
    sel = (i128((nrow, nrow * half), 1) // half == i128((nrow, nrow * half), 0)).astype(MXU_DTYPE)
    y_h = _row_sums(sel, y_d)
    y_h = y_h + pltpu.roll(y_h, n, 1)
    for s in seqs:
        base = s * RWKV_HEADS
        yrow_s[s:s + 1, :] = jnp.concatenate(
            [jnp.where(upper_lane, y_h[base + 2 * q + 1:base + 2 * q + 2], y_h[base + 2 * q:base + 2 * q + 1])
             for q in range(RWKV_HEADS // 2)], axis=1)
    y = yrow_s[...]
    mean = _head_sums_rows(y, ones_pair) * (1.0 / n)
    yc = y - mean
    var = _head_sums_rows(yc * yc, ones_pair) * (1.0 / n)
    y_rwkv_ref[0] = (yc * lax.rsqrt(var + RWKV_GN_EPS) * lnw_ref[...] + lnb_ref[...] + bonus * v) * g

    p_sc = p_sc_ref[0]
    u = p_sc[:, SC_DIM:2 * SC_DIM] * p_sc[:, 2 * SC_DIM:]
    buf0 = scbuf_ref[0][:, :SC_DIM]
    buf1 = scbuf_ref[0][:, SC_DIM:]
    conv = buf0 * scw_ref[0:1, :] + buf1 * scw_ref[1:2, :] + u * scw_ref[2:3, :]
    y_sc_ref[0] = _rmsnorm(p_sc[:, :SC_DIM] * conv, scg_ref[...])
    scbuf_out_ref[0, :, :SC_DIM] = buf1
    scbuf_out_ref[0, :, SC_DIM:] = u

    xbc = xbc_ref[0]
    cbuf = ssmbuf_ref[0]
    conv = xbc * cw_ref[3:4, :]
    for j in range(SSM_CONV - 1):
        conv += cbuf[:, j * SSM_CONV_DIM:(j + 1) * SSM_CONV_DIM] * cw_ref[j:j + 1, :]
    ssmbuf_out_ref[0, :, :2 * SSM_CONV_DIM] = cbuf[:, SSM_CONV_DIM:]
    ssmbuf_out_ref[0, :, 2 * SSM_CONV_DIM:] = xbc
    xc = _silu(conv + cb_ref[...])
    xh = xc[:, :SSM_DIM]
    dt = _softplus(p_sm_ref[0][:, SM_DT:SM_DT + LANE] + dtb_ref[...])
    decay_all = jnp.exp(dt * (-jnp.exp(alog_ref[...])))
    head_of_lane = i128((nseq, SSM_DIM), 1) // SSM_HEADDIM
    dt_exp = jnp.zeros((nseq, SSM_DIM), F32)
    d_exp = jnp.zeros((nseq, SSM_DIM), F32)
    for h in range(SSM_HEADS):
        dt_exp = jnp.where(head_of_lane == h, dt[:, h:h + 1], dt_exp)
        d_exp = jnp.where(head_of_lane == h, dskip_ref[:, h:h + 1], d_exp)
    xdt = xh * dt_exp
    heads = range(SSM_HEADS)
    sh = [(s, h) for s in seqs for h in heads]
    heads_per_group = SSM_HEADS // SSM_GROUPS
    bm = lambda s, h: xc[s:s + 1, SSM_DIM + (h // heads_per_group) * SSM_STATE:
                         SSM_DIM + (h // heads_per_group + 1) * SSM_STATE]
    cm = lambda s, h: xc[s:s + 1, SSM_DIM + (SSM_GROUPS + h // heads_per_group) * SSM_STATE:
                         SSM_DIM + (SSM_GROUPS + h // heads_per_group + 1) * SSM_STATE]
    diag = lambda h: diag_hi if h % 2 else diag_lo
    slab = lambda x, s, h: x[s:s + 1, (h // 2) * LANE:(h // 2 + 1) * LANE]
    rs = lambda x, s, h: x[(s * SSM_HEADS + h) * SSM_HEADDIM:(s * SSM_HEADS + h + 1) * SSM_HEADDIM]
    xdt_col = _lane_sums(jnp.concatenate([jnp.where(diag(h), slab(xdt, s, h), 0.0) for s, h in sh], axis=0),
                         ones_full)
    for s, h in sh:
        ssm_out_ref[s, h] = ssm_ref[s, h] * decay_all[s:s + 1, h:h + 1] + rs(xdt_col, s, h) * bm(s, h)
    y_b = _lane_sums(jnp.concatenate([ssm_out_ref[s, h] * cm(s, h) for s, h in sh], axis=0), ones_full)
    y_d = jnp.concatenate([jnp.where(diag(h), rs(y_b, s, h), 0.0) for s, h in sh], axis=0)
    nrow = nseq * SSM_HEADS
    sel = (i128((nrow, nrow * SSM_HEADDIM), 1) // SSM_HEADDIM == i128((nrow, nrow * SSM_HEADDIM), 0)).astype(MXU_DTYPE)
    y_h = _row_sums(sel, y_d)
    for s in seqs:
        base = s * SSM_HEADS
        yssm_s[s:s + 1, :] = jnp.concatenate(
            [y_h[base + 2 * q:base + 2 * q + 1] + y_h[base + 2 * q + 1:base + 2 * q + 2]
             for q in range(SSM_HEADS // 2)], axis=1)
    y = (yssm_s[...] + d_exp * xh) * _silu(z_ref[0])
    gw = SSM_DIM // SSM_GROUPS
    for grp in range(SSM_GROUPS):
        yg = y[:, grp * gw:(grp + 1) * gw]
        y_ssm_ref[0, :, grp * gw:(grp + 1) * gw] = _rmsnorm(yg, ssmg_ref[:, grp * gw:(grp + 1) * gw])


def _mixers_step(proj, states, wp, layer, *, batch):
    shift_rkv, shift_sm, wkv2, sc_buf, ssm_buf, ssm = states
    depth = ssm.shape[0]
    nseq = _tile(batch, STEP_SEQS)
    nb = batch // nseq
    p3 = proj.reshape(nb, nseq, PROJ_COLS)
    grp = lambda a: a.reshape((depth, nb, nseq, -1))
    row = lambda width, blk: pl.BlockSpec((1, nseq, width), lambda b: (b, 0, blk))
    srow = lambda width: pl.BlockSpec((None, 1, nseq, width), lambda b: (layer, b, 0, 0))
    st4 = lambda d1, d2, d3: pl.BlockSpec((nseq, d1, d2, d3), lambda b: (b, 0, 0, 0))
    sst4 = lambda d1, d2, d3: pl.BlockSpec((None, nseq, d1, d2, d3), lambda b: (layer, b, 0, 0, 0))
    vec = lambda width: pl.BlockSpec((None, 1, width), lambda b: (layer, 0, 0))
    mat = lambda rows, width: pl.BlockSpec((None, rows, width), lambda b: (layer, 0, 0))
    sc_w = (SC_WIDTH - 1) * SC_DIM
    ssm_w = (SSM_CONV - 1) * SSM_CONV_DIM
    outs = pl.pallas_call(
        _step_kernel,
        grid=(nb,),
        in_specs=[
            row(3 * RWKV_DIM, COL_RKV // (3 * RWKV_DIM)),
            row(SMALL_W, COL_SMALL // SMALL_W),
            row(3 * SC_DIM, COL_SC // (3 * SC_DIM)),
            row(SSM_DIM, COL_Z // SSM_DIM),
            row(SSM_CONV_DIM, COL_XBC // SSM_CONV_DIM),
            srow(3 * RWKV_DIM), srow(SMALL_W),
            sst4(RWKV_HEADS, RWKV_HEAD // 2, LANE),
            srow(sc_w), srow(ssm_w),
            sst4(SSM_HEADS, SSM_HEADDIM, SSM_STATE),
        ] + [vec(3 * RWKV_DIM), vec(SMALL_W), vec(RWKV_DIM), mat(DECAY_LORA, RWKV_DIM), vec(RWKV_DIM),
             mat(AAA_LORA, RWKV_DIM), mat(GATE_LORA, RWKV_DIM), vec(RWKV_DIM), vec(RWKV_DIM), vec(RWKV_DIM),
             vec(RWKV_DIM), vec(RWKV_DIM),
             mat(SC_WIDTH, SC_DIM), vec(SC_DIM), mat(SSM_CONV, SSM_CONV_DIM), vec(SSM_CONV_DIM),
             vec(LANE), vec(LANE), vec(LANE), vec(SSM_DIM)],
        out_specs=[
            row(RWKV_DIM, 0), row(SC_DIM, 0), row(SSM_DIM, 0),
            st4(RWKV_HEADS, RWKV_HEAD // 2, LANE),
            row(sc_w, 0), row(ssm_w, 0),
            st4(SSM_HEADS, SSM_HEADDIM, SSM_STATE),
        ],
        out_shape=[
            jax.ShapeDtypeStruct((nb, nseq, RWKV_DIM), F32),
            jax.ShapeDtypeStruct((nb, nseq, SC_DIM), F32),
            jax.ShapeDtypeStruct((nb, nseq, SSM_DIM), F32),
            jax.ShapeDtypeStruct(wkv2.shape[1:], F32),
            jax.ShapeDtypeStruct((nb, nseq, sc_w), F32),
            jax.ShapeDtypeStruct((nb, nseq, ssm_w), F32),
            jax.ShapeDtypeStruct(ssm.shape[1:], F32),
        ],
        scratch_shapes=[pltpu.VMEM((6, nseq, RWKV_DIM), F32), pltpu.VMEM((6, nseq, RWKV_DIM), F32),
                        pltpu.VMEM((nseq, RWKV_DIM), F32), pltpu.VMEM((nseq, SSM_DIM), F32)],
        compiler_params=_params("parallel"),
        name="mixers_step",
    )(p3, p3, p3, p3, p3, grp(shift_rkv), grp(shift_sm), wkv2, grp(sc_buf), grp(ssm_buf), ssm,
      *_rwkv_weight_args(wp), wp["sc_conv_w"], wp["sc_norm"], wp["ssm_conv_w"], wp["ssm_conv_b"],
      wp["dt_bias_pad"], wp["a_log_pad"], wp["d_pad"], wp["ssm_norm"])
    y_rwkv, y_sc, y_ssm, wkv_n, sc_n, ssmbuf_n, ssm_n = outs
    return (y_rwkv.reshape(batch, RWKV_DIM), y_sc.reshape(batch, SC_DIM), y_ssm.reshape(batch, SSM_DIM),
            wkv_n, sc_n.reshape(sc_buf.shape[1:]), ssmbuf_n.reshape(ssm_buf.shape[1:]), ssm_n)


def _reorder_cols(a):
    lead = a.shape[:-1]
    rkv = a[..., :3 * RWKV_DIM]
    lora = a[..., 3 * RWKV_DIM:RWKV_PROJ]
    small = jnp.concatenate([lora, jnp.zeros(lead + (SMALL_W - lora.shape[-1],), a.dtype)], axis=-1)
    return rkv, small


def _prepare_weights(w):
    depth = w["w_in"].shape[0]
    wp = {}
    w_in = w["w_in"]
    o_ssm = RWKV_PROJ + 3 * SC_DIM
    o_dt = o_ssm + SSM_DIM + SSM_CONV_DIM
    piece = lambda lo, hi: w_in[..., lo:hi].astype(MXU_DTYPE)
    zeros = lambda width: jnp.zeros(w_in.shape[:-1] + (width,), MXU_DTYPE)
    wp["w_all"] = jnp.concatenate(
        [piece(0, 3 * RWKV_DIM), piece(RWKV_PROJ, o_dt),
         piece(3 * RWKV_DIM, RWKV_PROJ), zeros(SM_DT - (RWKV_PROJ - 3 * RWKV_DIM)),
         piece(o_dt, o_dt + SSM_HEADS), zeros(SMALL_W - SM_DT - SSM_HEADS)],
        axis=-1)
    mu_rkv, mu_sm = _reorder_cols(w["rwkv_mu"])
    wp["mu_rkv"] = mu_rkv.reshape(depth, 1, -1)
    wp["mu_sm"] = mu_sm.reshape(depth, 1, -1)
    for name in ("rwkv_w0", "rwkv_a0", "rwkv_k_k", "rwkv_k_a", "rwkv_ln_w", "rwkv_ln_b", "sc_norm",
                 "ssm_conv_b", "ssm_norm"):
        wp[name] = w[name].reshape(depth, 1, -1)
    wp["rwkv_r_k"] = w["rwkv_r_k"].reshape(depth, 1, RWKV_DIM)
    for name in ("rwkv_w2", "rwkv_a2", "rwkv_g2"):
        wp[name] = w[name].astype(MXU_DTYPE)
    wp["sc_conv_w"] = w["sc_conv_w"]
    wp["ssm_conv_w"] = w["ssm_conv_w"]
    pad_lane = lambda a: jnp.pad(a, ((0, 0), (0, LANE - a.shape[-1]))).reshape(depth, 1, LANE)
    wp["dt_bias_pad"] = pad_lane(w["ssm_dt_bias"])
    wp["a_log_pad"] = pad_lane(w["ssm_A_log"])
    wp["d_pad"] = pad_lane(w["ssm_D"])
    for name in ("ffn1_w_in", "ffn1_w_out", "ffn2_w_in", "ffn2_w_out", "w_out"):
        wp[name] = w[name].astype(MXU_DTYPE)
    for name in ("norm_ffn1", "norm_mix", "norm_ffn2"):
        wp[name] = w[name]
    return wp


def _shift_state_to_cols(proj_last):
    return jnp.concatenate([proj_last[..., :3 * RWKV_DIM],
                            proj_last[..., COL_SMALL:COL_SMALL + RWKV_PROJ - 3 * RWKV_DIM]], axis=-1)


def _tile(m, pref):
    t = min(m, pref)
    while m % t:
        t //= 2
    return t


def _tile_plan(m, seqlen):
    return dict(
        ffn_rows=_tile(m, 512), ffn_cols=D_FF // 2,
        proj_rows=_tile(m, 1024), proj_cols=PROJ_COLS // 4,
        out_rows=_tile(m, 1024),
        conv_rows=_tile(seqlen, 1024),
    )


def _trunk(x3, states, wp, norm_final):
    batch, seqlen, _ = x3.shape
    m = batch * seqlen
    x = x3.reshape(m, D_MODEL)
    depth = wp["w_all"].shape[0]
    tiles = _tile_plan(m, seqlen)
    new = ([], [], [], [], [])
    shift, wkv, sc_buf, ssm_buf, ssm = states
    shift_rkv, shift_sm = _reorder_cols(shift)
    shift_rkv = shift_rkv.reshape(depth, batch, 1, -1)
    shift_sm = shift_sm.reshape(depth, batch, 1, -1)
    if seqlen == 1:
        wkv = wkv.reshape(depth, batch, RWKV_HEADS, RWKV_HEAD // 2, LANE)
    for layer in range(depth):
        x = _ffn(x, wp["norm_ffn1"], wp["ffn1_w_in"], wp["ffn1_w_out"], layer,
                 tm=tiles["ffn_rows"], tf=tiles["ffn_cols"])
        proj = _proj_in(x, wp["norm_mix"], wp["w_all"], layer, tm=tiles["proj_rows"], tn=tiles["proj_cols"])
        if seqlen == 1:
            y_rwkv, y_sc, y_ssm, wkv_n, sc_n, ssmbuf_n, ssm_n = _mixers_step(
                proj, (shift_rkv, shift_sm, wkv, sc_buf, ssm_buf, ssm), wp, layer, batch=batch)
        else:
            y_rwkv, wkv_n = _rwkv_prompt(proj, shift_rkv, shift_sm, wkv, wp, layer, batch=batch, seqlen=seqlen)
            y_sc, sc_n = _sc_prompt(proj, sc_buf, wp, layer, batch=batch, seqlen=seqlen,
                                    tl=tiles["conv_rows"])
            y_ssm, ssmbuf_n, ssm_n = _ssd_prompt(proj, ssm_buf, ssm, wp, layer, batch=batch, seqlen=seqlen)
        shift_n = _shift_state_to_cols(proj.reshape(batch, seqlen, PROJ_COLS)[:, -1])
        x = _proj_out(x, y_rwkv, y_sc, y_ssm, wp["w_out"], layer, tm=tiles["out_rows"])
        x = _ffn(x, wp["norm_ffn2"], wp["ffn2_w_in"], wp["ffn2_w_out"], layer,
                 final_g=norm_final if layer == depth - 1 else None,
                 tm=tiles["ffn_rows"], tf=tiles["ffn_cols"])
        for lst, s in zip(new, (shift_n, wkv_n, sc_n, ssmbuf_n, ssm_n)):
            lst.append(s)
    new = [jnp.stack(lst) for lst in new]
    new[1] = new[1].reshape(depth, batch, RWKV_HEADS, RWKV_HEAD, RWKV_HEAD)
    return x.reshape(batch, seqlen, D_MODEL), tuple(new)


def kernel(x_prompt, x_sample, state_rwkv_shift, state_rwkv_wkv, state_sc_buf, state_ssm_conv, state_ssm,
           norm_ffn1, ffn1_w_in, ffn1_w_out, norm_mix, w_in, rwkv_mu, rwkv_w0, rwkv_w2, rwkv_a0, rwkv_a2,
           rwkv_g2, rwkv_k_k, rwkv_k_a, rwkv_r_k, rwkv_ln_w, rwkv_ln_b, sc_conv_w, sc_norm, ssm_conv_w,
           ssm_conv_b, ssm_dt_bias, ssm_A_log, ssm_D, ssm_norm, w_out, norm_ffn2, ffn2_w_in, ffn2_w_out,
           norm_final):
    weights = dict(
        norm_ffn1=norm_ffn1, ffn1_w_in=ffn1_w_in, ffn1_w_out=ffn1_w_out, norm_mix=norm_mix, w_in=w_in,
        rwkv_mu=rwkv_mu, rwkv_w0=rwkv_w0, rwkv_w2=rwkv_w2, rwkv_a0=rwkv_a0, rwkv_a2=rwkv_a2, rwkv_g2=rwkv_g2,
        rwkv_k_k=rwkv_k_k, rwkv_k_a=rwkv_k_a, rwkv_r_k=rwkv_r_k, rwkv_ln_w=rwkv_ln_w, rwkv_ln_b=rwkv_ln_b,
        sc_conv_w=sc_conv_w, sc_norm=sc_norm, ssm_conv_w=ssm_conv_w, ssm_conv_b=ssm_conv_b,
        ssm_dt_bias=ssm_dt_bias, ssm_A_log=ssm_A_log, ssm_D=ssm_D, ssm_norm=ssm_norm, w_out=w_out,
        norm_ffn2=norm_ffn2, ffn2_w_in=ffn2_w_in, ffn2_w_out=ffn2_w_out)
    wp = _prepare_weights(weights)
    depth = w_in.shape[0]
    nb, dt_ = x_prompt.shape[0], x_prompt.dtype
    zero_states = (
        jnp.zeros((depth, nb, RWKV_PROJ), dt_),
        jnp.zeros((depth, nb, RWKV_HEADS, RWKV_HEAD, RWKV_HEAD), dt_),
        jnp.zeros((depth, nb, SC_WIDTH - 1, SC_DIM), dt_),
        jnp.zeros((depth, nb, SSM_CONV - 1, SSM_CONV_DIM), dt_),
        jnp.zeros((depth, nb, SSM_HEADS, SSM_HEADDIM, SSM_STATE), dt_),
    )
    y_prompt, p_states = _trunk(x_prompt, zero_states, wp, norm_final)
    sample_states = (state_rwkv_shift, state_rwkv_wkv, state_sc_buf, state_ssm_conv, state_ssm)
    y_sample, s_states = _trunk(x_sample, sample_states, wp, norm_final)
    return (y_prompt, y_sample) + tuple(p_states) + tuple(s_states)
```
